```python
import math
import jax
import jax.numpy as jnp
from jax import lax
import numpy as np

D_MODEL = 2048
BATCH = 2
SEQ = 4096
DEPTH = 4
DEC_BATCH = 8
DEC_SEQ = 4
PAST_LEN = 16384
PAGE_SIZE = 128

N_AB_LAYERS = (DEPTH + 1) // 2
N_C_LAYERS = DEPTH // 2

A_HEADS = 8
A_HEAD_DIM = 128
A_DIM = A_HEADS * A_HEAD_DIM
A_GROUPS = ((128, 1), (512, 4), (2048, 16))
A_WINDOW_MAX = 2048
A_BLOCK = 128
ROPE_THETA = 10000.0

B_HEADS = 8
B_HEAD_DIM = 128
B_DIM = B_HEADS * B_HEAD_DIM
B_CONV = 4
B_CHUNK = 64

C_HEADS = 16
C_EXPAND = 128
C_HEAD_DIM = D_MODEL // C_HEADS
C_KEY_DIM = C_HEADS * C_EXPAND
C_VAL_DIM = C_HEADS * C_HEAD_DIM
C_CHUNK = 32

D_FF = 5504
FFN_RESIDUAL = 0.5
N_MOD = 9
NORM_EPS = 1e-6

AB_PROJ = 3 * A_DIM + 3 * B_DIM + 2 * B_HEADS + B_DIM
C_PROJ = 2 * C_KEY_DIM + 2 * C_VAL_DIM

kernel_name = "hybrid_dilated_gdn_hgrn2_macaron_step"


def rms_norm(x, g):
    xf = x.astype(jnp.float32)
    y = xf * lax.rsqrt(jnp.mean(xf * xf, axis=-1, keepdims=True) + NORM_EPS)
    return (y * g.astype(jnp.float32)).astype(x.dtype)


def l2_normalize(x):
    xf = x.astype(jnp.float32)
    return xf * lax.rsqrt(jnp.sum(xf * xf, axis=-1, keepdims=True) + 1e-6)


def modulate(h, shift, scale):
    return h * (1.0 + scale[:, None, :]) + shift[:, None, :]


def apply_rope(x, pos):
    half = x.shape[-1] // 2
    inv_freq = jnp.power(ROPE_THETA, -jnp.arange(half, dtype=jnp.float32) / half)
    ang = pos.astype(jnp.float32)[:, None] * inv_freq[None, :]
    cos = jnp.cos(ang)[:, None, :]
    sin = jnp.sin(ang)[:, None, :]
    xf = x.astype(jnp.float32)
    x1, x2 = xf[..., :half], xf[..., half:]
    return jnp.concatenate([x1 * cos - x2 * sin, x2 * cos + x1 * sin], axis=-1).astype(x.dtype)


def _to_chunks(x, c):
    bsz, t = x.shape[0], x.shape[1]
    nc = -(-t // c)
    x = jnp.pad(x, ((0, 0), (0, nc * c - t)) + ((0, 0),) * (x.ndim - 2))
    x = x.reshape((bsz, nc, c) + x.shape[2:])
    return jnp.swapaxes(jnp.moveaxis(x, 1, 0), 2, 3)


def _from_chunks(o, t):
    nc, bsz, h, c, d = o.shape
    o = jnp.moveaxis(jnp.swapaxes(o, 2, 3), 0, 1)
    return o.reshape(bsz, nc * c, h, d)[:, :t]


def _strided_band_attention(q, k, v, dil, span):
    bsz, t, h, hd = q.shape
    length = t // dil
    nb = -(-length // A_BLOCK)
    lp = nb * A_BLOCK

    def to_blocks(x):
        x = x.astype(jnp.float32).reshape(bsz, length, dil, h, hd).transpose(0, 2, 3, 1, 4)
        x = jnp.pad(x, ((0, 0), (0, 0), (0, 0), (0, lp - length), (0, 0)))
        return x.reshape(bsz, dil, h, nb, A_BLOCK, hd)

    def with_prev(x):
        prev = jnp.pad(x, ((0, 0), (0, 0), (0, 0), (1, 0), (0, 0), (0, 0)))[:, :, :, :-1]
        return jnp.concatenate([prev, x], axis=4)

    qb = to_blocks(q)
    kb = with_prev(to_blocks(k))
    vb = with_prev(to_blocks(v))
    s = jnp.einsum('brhnqe,brhnke->brhnqk', qb, kb) * (hd ** -0.5)
    qi = jnp.arange(A_BLOCK)[:, None]
    ki = jnp.arange(2 * A_BLOCK)[None, :]
    rel = qi + A_BLOCK - ki
    blk = jnp.arange(nb)[:, None, None]
    valid = (rel >= 0) & (rel <= span) & (blk * A_BLOCK + ki >= A_BLOCK)
    s = jnp.where(valid, s, -jnp.inf)
    m = jnp.max(s, axis=-1, keepdims=True)
    p = jnp.exp(s - m)
    den = jnp.sum(p, axis=-1, keepdims=True)
    o = jnp.einsum('brhnqk,brhnke->brhnqe', p, vb) / den
    lse = (m + jnp.log(den))[..., 0]
    o = o.reshape(bsz, dil, h, lp, hd)[:, :, :, :length].transpose(0, 3, 1, 2, 4).reshape(bsz, t, h, hd)
    lse = lse.reshape(bsz, dil, h, lp)[:, :, :, :length].transpose(0, 3, 1, 2).reshape(bsz, t, h)
    return o, lse


def _gathered_attention(q, k_all, v_all, n_buf, dil, span):
    tn, hd = q.shape[1], q.shape[-1]
    idx = n_buf + jnp.arange(tn)[:, None] - dil * jnp.arange(span + 1)[None, :]
    valid = idx >= 0
    idx = jnp.maximum(idx, 0)
    kg = k_all[:, idx].astype(jnp.float32)
    vg = v_all[:, idx].astype(jnp.float32)
    s = jnp.einsum('bqhe,bqkhe->bhqk', q.astype(jnp.float32), kg) * (hd ** -0.5)
    s = jnp.where(valid, s, -jnp.inf)
    m = jnp.max(s, axis=-1, keepdims=True)
    p = jnp.exp(s - m)
    den = jnp.sum(p, axis=-1, keepdims=True)
    o = jnp.einsum('bhqk,bqkhe->bqhe', p, vg) / jnp.swapaxes(den, 1, 2)
    lse = jnp.swapaxes((m + jnp.log(den))[..., 0], 1, 2)
    return o, lse


def _combine_groups(outs, lses):
    w = jax.nn.softmax(jnp.stack(lses, axis=-1), axis=-1)
    return jnp.einsum('bthg,bthge->bthe', w, jnp.stack(outs, axis=3))


def dilated_attention_prompt(q, k, v):
    outs, lses = [], []
    for window, dil in A_GROUPS:
        o, lse = _strided_band_attention(q, k, v, dil, window // dil)
        outs.append(o)
        lses.append(lse)
    return _combine_groups(outs, lses)


def dilated_attention_sample(q, k_all, v_all, n_buf):
    outs, lses = [], []
    for window, dil in A_GROUPS:
        o, lse = _gathered_attention(q, k_all, v_all, n_buf, dil, window // dil)
        outs.append(o)
        lses.append(lse)
    return _combine_groups(outs, lses)


def short_conv(x, buf, w):
    t = x.shape[1]
    xp = jnp.concatenate([buf, x], axis=1)
    y = xp[:, 0:t] * w[0]
    for j in range(1, B_CONV):
        y = y + xp[:, j:j + t] * w[j]
    return jax.nn.silu(y), xp[:, xp.shape[1] - (B_CONV - 1):]


def gated_delta_rule(q, k, v, g, beta, s0):
    t = q.shape[1]
    qc, kc, vc = _to_chunks(q, B_CHUNK), _to_chunks(k, B_CHUNK), _to_chunks(v, B_CHUNK)
    gc, bc = _to_chunks(g, B_CHUNK), _to_chunks(beta, B_CHUNK)
    cum = jnp.cumsum(gc, axis=-1)
    ar = jnp.arange(B_CHUNK)
    tri = ar[:, None] >= ar[None, :]
    strict = ar[:, None] > ar[None, :]
    diff = cum[..., :, None] - cum[..., None, :]
    decay = jnp.where(tri, jnp.exp(jnp.where(tri, diff, 0.0)), 0.0)
    kk = jnp.einsum('nbhie,nbhje->nbhij', kc, kc)
    a_mat = jnp.where(strict, bc[..., :, None] * kk * decay, 0.0) + jnp.eye(B_CHUNK, dtype=jnp.float32)
    u = lax.linalg.triangular_solve(a_mat, vc * bc[..., None], left_side=True, lower=True, unit_diagonal=True)
    w = lax.linalg.triangular_solve(a_mat, kc * (bc * jnp.exp(cum))[..., None], left_side=True, lower=True, unit_diagonal=True)
    a_qk = jnp.einsum('nbhie,nbhje->nbhij', qc, kc) * decay
    g_last = cum[..., -1]

    def step(s, xs):
        q_n, k_n, u_n, w_n, c_n, aqk_n, gl_n = xs
        v_new = u_n - jnp.einsum('bhce,bhef->bhcf', w_n, s)
        o_n = (jnp.einsum('bhce,bhef->bhcf', q_n * jnp.exp(c_n)[..., None], s)
               + jnp.einsum('bhij,bhjf->bhif', aqk_n, v_new))
        k_dec = k_n * jnp.exp(gl_n[..., None] - c_n)[..., None]
        s = s * jnp.exp(gl_n)[..., None, None] + jnp.einsum('bhce,bhcf->bhef', k_dec, v_new)
        return s, o_n

    s_final, o = lax.scan(step, s0, (qc, kc, u, w, cum, a_qk, g_last))
    return _from_chunks(o, t), s_final


def gla_chunked(q, k, v, log_f, s0):
    t = q.shape[1]
    qc, kc, vc, lc = (_to_chunks(x, C_CHUNK) for x in (q, k, v, log_f))
    cum = jnp.cumsum(lc, axis=3)
    ar = jnp.arange(C_CHUNK)
    tri = ar[:, None] >= ar[None, :]

    def step(s, xs):
        q_n, k_n, v_n, c_n = xs
        diff = c_n[:, :, :, None, :] - c_n[:, :, None, :, :]
        dec = jnp.exp(jnp.where(tri[:, :, None], diff, -jnp.inf))
        att = jnp.einsum('bhie,bhje,bhije->bhij', q_n, k_n, dec)
        c_last = c_n[:, :, -1:, :]
        o_n = jnp.einsum('bhie,bhef->bhif', q_n * jnp.exp(c_n), s) + jnp.einsum('bhij,bhjf->bhif', att, v_n)
        s = jnp.exp(c_last)[:, :, 0, :, None] * s + jnp.einsum('bhje,bhjf->bhef', k_n * jnp.exp(c_last - c_n), v_n)
        return s, o_n

    s_final, o = lax.scan(step, s0, (qc, kc, vc, cum))
    return _from_chunks(o, t), s_final


def hgrn2_mixer(h, s0, w_in, w_out, lb, norm_w):
    bsz, t, _ = h.shape
    q, f, i, g = jnp.split(h @ w_in, (C_KEY_DIM, 2 * C_KEY_DIM, 2 * C_KEY_DIM + C_VAL_DIM), axis=-1)
    f = f.astype(jnp.float32)
    log_f = jnp.logaddexp(jnp.log(lb), jnp.log1p(-lb) + jax.nn.log_sigmoid(f))
    k = (1.0 - lb) * jax.nn.sigmoid(-f)
    kh = lambda x: x.reshape(bsz, t, C_HEADS, C_EXPAND)
    vh = lambda x: x.reshape(bsz, t, C_HEADS, C_HEAD_DIM)
    q = q.astype(jnp.float32) * (C_EXPAND ** -0.5)
    o, s_new = gla_chunked(kh(q), kh(k), vh(i.astype(jnp.float32)), kh(log_f), s0.astype(jnp.float32))
    o = rms_norm(o, norm_w) * jax.nn.silu(vh(g.astype(jnp.float32)))
    return o.reshape(bsz, t, C_VAL_DIM).astype(h.dtype) @ w_out, s_new


def ab_mixer(h, pos, kv_past, s0, conv_buf, w_in, w_out, conv_w, a_log, dt_bias, norm_w):
    bsz, t, _ = h.shape
    bounds = (A_DIM, 2 * A_DIM, 3 * A_DIM, 3 * A_DIM + 3 * B_DIM,
              3 * A_DIM + 3 * B_DIM + B_HEADS, 3 * A_DIM + 3 * B_DIM + 2 * B_HEADS)
    qa, ka, va, qkv_b, a_b, b_b, z_b = jnp.split(h @ w_in, bounds, axis=-1)
    heads_a = lambda x: x.reshape(bsz, t, A_HEADS, A_HEAD_DIM)
    qa = apply_rope(heads_a(qa), pos)
    ka = apply_rope(heads_a(ka), pos)
    va = heads_a(va)
    if kv_past is None:
        o_a = dilated_attention_prompt(qa, ka, va)
        keep = min(A_WINDOW_MAX, t)
        k_rows, v_rows = ka[:, t - keep:], va[:, t - keep:]
    else:
        n_buf = kv_past[0].shape[1]
        k_all = jnp.concatenate([kv_past[0].astype(ka.dtype), ka], axis=1)
        v_all = jnp.concatenate([kv_past[1].astype(va.dtype), va], axis=1)
        o_a = dilated_attention_sample(qa, k_all, v_all, n_buf)
        k_rows, v_rows = ka, va

    act, buf_new = short_conv(qkv_b, conv_buf.astype(qkv_b.dtype), conv_w)
    qb, kb, vb = jnp.split(act, 3, axis=-1)
    heads_b = lambda x: x.reshape(bsz, t, B_HEADS, B_HEAD_DIM)
    qb = l2_normalize(heads_b(qb)) * (B_HEAD_DIM ** -0.5)
    kb = l2_normalize(heads_b(kb))
    g = -jnp.exp(a_log.astype(jnp.float32)) * jax.nn.softplus(a_b.astype(jnp.float32) + dt_bias.astype(jnp.float32))
    beta = jax.nn.sigmoid(b_b.astype(jnp.float32))
    o_b, s_new = gated_delta_rule(qb, kb, heads_b(vb).astype(jnp.float32), g, beta, s0.astype(jnp.float32))
    o_b = rms_norm(o_b, norm_w) * jax.nn.silu(heads_b(z_b.astype(jnp.float32)))

    mixed = jnp.concatenate([o_a.reshape(bsz, t, A_DIM), o_b.reshape(bsz, t, B_DIM)], axis=-1).astype(h.dtype)
    return mixed @ w_out, (k_rows, v_rows, s_new, buf_new)


def half_ffn(x, shift, scale, gate, g_pre, g_post, wi, wo):
    h = modulate(rms_norm(x, g_pre), shift, scale)
    u, v = jnp.split(h @ wi, 2, axis=-1)
    out = (jax.nn.silu(v) * u) @ wo
    return x + FFN_RESIDUAL * gate[:, None, :] * rms_norm(out, g_post)


def trunk(x, c, pos, past_a_k, past_a_v, past_b_s, past_b_conv, past_c_s,
          ada_w, ada_b, norm_pre, norm_post, ffn_wi, ffn_wo,
          ab_w_in, ab_w_out, b_conv_w, b_a_log, b_dt_bias, b_norm,
          c_w_in, c_w_out, c_lower_bounds, c_norm):
    sample = past_a_k is not None
    bsz = x.shape[0]
    lb_all = jnp.cumsum(jax.nn.softmax(c_lower_bounds.astype(jnp.float32), axis=0), axis=0)
    lb_all = lb_all - lb_all[0:1]
    a_k, a_v, b_s, b_conv, c_s = [], [], [], [], []
    for layer in range(DEPTH):
        mods = jnp.split(jax.nn.silu(c) @ ada_w[layer] + ada_b[layer], N_MOD, axis=-1)
        x = half_ffn(x, mods[0], mods[1], mods[2], norm_pre[layer, 0], norm_post[layer, 0],
                     ffn_wi[layer, 0], ffn_wo[layer, 0])
        h = modulate(rms_norm(x, norm_pre[layer, 1]), mods[3], mods[4])
        j = layer // 2
        if layer % 2 == 0:
            if sample:
                kv_past, s0, buf = (past_a_k[j], past_a_v[j]), past_b_s[j], past_b_conv[j]
            else:
                kv_past = None
                s0 = jnp.zeros((bsz, B_HEADS, B_HEAD_DIM, B_HEAD_DIM), jnp.float32)
                buf = jnp.zeros((bsz, B_CONV - 1, 3 * B_DIM), h.dtype)
            out, (k_rows, v_rows, s_new, buf_new) = ab_mixer(
                h, pos, kv_past, s0, buf, ab_w_in[j], ab_w_out[j], b_conv_w[j], b_a_log[j], b_dt_bias[j], b_norm[j])
            a_k.append(k_rows)
            a_v.append(v_rows)
            b_s.append(s_new)
            b_conv.append(buf_new)
        else:
            s0 = past_c_s[j] if sample else jnp.zeros((bsz, C_HEADS, C_EXPAND, C_HEAD_DIM), jnp.float32)
            out, s_new = hgrn2_mixer(h, s0, c_w_in[j], c_w_out[j], lb_all[j], c_norm[j])
            c_s.append(s_new)
        x = x + mods[5][:, None, :] * rms_norm(out, norm_post[layer, 1])
        x = half_ffn(x, mods[6], mods[7], mods[8], norm_pre[layer, 2], norm_post[layer, 2],
                     ffn_wi[layer, 1], ffn_wo[layer, 1])
    return x, jnp.stack(a_k), jnp.stack(a_v), jnp.stack(b_s), jnp.stack(b_conv), jnp.stack(c_s)


def setup_inputs(seed: int = 0) -> dict:
    key = jax.random.key(seed)
    ks = jax.random.split(key, 26)
    f32 = jnp.float32

    def normal(k, shape, scale):
        return jax.random.normal(k, shape, f32) * scale

    def gain(k, shape):
        return 1.0 + 0.05 * jax.random.normal(k, shape, f32)

    wb = min(A_WINDOW_MAX, PAST_LEN)
    dt = jnp.exp(jax.random.uniform(ks[19], (N_AB_LAYERS, B_HEADS), f32, math.log(1e-3), math.log(1e-1)))
    return {
        "x_prompt": normal(ks[0], (BATCH, SEQ, D_MODEL), 1.0),
        "x_sample": normal(ks[1], (DEC_BATCH, DEC_SEQ, D_MODEL), 1.0),
        "cache_a_k": normal(ks[2], (N_AB_LAYERS, DEC_BATCH, wb, A_HEADS, A_HEAD_DIM), 1.0),
        "cache_a_v": normal(ks[3], (N_AB_LAYERS, DEC_BATCH, wb, A_HEADS, A_HEAD_DIM), 1.0),
        "state_b_s": normal(ks[4], (N_AB_LAYERS, DEC_BATCH, B_HEADS, B_HEAD_DIM, B_HEAD_DIM), B_HEAD_DIM ** -0.5),
        "state_b_conv": normal(ks[5], (N_AB_LAYERS, DEC_BATCH, B_CONV - 1, 3 * B_DIM), 1.0),
        "state_c_s": normal(ks[6], (N_C_LAYERS, DEC_BATCH, C_HEADS, C_EXPAND, C_HEAD_DIM), 1.0),
        "c_prompt": normal(ks[7], (BATCH, D_MODEL), 1.0),
        "c_sample": normal(ks[8], (DEC_BATCH, D_MODEL), 1.0),
        "ada_w": normal(ks[9], (DEPTH, D_MODEL, N_MOD * D_MODEL), 0.5 * D_MODEL ** -0.5),
        "ada_b": normal(ks[10], (DEPTH, N_MOD * D_MODEL), 0.02),
        "norm_pre": gain(ks[11], (DEPTH, 3, D_MODEL)),
        "norm_post": gain(ks[12], (DEPTH, 3, D_MODEL)),
        "ffn_wi": normal(ks[13], (DEPTH, 2, D_MODEL, 2 * D_FF), D_MODEL ** -0.5),
        "ffn_wo": normal(ks[14], (DEPTH, 2, D_FF, D_MODEL), D_FF ** -0.5),
        "ab_w_in": normal(ks[15], (N_AB_LAYERS, D_MODEL, AB_PROJ), D_MODEL ** -0.5),
        "ab_w_out": normal(ks[16], (N_AB_LAYERS, A_DIM + B_DIM, D_MODEL), (A_DIM + B_DIM) ** -0.5),
        "b_conv_w": normal(ks[17], (N_AB_LAYERS, B_CONV, 3 * B_DIM), B_CONV ** -0.5),
        "b_a_log": jnp.log(jax.random.uniform(ks[18], (N_AB_LAYERS, B_HEADS), f32, 1.0, 16.0)),
        "b_dt_bias": dt + jnp.log(-jnp.expm1(-dt)),
        "b_norm": gain(ks[20], (N_AB_LAYERS, B_HEAD_DIM)),
        "c_w_in": normal(ks[21], (N_C_LAYERS, D_MODEL, C_PROJ), D_MODEL ** -0.5),
        "c_w_out": normal(ks[22], (N_C_LAYERS, C_VAL_DIM, D_MODEL), C_VAL_DIM ** -0.5),
        "c_lower_bounds": normal(ks[23], (N_C_LAYERS, C_KEY_DIM), 1.0),
        "c_norm": gain(ks[24], (N_C_LAYERS, C_HEAD_DIM)),
    }


def reference(x_prompt, x_sample, cache_a_k, cache_a_v, state_b_s, state_b_conv, state_c_s,
              c_prompt, c_sample, ada_w, ada_b, norm_pre, norm_post, ffn_wi, ffn_wo,
              ab_w_in, ab_w_out, b_conv_w, b_a_log, b_dt_bias, b_norm,
              c_w_in, c_w_out, c_lower_bounds, c_norm):
    pos_p = jnp.arange(x_prompt.shape[1], dtype=jnp.int32)
    pos_s = PAST_LEN + jnp.arange(x_sample.shape[1], dtype=jnp.int32)
    y_p, ak_p, av_p, bs_p, bc_p, cs_p = trunk(
        x_prompt, c_prompt, pos_p, None, None, None, None, None,
        ada_w, ada_b, norm_pre, norm_post, ffn_wi, ffn_wo,
        ab_w_in, ab_w_out, b_conv_w, b_a_log, b_dt_bias, b_norm,
        c_w_in, c_w_out, c_lower_bounds, c_norm)
    y_s, ak_s, av_s, bs_s, bc_s, cs_s = trunk(
        x_sample, c_sample, pos_s, cache_a_k, cache_a_v, state_b_s, state_b_conv, state_c_s,
        ada_w, ada_b, norm_pre, norm_post, ffn_wi, ffn_wo,
        ab_w_in, ab_w_out, b_conv_w, b_a_log, b_dt_bias, b_norm,
        c_w_in, c_w_out, c_lower_bounds, c_norm)
    return (y_p, y_s, ak_p, av_p, bs_p, bc_p, cs_p, ak_s, av_s, bs_s, bc_s, cs_s)
```

```python
import functools
import math

import jax
import jax.numpy as jnp
from jax import lax
from jax.experimental import pallas as pl
from jax.experimental.pallas import tpu as pltpu

F32 = jnp.float32
BF16 = jnp.bfloat16
HIGHEST = lax.Precision.HIGHEST

LANES = 128
SUBLANES = 8
VMEM_LIMIT_BYTES = 56 * 1024 * 1024

NORM_EPS = 1e-6
FFN_RESIDUAL = 0.5
N_MOD = 9
ROPE_THETA = 10000.0
A_HEADS = 8
A_GROUPS = ((128, 1), (512, 4), (2048, 16))
A_BLOCK = 128
B_HEADS = 8
B_CONV = 4
B_CHUNK = 64
C_HEADS = 16
HEAD_DIM = 128
SAMPLE_T_PAD = 64
C_CHUNK = 64


def _cparams(*sem):
    return pltpu.CompilerParams(dimension_semantics=sem, vmem_limit_bytes=VMEM_LIMIT_BYTES)


def _sigmoid(x):
    return 1.0 / (1.0 + jnp.exp(-x))


def _silu(x):
    return x * _sigmoid(x)


def _softplus(x):
    return jnp.maximum(x, 0.0) + jnp.log1p(jnp.exp(-jnp.abs(x)))


def _rms(x, g):
    return x * lax.rsqrt(jnp.mean(x * x, axis=-1, keepdims=True) + NORM_EPS) * g


def _dot(a, b):
    return jnp.dot(a.astype(BF16), b.astype(BF16), preferred_element_type=F32)


def _dot_nt(a, b):
    return lax.dot_general(a.astype(BF16), b.astype(BF16), (((1,), (1,)), ((), ())),
                           preferred_element_type=F32)


def _dot_tn(a, b):
    return lax.dot_general(a.astype(BF16), b.astype(BF16), (((0,), (0,)), ((), ())),
                           preferred_element_type=F32)


def _dot_hi(a, b):
    return jnp.dot(a, b, preferred_element_type=F32, precision=HIGHEST)


def _iota(shape, dim):
    return lax.broadcasted_iota(jnp.int32, shape, dim)


def _ada_kernel(c_ref, w_ref, b_ref, o_ref):
    a = _silu(c_ref[...])
    o_ref[...] = _dot(a, w_ref[...]) + b_ref[...]


def _ada_mods(c_all, ada_w, ada_b):
    depth, d, n = ada_w.shape
    rows = c_all.shape[0]
    tn = 1024
    return pl.pallas_call(
        _ada_kernel,
        grid=(depth, n // tn),
        in_specs=[pl.BlockSpec((rows, d), lambda l, j: (0, 0)),
                  pl.BlockSpec((None, d, tn), lambda l, j: (l, 0, j)),
                  pl.BlockSpec((None, 1, tn), lambda l, j: (l, 0, j))],
        out_specs=pl.BlockSpec((None, rows, tn), lambda l, j: (l, 0, j)),
        out_shape=jax.ShapeDtypeStruct((depth, rows, n), F32),
        compiler_params=_cparams("parallel", "parallel"),
        name="ada_mods",
    )(c_all, ada_w, ada_b.reshape(depth, 1, n))


def _norm_kernel(*refs, has_post, has_pre, coef):
    it = iter(refs)
    x_ref = next(it)
    if has_post:
        y_ref, gpost_ref, gate_ref = next(it), next(it), next(it)
    if has_pre:
        gpre_ref, shift_ref, scale_ref = next(it), next(it), next(it)
    x = x_ref[...]
    if has_post:
        xo_ref = next(it)
        x = x + (coef * gate_ref[...]) * _rms(y_ref[...], gpost_ref[...])
        xo_ref[...] = x
    if has_pre:
        h_ref = next(it)
        h = _rms(x, gpre_ref[...]) * (1.0 + scale_ref[...]) + shift_ref[...]
        h_ref[...] = h.astype(BF16)


def _norm_call(x, mods5, boff, post=None, pre=None):
    bsz, t, d = x.shape
    tt = min(t, 256)
    row = pl.BlockSpec((None, tt, d), lambda b, i: (b, i, 0))
    vec = pl.BlockSpec((1, d), lambda b, i: (0, 0))

    def mod_spec(layer, k):
        return pl.BlockSpec((None, None, None, 1, d), lambda b, i: (layer, boff + b, k, 0, 0))

    args, in_specs, out_shape, out_specs = [x], [row], [], []
    coef = 1.0
    if post is not None:
        y, g_post, layer, gate_idx, coef = post
        args += [y, g_post, mods5]
        in_specs += [row, vec, mod_spec(layer, gate_idx)]
        out_shape.append(jax.ShapeDtypeStruct(x.shape, F32))
        out_specs.append(row)
    if pre is not None:
        g_pre, layer, shift_idx, scale_idx = pre
        args += [g_pre, mods5, mods5]
        in_specs += [vec, mod_spec(layer, shift_idx), mod_spec(layer, scale_idx)]
        out_shape.append(jax.ShapeDtypeStruct(x.shape, BF16))
        out_specs.append(row)
    outs = pl.pallas_call(
        functools.partial(_norm_kernel, has_post=post is not None, has_pre=pre is not None, coef=coef),
        grid=(bsz, t // tt),
        in_specs=in_specs, out_specs=out_specs, out_shape=out_shape,
        compiler_params=_cparams("parallel", "parallel"),
        name="sandwich_norm",
    )(*args)
    return outs


def _mm_kernel(*refs, k_sizes):
    n_x = len(k_sizes)
    x_refs, w_ref, o_ref, wbf_ref = refs[:n_x], refs[n_x], refs[n_x + 1], refs[n_x + 2]

    @pl.when(pl.program_id(1) == 0)
    def _():
        wbf_ref[...] = w_ref[...].astype(BF16)

    acc, off = None, 0
    for x_ref, ks in zip(x_refs, k_sizes):
        part = jnp.dot(x_ref[...].astype(BF16), wbf_ref[off:off + ks, :], preferred_element_type=F32)
        acc = part if acc is None else acc + part
        off += ks
    o_ref[...] = acc.astype(o_ref.dtype)


def _mm(xs, w2d, *, k_block, n0, n, tn, tm, out_dtype=F32):
    m = xs[0].shape[0]
    k_sizes = tuple(x.shape[1] for x in xs)
    k = sum(k_sizes)
    tm = min(tm, m)
    assert m % tm == 0 and n % tn == 0 and n0 % tn == 0 and w2d.shape[0] % k == 0
    nb0 = n0 // tn
    in_specs = [pl.BlockSpec((tm, ks), lambda j, i: (i, 0)) for ks in k_sizes]
    in_specs.append(pl.BlockSpec((k, tn), lambda j, i: (k_block, nb0 + j)))
    return pl.pallas_call(
        functools.partial(_mm_kernel, k_sizes=k_sizes),
        grid=(n // tn, m // tm),
        in_specs=in_specs,
        out_specs=pl.BlockSpec((tm, tn), lambda j, i: (i, j)),
        out_shape=jax.ShapeDtypeStruct((m, n), out_dtype),
        scratch_shapes=[pltpu.VMEM((k, tn), BF16)],
        compiler_params=_cparams("parallel", "arbitrary"),
        name="matmul",
    )(*xs, w2d)


_SWIGLU_GROUP = 4


def _swiglu_kernel(*refs):
    ng = _SWIGLU_GROUP
    x_ref, w_refs, o_ref, wbf_ref = refs[0], refs[1:1 + 2 * ng], refs[1 + 2 * ng], refs[2 + 2 * ng]

    @pl.when(pl.program_id(1) == 0)
    def _():
        for j in range(2 * ng):
            wbf_ref[:, j * LANES:(j + 1) * LANES] = w_refs[j][...].astype(BF16)

    r = jnp.dot(x_ref[...], wbf_ref[...], preferred_element_type=F32)
    u, v = r[:, :ng * LANES], r[:, ng * LANES:]
    o_ref[...] = (_silu(v) * u).astype(o_ref.dtype)


def _swiglu_in(h, wi2d, k_block, tm):
    m, k = h.shape
    d_ff = wi2d.shape[1] // 2
    assert d_ff % LANES == 0
    nblk = d_ff // LANES
    ng = _SWIGLU_GROUP
    tn = ng * LANES
    tm = min(tm, m)
    last = nblk - 1
    w_specs = [pl.BlockSpec((k, LANES), lambda j, i, g=g, base=base: (k_block, base + jnp.minimum(ng * j + g, last)))
               for base in (0, nblk) for g in range(ng)]
    return pl.pallas_call(
        _swiglu_kernel,
        grid=(pl.cdiv(nblk, ng), m // tm),
        in_specs=[pl.BlockSpec((tm, k), lambda j, i: (i, 0))] + w_specs,
        out_specs=pl.BlockSpec((tm, tn), lambda j, i: (i, j)),
        out_shape=jax.ShapeDtypeStruct((m, d_ff), BF16),
        scratch_shapes=[pltpu.VMEM((k, 2 * tn), BF16)],
        compiler_params=_cparams("parallel", "arbitrary"),
        name="swiglu_in",
    )(h, *([wi2d] * (2 * ng)))


def _rope_kernel(x_ref, cos_ref, sin_ref, q_ref, k_ref):
    cos, sin = cos_ref[...], sin_ref[...]
    for h in range(2 * A_HEADS):
        xh = x_ref[:, h * HEAD_DIM:(h + 1) * HEAD_DIM]
        r = xh * cos + pltpu.roll(xh, HEAD_DIM // 2, 1) * sin
        dst = q_ref if h < A_HEADS else k_ref
        hh = h % A_HEADS
        dst[:, hh * HEAD_DIM:(hh + 1) * HEAD_DIM] = r


def _rope(proj, cos_t, sin_t):
    bsz, t, _ = proj.shape
    a_dim = A_HEADS * HEAD_DIM
    tt = min(t, 256)
    out = jax.ShapeDtypeStruct((bsz, t, a_dim), F32)
    o_spec = pl.BlockSpec((None, tt, a_dim), lambda b, i: (b, i, 0))
    return pl.pallas_call(
        _rope_kernel,
        grid=(bsz, t // tt),
        in_specs=[pl.BlockSpec((None, tt, 2 * a_dim), lambda b, i: (b, i, 0)),
                  pl.BlockSpec((tt, HEAD_DIM), lambda b, i: (i, 0)),
                  pl.BlockSpec((tt, HEAD_DIM), lambda b, i: (i, 0))],
        out_specs=[o_spec, o_spec], out_shape=[out, out],
        compiler_params=_cparams("parallel", "parallel"),
        name="rope",
    )(proj, cos_t, sin_t)


def _rope_tables(pos):
    half = HEAD_DIM // 2
    inv_freq = jnp.power(ROPE_THETA, -jnp.arange(half, dtype=F32) / half)
    ang = pos.astype(F32)[:, None] * inv_freq[None, :]
    cos, sin = jnp.cos(ang), jnp.sin(ang)
    return jnp.concatenate([cos, cos], axis=1), jnp.concatenate([-sin, sin], axis=1)


def _band_attn_kernel(q_ref, kp_ref, kc_ref, vp_ref, vc_ref, o_ref, lse_ref, *, span):
    n = pl.program_id(2)
    blk = A_BLOCK
    qi = _iota((blk, 2 * blk), 0)
    ki = _iota((blk, 2 * blk), 1)
    rel = qi + blk - ki
    valid = (rel >= 0) & (rel <= span) & ((ki >= blk) | (n > 0))
    scale = HEAD_DIM ** -0.5
    for h in range(A_HEADS):
        sl = slice(h * HEAD_DIM, (h + 1) * HEAD_DIM)
        kcat = jnp.concatenate([kp_ref[:, sl].astype(BF16), kc_ref[:, sl].astype(BF16)], axis=0)
        vcat = jnp.concatenate([vp_ref[:, sl].astype(BF16), vc_ref[:, sl].astype(BF16)], axis=0)
        s = _dot_nt(q_ref[:, sl], kcat) * scale
        s = jnp.where(valid, s, -jnp.inf)
        m = jnp.max(s, axis=-1, keepdims=True)
        p = jnp.exp(s - m)
        den = jnp.sum(p, axis=-1, keepdims=True)
        o_ref[:, sl] = _dot(p, vcat) / den
        lse_ref[:, sl] = jnp.broadcast_to(m + jnp.log(den), (blk, HEAD_DIM))


def _band_attention(q_r, k_r, proj, dil, span):
    bsz, t, a_dim = q_r.shape
    p_cols = proj.shape[2]
    length = t // dil
    assert t % dil == 0 and length % A_BLOCK == 0 and span <= A_BLOCK and p_cols % a_dim == 0
    nb = length // A_BLOCK
    v_blk = p_cols // a_dim
    q_v = q_r.reshape(bsz, length, dil * a_dim)
    k_v = k_r.reshape(bsz, length, dil * a_dim)
    p_v = proj.reshape(bsz, length, dil * p_cols)
    blk = (None, A_BLOCK, a_dim)
    cur = lambda b, r, n: (b, n, r)
    prev = lambda b, r, n: (b, jnp.maximum(n - 1, 0), r)
    v_cur = lambda b, r, n: (b, n, r * v_blk + 2)
    v_prev = lambda b, r, n: (b, jnp.maximum(n - 1, 0), r * v_blk + 2)
    out = jax.ShapeDtypeStruct((bsz, length, dil * a_dim), F32)
    o, lse = pl.pallas_call(
        functools.partial(_band_attn_kernel, span=span),
        grid=(bsz, dil, nb),
        in_specs=[pl.BlockSpec(blk, cur), pl.BlockSpec(blk, prev), pl.BlockSpec(blk, cur),
                  pl.BlockSpec(blk, v_prev), pl.BlockSpec(blk, v_cur)],
        out_specs=[pl.BlockSpec(blk, cur), pl.BlockSpec(blk, cur)],
        out_shape=[out, out],
        compiler_params=_cparams("parallel", "parallel", "parallel"),
        name="band_attention",
    )(q_v, k_v, k_v, p_v, p_v)
    return o.reshape(bsz, t, a_dim), lse.reshape(bsz, t, a_dim)


def _combine_kernel(o1, o2, o3, l1, l2, l3, out_ref):
    a, b, c = l1[...], l2[...], l3[...]
    m = jnp.maximum(jnp.maximum(a, b), c)
    ea, eb, ec = jnp.exp(a - m), jnp.exp(b - m), jnp.exp(c - m)
    num = ea * o1[...] + eb * o2[...] + ec * o3[...]
    out_ref[...] = (num / (ea + eb + ec)).astype(out_ref.dtype)


def _combine_groups(outs, lses):
    bsz, t, a_dim = outs[0].shape
    tt = min(t, 512)
    spec = pl.BlockSpec((None, tt, a_dim), lambda b, i: (b, i, 0))
    return pl.pallas_call(
        _combine_kernel,
        grid=(bsz, t // tt),
        in_specs=[spec] * 6, out_specs=spec,
        out_shape=jax.ShapeDtypeStruct((bsz, t, a_dim), BF16),
        compiler_params=_cparams("parallel", "parallel"),
        name="combine_groups",
    )(*outs, *lses)


def _group_count(d):
    cnt = jnp.zeros(d.shape, F32)
    for window, dil in A_GROUPS:
        hit = (d >= 0) & (d % dil == 0) & (d <= window)
        cnt = cnt + jnp.where(hit, 1.0, 0.0)
    return cnt


def _cache_attn_kernel(q_ref, kn_ref, vn_ref, kc_ref, vc_ref, o_ref, *, n_buf):
    tq = q_ref.shape[0]
    scale = HEAD_DIM ** -0.5
    q = q_ref[...]
    d_c = n_buf + _iota((tq, n_buf), 0) - _iota((tq, n_buf), 1)
    d_n = _iota((tq, tq), 0) - _iota((tq, tq), 1)
    cnt_c, cnt_n = _group_count(d_c), _group_count(d_n)
    s_c = jnp.where(cnt_c > 0, _dot_nt(q, kc_ref[...]) * scale, -jnp.inf)
    s_n = jnp.where(cnt_n > 0, _dot_nt(q, kn_ref[...]) * scale, -jnp.inf)
    m = jnp.maximum(jnp.max(s_c, axis=-1, keepdims=True), jnp.max(s_n, axis=-1, keepdims=True))
    p_c = cnt_c * jnp.exp(s_c - m)
    p_n = cnt_n * jnp.exp(s_n - m)
    den = jnp.sum(p_c, axis=-1, keepdims=True) + jnp.sum(p_n, axis=-1, keepdims=True)
    o_ref[...] = ((_dot(p_c, vc_ref[...]) + _dot(p_n, vn_ref[...])) / den).astype(o_ref.dtype)


def _cache_attention(q_r, k_r, proj, cache_k, cache_v):
    bsz, tp, a_dim = q_r.shape
    n_buf = cache_k.shape[1]
    ck = cache_k.reshape(bsz, n_buf, a_dim)
    cv = cache_v.reshape(bsz, n_buf, a_dim)
    new = pl.BlockSpec((None, tp, HEAD_DIM), lambda b, h: (b, 0, h))
    v_new = pl.BlockSpec((None, tp, HEAD_DIM), lambda b, h: (b, 0, 2 * A_HEADS + h))
    cache = pl.BlockSpec((None, n_buf, HEAD_DIM), lambda b, h: (b, 0, h))
    return pl.pallas_call(
        functools.partial(_cache_attn_kernel, n_buf=n_buf),
        grid=(bsz, A_HEADS),
        in_specs=[new, new, v_new, cache, cache],
        out_specs=new,
        out_shape=jax.ShapeDtypeStruct((bsz, tp, a_dim), BF16),
        compiler_params=_cparams("parallel", "parallel"),
        name="cache_attention",
    )(q_r, k_r, proj, ck, cv)


def _gdn_prep_kernel(raw_ref, prev_ref, buf_ref, cw_ref, gb_ref, gbt_ref, alr_ref, dtr_ref, alc_ref, dtc_ref,
                     u_ref, w_ref, qg_ref, kd_ref, aqk_ref, eg_ref, halo_ref, *, t_valid):
    c = pl.program_id(1)
    ch = B_CHUNK
    hd = HEAD_DIM
    b_dim = B_HEADS * hd

    halo_ref[0:SUBLANES, :] = jnp.where(c == 0, buf_ref[...], prev_ref[...])
    halo_ref[SUBLANES:SUBLANES + ch, :] = raw_ref[...]
    y = raw_ref[...] * cw_ref[B_CONV - 1:B_CONV, :]
    for j in range(B_CONV - 1):
        lag = B_CONV - 1 - j
        y = y + halo_ref[SUBLANES - lag:SUBLANES - lag + ch, :] * cw_ref[j:j + 1, :]
    act = _silu(y)

    row_ok = (c * ch + _iota((ch, 1), 0)) < t_valid
    col_ok = (c * ch + _iota((1, ch), 1)) < t_valid

    gb = gb_ref[...]
    g_col = jnp.where(row_ok, -jnp.exp(alr_ref[...]) * _softplus(gb + dtr_ref[...]), 0.0)
    beta_col = jnp.where(row_ok, _sigmoid(gb), 0.0)
    gbt = gbt_ref[...]
    g_row = jnp.where(col_ok, -jnp.exp(alc_ref[:, :ch]) * _softplus(gbt + dtc_ref[:, :ch]), 0.0)

    ri = _iota((ch, ch), 0)
    ci = _iota((ch, ch), 1)
    tri = ri >= ci
    strict = ri > ci
    lower_ones = jnp.where(tri, 1.0, 0.0)
    upper_ones = jnp.where(ri <= ci, 1.0, 0.0)
    eye = jnp.where(ri == ci, 1.0, 0.0)
    cum_col = _dot_hi(lower_ones, g_col)
    cum_row = _dot_hi(g_row, upper_ones)

    for h in range(B_HEADS):
        sl = slice(h * hd, (h + 1) * hd)
        q = act[:, sl]
        k = act[:, b_dim + h * hd:b_dim + (h + 1) * hd]
        v = act[:, 2 * b_dim + h * hd:2 * b_dim + (h + 1) * hd]
        q = q * lax.rsqrt(jnp.sum(q * q, axis=-1, keepdims=True) + 1e-6) * (hd ** -0.5)
        k = k * lax.rsqrt(jnp.sum(k * k, axis=-1, keepdims=True) + 1e-6)
        q = jnp.where(row_ok, q, 0.0)
        k = jnp.where(row_ok, k, 0.0)
        v = jnp.where(row_ok, v, 0.0)
        cc = cum_col[:, h:h + 1]
        cr = cum_row[h:h + 1, :]
        beta = beta_col[:, B_HEADS + h:B_HEADS + h + 1]
        decay = jnp.where(tri, jnp.exp(jnp.where(tri, cc - cr, 0.0)), 0.0)
        kk = _dot_nt(k, k)
        low = jnp.where(strict, beta * kk * decay, 0.0)
        inv = eye - low
        pw = _dot_hi(low, low)
        size = 2
        while size < ch:
            inv = inv + _dot_hi(inv, pw)
            size *= 2
            if size < ch:
                pw = _dot_hi(pw, pw)
        u_ref[:, sl] = _dot_hi(inv, v * beta)
        w_ref[:, sl] = _dot_hi(inv, k * (beta * jnp.exp(cc)))
        c_last = cc[ch - 1:ch, :]
        qg_ref[:, sl] = q * jnp.exp(cc)
        kd_ref[:, sl] = k * jnp.exp(c_last - cc)
        aqk = _dot_nt(q, k) * decay
        aqk_ref[:, sl] = jnp.concatenate([aqk, jnp.zeros((ch, hd - ch), F32)], axis=1)
        eg_ref[h:h + 1, :] = jnp.broadcast_to(jnp.exp(c_last), (1, hd))


def _gdn_prep(proj, buf8, conv_w, gb_src, gb_blk, gbt, a_log, dt_bias, t_valid):
    bsz, t, _ = proj.shape
    c3 = conv_w.shape[1]
    ch = B_CHUNK
    nc = t // ch
    b_dim = B_HEADS * HEAD_DIM
    pad = jnp.zeros((LANES - B_HEADS,), F32)
    al_row = jnp.concatenate([a_log.astype(F32), pad]).reshape(1, LANES)
    dt_row = jnp.concatenate([dt_bias.astype(F32), pad]).reshape(1, LANES)
    pad_c = jnp.zeros((2 * SUBLANES - B_HEADS,), F32)
    al_col = jnp.broadcast_to(jnp.concatenate([a_log.astype(F32), pad_c])[:, None], (2 * SUBLANES, LANES))
    dt_col = jnp.broadcast_to(jnp.concatenate([dt_bias.astype(F32), pad_c])[:, None], (2 * SUBLANES, LANES))
    full = lambda shape: pl.BlockSpec(shape, lambda b, c: (0,) * len(shape))
    row_out = jax.ShapeDtypeStruct((bsz, t, b_dim), F32)
    row_spec = pl.BlockSpec((None, ch, b_dim), lambda b, c: (b, c, 0))
    return pl.pallas_call(
        functools.partial(_gdn_prep_kernel, t_valid=t_valid),
        grid=(bsz, nc),
        in_specs=[pl.BlockSpec((None, ch, c3), lambda b, c: (b, c, 1)),
                  pl.BlockSpec((None, SUBLANES, c3), lambda b, c: (b, jnp.maximum(c * (ch // SUBLANES) - 1, 0), 1)),
                  pl.BlockSpec((None, SUBLANES, c3), lambda b, c: (b, 0, 0)),
                  full((B_CONV, c3)),
                  pl.BlockSpec((None, ch, LANES), lambda b, c: (b, c, gb_blk)),
                  pl.BlockSpec((None, None, 2 * SUBLANES, ch), lambda b, c: (b, c, 0, 0)),
                  full((1, LANES)), full((1, LANES)),
                  full((2 * SUBLANES, LANES)), full((2 * SUBLANES, LANES))],
        out_specs=[row_spec] * 5 + [pl.BlockSpec((None, None, B_HEADS, HEAD_DIM), lambda b, c: (b, c, 0, 0))],
        out_shape=[row_out] * 5 + [jax.ShapeDtypeStruct((bsz, nc, B_HEADS, HEAD_DIM), F32)],
        scratch_shapes=[pltpu.VMEM((SUBLANES + ch, c3), F32)],
        compiler_params=_cparams("parallel", "parallel"),
        name="gdn_prep",
    )(proj, proj, buf8, conv_w, gb_src, gbt, al_row, dt_row, al_col, dt_col)


def _gdn_scan_kernel(u_ref, w_ref, qg_ref, kd_ref, aqk_ref, eg_ref, s0_ref, z_ref, nw_ref,
                     o_ref, sfin_ref, s_ref):
    c = pl.program_id(1)
    ch = B_CHUNK
    hd = HEAD_DIM

    @pl.when(c == 0)
    def _():
        s_ref[...] = s0_ref[...]

    for h in range(B_HEADS):
        sl = slice(h * hd, (h + 1) * hd)
        s = s_ref[h]
        v_new = u_ref[:, sl] - _dot(w_ref[:, sl], s)
        o = _dot(qg_ref[:, sl], s) + _dot(aqk_ref[:, h * hd:h * hd + ch], v_new)
        s_ref[h] = s * eg_ref[h:h + 1, :] + _dot_tn(kd_ref[:, sl], v_new)
        o_ref[:, sl] = (_rms(o, nw_ref[...]) * _silu(z_ref[:, sl])).astype(o_ref.dtype)

    @pl.when(c == pl.num_programs(1) - 1)
    def _():
        sfin_ref[...] = s_ref[...]


def _gdn_scan(u, w, qg, kd, aqk, eg, s0, z_src, norm_w):
    bsz, t, b_dim = u.shape
    ch = B_CHUNK
    nc = t // ch
    row_spec = pl.BlockSpec((None, ch, b_dim), lambda b, c: (b, c, 0))
    st_spec = pl.BlockSpec((None, B_HEADS, HEAD_DIM, HEAD_DIM), lambda b, c: (b, 0, 0, 0))
    return pl.pallas_call(
        _gdn_scan_kernel,
        grid=(bsz, nc),
        in_specs=[row_spec] * 5 + [pl.BlockSpec((None, None, B_HEADS, HEAD_DIM), lambda b, c: (b, c, 0, 0)),
                                   st_spec, row_spec, pl.BlockSpec((1, HEAD_DIM), lambda b, c: (0, 0))],
        out_specs=[row_spec, st_spec],
        out_shape=[jax.ShapeDtypeStruct((bsz, t, b_dim), BF16),
                   jax.ShapeDtypeStruct((bsz, B_HEADS, HEAD_DIM, HEAD_DIM), F32)],
        scratch_shapes=[pltpu.VMEM((B_HEADS, HEAD_DIM, HEAD_DIM), F32)],
        compiler_params=_cparams("parallel", "arbitrary"),
        name="gdn_scan",
    )(u, w, qg, kd, aqk, eg, s0, z_src, norm_w)


def _hgrn_kernel(q_ref, f_ref, i_ref, g_ref, llb_ref, l1m_ref, oml_ref, s0_ref, nw_ref,
                 o_ref, sfin_ref, st_ref, *, t_valid):
    c = pl.program_id(1)
    ch = C_CHUNK
    hd = HEAD_DIM
    width = C_HEADS * hd

    @pl.when(c == 0)
    def _():
        for h in range(C_HEADS):
            st_ref[h] = s0_ref[h].T

    row_idx = _iota((ch, 1), 0)
    row_ok = (c * ch + row_idx) < t_valid
    f = f_ref[...]
    log_sig = jnp.minimum(f, 0.0) - jnp.log1p(jnp.exp(-jnp.abs(f)))
    a = llb_ref[...]
    b = l1m_ref[...] + log_sig
    log_f = jnp.maximum(a, b) + jnp.log1p(jnp.exp(-jnp.abs(a - b)))
    log_f = jnp.where(row_ok, log_f, 0.0)
    k = jnp.where(row_ok, oml_ref[...] * _sigmoid(-f), 0.0)
    q = jnp.where(row_ok, q_ref[...] * (hd ** -0.5), 0.0)
    v = jnp.where(row_ok, i_ref[...], 0.0)

    ri = _iota((ch, ch), 0)
    ci = _iota((ch, ch), 1)
    cum = _dot_hi(jnp.where(ri >= ci, 1.0, 0.0), log_f)
    c_last = cum[ch - 1:ch, :]
    qg = q * jnp.exp(cum)
    kd = k * jnp.exp(c_last - cum)
    e_last = jnp.exp(c_last)

    levels = []
    half = SUBLANES
    while half < ch:
        base = (ri // (2 * half)) * (2 * half)
        ref_row = base + half - 1
        later = (ri - base) >= half
        lo_col = jnp.where(later, ref_row, ri)
        hi_col = jnp.where(later, ri, ref_row)
        sel = jnp.where(ci > lo_col, jnp.where(ci <= hi_col, 1.0, 0.0), 0.0)
        e = jnp.exp(_dot_hi(sel, log_f))
        later_row = (row_idx % (2 * half)) >= half
        qt = jnp.where(later_row, q * e, 0.0)
        kt = jnp.where(later_row, 0.0, k * e)
        cbase = (ci // (2 * half)) * (2 * half)
        mask = later & (cbase == base) & ((ci - cbase) < half)
        levels.append((qt, kt, mask))
        half *= 2

    sub_row = row_idx % SUBLANES
    lag_terms, lag_vals = [], []
    for lag in range(SUBLANES):
        ok = sub_row >= lag
        if lag == 0:
            k_l, c_l, v_l = k, cum, v
        else:
            k_l = pltpu.roll(k, lag, 0)
            c_l = pltpu.roll(cum, lag, 0)
            v_l = pltpu.roll(v, lag, 0)
        term = jnp.where(ok, q * k_l * jnp.exp(jnp.where(ok, cum - c_l, 0.0)), 0.0)
        lag_terms.append(term)
        lag_vals.append(v_l)
    ones = jnp.ones((hd, hd), BF16)

    for h in range(C_HEADS):
        sl = slice(h * hd, (h + 1) * hd)
        st = st_ref[h]
        att = jnp.zeros((ch, ch), F32)
        for qt, kt, mask in levels:
            att = att + jnp.where(mask, _dot_nt(qt[:, sl], kt[:, sl]), 0.0)
        o = _dot_nt(qg[:, sl], st) + _dot(att, v[:, sl])
        stack = jnp.concatenate([t[:, sl] for t in lag_terms], axis=0)
        hi = stack.astype(BF16)
        lo = (stack - hi.astype(F32)).astype(BF16)
        sums = (jnp.dot(hi, ones, preferred_element_type=F32)
                + jnp.dot(lo, ones, preferred_element_type=F32))
        for lag in range(SUBLANES):
            o = o + sums[lag * ch:(lag + 1) * ch, :] * lag_vals[lag][:, sl]
        st_ref[h] = st * e_last[:, sl] + _dot_tn(v[:, sl], kd[:, sl])
        o_ref[:, sl] = (_rms(o, nw_ref[...]) * _silu(g_ref[:, sl])).astype(o_ref.dtype)

    @pl.when(c == pl.num_programs(1) - 1)
    def _():
        for h in range(C_HEADS):
            sfin_ref[h] = st_ref[h].T


def _hgrn(proj, lb, s0, norm_w, t_valid):
    bsz, t, four_w = proj.shape
    width = four_w // 4
    ch = C_CHUNK
    nc = t // ch
    lb = lb.astype(F32).reshape(1, width)
    col = lambda j: pl.BlockSpec((None, ch, width), lambda b, c: (b, c, j))
    vec = pl.BlockSpec((1, width), lambda b, c: (0, 0))
    st_spec = pl.BlockSpec((None, C_HEADS, HEAD_DIM, HEAD_DIM), lambda b, c: (b, 0, 0, 0))
    return pl.pallas_call(
        functools.partial(_hgrn_kernel, t_valid=t_valid),
        grid=(bsz, nc),
        in_specs=[col(0), col(1), col(2), col(3), vec, vec, vec, st_spec,
                  pl.BlockSpec((1, HEAD_DIM), lambda b, c: (0, 0))],
        out_specs=[pl.BlockSpec((None, ch, width), lambda b, c: (b, c, 0)), st_spec],
        out_shape=[jax.ShapeDtypeStruct((bsz, t, width), BF16),
                   jax.ShapeDtypeStruct((bsz, C_HEADS, HEAD_DIM, HEAD_DIM), F32)],
        scratch_shapes=[pltpu.VMEM((C_HEADS, HEAD_DIM, HEAD_DIM), F32)],
        compiler_params=_cparams("parallel", "arbitrary"),
        name="hgrn2",
    )(proj, proj, proj, proj, jnp.log(lb), jnp.log1p(-lb), 1.0 - lb, s0, norm_w)


def _ab_mixer(h2d, bsz, t, t_valid, rope_tabs, kv_past, s0, conv_buf, w_in2d, j, w_zgb, w_out2d,
              conv_w, a_log, dt_bias, norm_w, tm):
    a_dim = A_HEADS * HEAD_DIM
    b_dim = B_HEADS * HEAD_DIM
    d = h2d.shape[1]
    main_cols = 3 * a_dim + 3 * b_dim
    proj = _mm([h2d], w_in2d, k_block=j, n0=0, n=main_cols, tn=1024, tm=tm).reshape(bsz, t, main_cols)
    zgb = _mm([h2d], w_zgb, k_block=0, n0=0, n=w_zgb.shape[1], tn=w_zgb.shape[1], tm=tm)
    zgb = zgb.reshape(bsz, t, w_zgb.shape[1])

    q_r, k_r = _rope(proj, *rope_tabs)
    if kv_past is None:
        outs, lses = [], []
        for window, dil in A_GROUPS:
            o, lse = _band_attention(q_r, k_r, proj, dil, window // dil)
            outs.append(o)
            lses.append(lse)
        o_a = _combine_groups(outs, lses)
        keep = min(A_GROUPS[-1][0], t_valid)
        k_rows = k_r[:, t_valid - keep:t_valid]
        v_rows = proj[:, t_valid - keep:t_valid, 2 * a_dim:3 * a_dim]
    else:
        o_a = _cache_attention(q_r, k_r, proj, kv_past[0], kv_past[1])
        k_rows = k_r[:, :t_valid]
        v_rows = proj[:, :t_valid, 2 * a_dim:3 * a_dim]
    k_rows = k_rows.reshape(bsz, -1, A_HEADS, HEAD_DIM)
    v_rows = v_rows.reshape(bsz, -1, A_HEADS, HEAD_DIM)

    raw = proj[:, :t_valid, 3 * a_dim:]
    buf_new = jnp.concatenate([conv_buf, raw], axis=1)[:, -(B_CONV - 1):]
    buf8 = jnp.pad(conv_buf, ((0, 0), (SUBLANES - (B_CONV - 1), 0), (0, 0)))
    nc = t // B_CHUNK
    gbt = zgb[:, :, b_dim:b_dim + 2 * SUBLANES].reshape(bsz, nc, B_CHUNK, 2 * SUBLANES).swapaxes(2, 3)
    u, w, qg, kd, aqk, eg = _gdn_prep(proj, buf8, conv_w, zgb, b_dim // LANES, gbt, a_log, dt_bias, t_valid)
    o_b, s_new = _gdn_scan(u, w, qg, kd, aqk, eg, s0, zgb, norm_w.reshape(1, HEAD_DIM))

    out = _mm([o_a.reshape(bsz * t, a_dim), o_b.reshape(bsz * t, b_dim)], w_out2d,
              k_block=j, n0=0, n=d, tn=1024, tm=tm)
    return out, (k_rows, v_rows, s_new, buf_new)


def _hgrn_mixer(h2d, bsz, t, t_valid, s0, w_in2d, j, w_out2d, lb, norm_w, tm):
    d = h2d.shape[1]
    n_proj = w_in2d.shape[1]
    proj = _mm([h2d], w_in2d, k_block=j, n0=0, n=n_proj, tn=1024, tm=tm).reshape(bsz, t, n_proj)
    o, s_new = _hgrn(proj, lb, s0, norm_w.reshape(1, HEAD_DIM), t_valid)
    out = _mm([o.reshape(bsz * t, n_proj // 4)], w_out2d, k_block=j, n0=0, n=d, tn=1024, tm=tm)
    return out, s_new


def _trunk(x, t_valid, pos, mods5, boff, past, p):
    bsz, t, d = x.shape
    depth = p["norm_pre"].shape[0]
    tm = 1024
    rope_tabs = _rope_tables(pos)
    a_dim = A_HEADS * HEAD_DIM
    b_dim = B_HEADS * HEAD_DIM
    d_ff = p["ffn_wo"].shape[2]
    wi2d = p["ffn_wi"].reshape(-1, 2 * d_ff)
    wo2d = p["ffn_wo"].reshape(-1, d)
    ab_in2d = p["ab_w_in"].reshape(-1, p["ab_w_in"].shape[2])
    ab_out2d = p["ab_w_out"].reshape(-1, d)
    c_in2d = p["c_w_in"].reshape(-1, p["c_w_in"].shape[2])
    c_out2d = p["c_w_out"].reshape(-1, d)
    norm_pre = p["norm_pre"].reshape(depth, 3, 1, d)
    norm_post = p["norm_post"].reshape(depth, 3, 1, d)
    lb_all = jnp.cumsum(jax.nn.softmax(p["c_lower_bounds"].astype(F32), axis=0), axis=0)
    lb_all = lb_all - lb_all[0:1]

    a_k, a_v, b_s, b_conv, c_s = [], [], [], [], []
    (h,) = _norm_call(x, mods5, boff, pre=(norm_pre[0, 0], 0, 0, 1))
    for layer in range(depth):
        j = layer // 2
        for sub in range(3):
            h2d = h.reshape(bsz * t, d)
            if sub == 1:
                if layer % 2 == 0:
                    main_cols = 3 * a_dim + 3 * b_dim
                    w_full = p["ab_w_in"][j]
                    w_zgb = jnp.concatenate(
                        [w_full[:, main_cols + 2 * B_HEADS:], w_full[:, main_cols:main_cols + 2 * B_HEADS],
                         jnp.zeros((d, LANES - 2 * B_HEADS), F32)], axis=1)
                    if past is None:
                        kv_past = None
                        s0 = jnp.zeros((bsz, B_HEADS, HEAD_DIM, HEAD_DIM), F32)
                        buf = jnp.zeros((bsz, B_CONV - 1, 3 * b_dim), F32)
                    else:
                        kv_past, s0, buf = (past[0][j], past[1][j]), past[2][j], past[3][j]
                    y, (k_rows, v_rows, s_new, buf_new) = _ab_mixer(
                        h2d, bsz, t, t_valid, rope_tabs, kv_past, s0, buf, ab_in2d, j, w_zgb, ab_out2d,
                        p["b_conv_w"][j], p["b_a_log"][j], p["b_dt_bias"][j], p["b_norm"][j], tm)
                    a_k.append(k_rows)
                    a_v.append(v_rows)
                    b_s.append(s_new)
                    b_conv.append(buf_new)
                else:
                    s0 = (jnp.zeros((bsz, C_HEADS, HEAD_DIM, HEAD_DIM), F32) if past is None else past[4][j])
                    y, s_new = _hgrn_mixer(h2d, bsz, t, t_valid, s0, c_in2d, j, c_out2d, lb_all[j],
                                           p["c_norm"][j], tm)
                    c_s.append(s_new)
                coef = 1.0
            else:
                i_ffn = 0 if sub == 0 else 1
                act = _swiglu_in(h2d, wi2d, layer * 2 + i_ffn, tm)
                y = _mm([act], wo2d, k_block=layer * 2 + i_ffn, n0=0, n=d, tn=512, tm=512)
                coef = FFN_RESIDUAL
            post = (y.reshape(bsz, t, d), norm_post[layer, sub], layer, 3 * sub + 2, coef)
            if sub < 2:
                pre = (norm_pre[layer, sub + 1], layer, 3 * (sub + 1), 3 * (sub + 1) + 1)
            elif layer + 1 < depth:
                pre = (norm_pre[layer + 1, 0], layer + 1, 0, 1)
            else:
                pre = None
            outs = _norm_call(x, mods5, boff, post=post, pre=pre)
            x = outs[0]
            h = outs[1] if pre is not None else None
    return x, jnp.stack(a_k), jnp.stack(a_v), jnp.stack(b_s), jnp.stack(b_conv), jnp.stack(c_s)


def kernel(x_prompt, x_sample, cache_a_k, cache_a_v, state_b_s, state_b_conv, state_c_s, c_prompt, c_sample,
           ada_w, ada_b, norm_pre, norm_post, ffn_wi, ffn_wo, ab_w_in, ab_w_out, b_conv_w, b_a_log, b_dt_bias,
           b_norm, c_w_in, c_w_out, c_lower_bounds, c_norm):
    p = dict(norm_pre=norm_pre, norm_post=norm_post, ffn_wi=ffn_wi, ffn_wo=ffn_wo, ab_w_in=ab_w_in,
             ab_w_out=ab_w_out, b_conv_w=b_conv_w, b_a_log=b_a_log, b_dt_bias=b_dt_bias, b_norm=b_norm,
             c_w_in=c_w_in, c_w_out=c_w_out, c_lower_bounds=c_lower_bounds, c_norm=c_norm)
    depth, d = norm_pre.shape[0], x_prompt.shape[2]
    n_p, t_p = x_prompt.shape[0], x_prompt.shape[1]
    n_s, t_s = x_sample.shape[0], x_sample.shape[1]
    past_len = 16384

    rows = -(-(n_p + n_s) // (2 * SUBLANES)) * (2 * SUBLANES)
    c_all = jnp.concatenate([c_prompt, c_sample, jnp.zeros((rows - n_p - n_s, d), F32)], axis=0)
    mods5 = _ada_mods(c_all, ada_w, ada_b).reshape(depth, rows, N_MOD, 1, d)

    pos_p = jnp.arange(t_p, dtype=jnp.int32)
    y_p, ak_p, av_p, bs_p, bc_p, cs_p = _trunk(x_prompt, t_p, pos_p, mods5, 0, None, p)

    x_s = jnp.pad(x_sample, ((0, 0), (0, SAMPLE_T_PAD - t_s), (0, 0)))
    pos_s = past_len + jnp.arange(SAMPLE_T_PAD, dtype=jnp.int32)
    past = (cache_a_k, cache_a_v, state_b_s, state_b_conv, state_c_s)
    y_s, ak_s, av_s, bs_s, bc_s, cs_s = _trunk(x_s, t_s, pos_s, mods5, n_p, past, p)
    return (y_p, y_s[:, :t_s], ak_p, av_p, bs_p, bc_p, cs_p, ak_s, av_s, bs_s, bc_s, cs_s)
```

```python
import functools
import math

import jax
import jax.numpy as jnp
from jax import lax
from jax.experimental import pallas as pl
from jax.experimental.pallas import tpu as pltpu

F32 = jnp.float32
BF16 = jnp.bfloat16
HIGHEST = lax.Precision.HIGHEST

LANES = 128
SUBLANES = 8
VMEM_LIMIT_BYTES = 56 * 1024 * 1024

_LOG2_E = 1.4426950408889634
NORM_EPS = 1e-6
FFN_RESIDUAL = 0.5
N_MOD = 9
ROPE_THETA = 10000.0
A_HEADS = 8
A_GROUPS = ((128, 1), (512, 4), (2048, 16))
A_BLOCK = 128
B_HEADS = 8
B_CONV = 4
B_CHUNK = 64
C_HEADS = 16
HEAD_DIM = 128
SAMPLE_T_PAD = 64
C_CHUNK = 64
_HGRN_HEAD_GROUP = 16


def _cparams(*sem):
    return pltpu.CompilerParams(dimension_semantics=sem, vmem_limit_bytes=VMEM_LIMIT_BYTES)


def _sigmoid(x):
    return 1.0 / (1.0 + jnp.exp(-x))


def _silu(x):
    return x * _sigmoid(x)


def _softplus(x):
    return jnp.maximum(x, 0.0) + jnp.log1p(jnp.exp(-jnp.abs(x)))


def _rms(x, g):
    return x * lax.rsqrt(jnp.mean(x * x, axis=-1, keepdims=True) + NORM_EPS) * g


def _dot(a, b):
    return jnp.dot(a.astype(BF16), b.astype(BF16), preferred_element_type=F32)


def _dot_nt(a, b):
    return lax.dot_general(a.astype(BF16), b.astype(BF16), (((1,), (1,)), ((), ())),
                           preferred_element_type=F32)


def _dot_tn(a, b):
    return lax.dot_general(a.astype(BF16), b.astype(BF16), (((0,), (0,)), ((), ())),
                           preferred_element_type=F32)


def _split2(x):
    hi = x.astype(BF16)
    lo = (x - hi.astype(F32)).astype(BF16)
    return hi, lo


def _dot_x3(a_parts, b_parts):
    (ah, al), (bh, bl) = a_parts, b_parts
    dot = functools.partial(jnp.dot, preferred_element_type=F32)
    return dot(ah, bh) + (dot(ah, bl) + dot(al, bh))


def _dot_ones(ones_bf16, x, ones_on_right):
    hi = x.astype(BF16)
    r1 = x - hi.astype(F32)
    mid = r1.astype(BF16)
    lo = (r1 - mid.astype(F32)).astype(BF16)
    dot = functools.partial(jnp.dot, preferred_element_type=F32)
    if ones_on_right:
        return dot(hi, ones_bf16) + (dot(mid, ones_bf16) + dot(lo, ones_bf16))
    return dot(ones_bf16, hi) + (dot(ones_bf16, mid) + dot(ones_bf16, lo))


def _roll_in_blocks(x, lag):
    rows, cols = x.shape
    x3 = x.reshape(rows // SUBLANES, SUBLANES, cols)
    return pltpu.roll(x3, lag, 1).reshape(rows, cols)


def _iota(shape, dim):
    return lax.broadcasted_iota(jnp.int32, shape, dim)


def _ada_kernel(c_ref, w_ref, b_ref, o_ref):
    a = _silu(c_ref[...])
    o_ref[...] = _dot(a, w_ref[...]) + b_ref[...]


def _ada_mods(c_all, ada_w, ada_b):
    depth, d, n = ada_w.shape
    rows = c_all.shape[0]
    tn = 1024
    return pl.pallas_call(
        _ada_kernel,
        grid=(depth, n // tn),
        in_specs=[pl.BlockSpec((rows, d), lambda l, j: (0, 0)),
                  pl.BlockSpec((None, d, tn), lambda l, j: (l, 0, j)),
                  pl.BlockSpec((None, 1, tn), lambda l, j: (l, 0, j))],
        out_specs=pl.BlockSpec((None, rows, tn), lambda l, j: (l, 0, j)),
        out_shape=jax.ShapeDtypeStruct((depth, rows, n), F32),
        compiler_params=_cparams("parallel", "parallel"),
        name="ada_mods",
    )(c_all, ada_w, ada_b.reshape(depth, 1, n))


def _norm_kernel(*refs, has_post, has_pre, coef):
    it = iter(refs)
    x_ref = next(it)
    if has_post:
        y_ref, gpost_ref, gate_ref = next(it), next(it), next(it)
    if has_pre:
        gpre_ref, shift_ref, scale_ref = next(it), next(it), next(it)
    x = x_ref[...]
    if has_post:
        xo_ref = next(it)
        x = x + (coef * gate_ref[...]) * _rms(y_ref[...], gpost_ref[...])
        xo_ref[...] = x
    if has_pre:
        h_ref = next(it)
        h = _rms(x, gpre_ref[...]) * (1.0 + scale_ref[...]) + shift_ref[...]
        h_ref[...] = h.astype(BF16)


def _norm_call(x, mods5, boff, post=None, pre=None):
    bsz, t, d = x.shape
    tt = min(t, 256)
    row = pl.BlockSpec((None, tt, d), lambda b, i: (b, i, 0))
    vec = pl.BlockSpec((1, d), lambda b, i: (0, 0))

    def mod_spec(layer, k):
        return pl.BlockSpec((None, None, None, 1, d), lambda b, i: (layer, boff + b, k, 0, 0))

    args, in_specs, out_shape, out_specs = [x], [row], [], []
    coef = 1.0
    if post is not None:
        y, g_post, layer, gate_idx, coef = post
        args += [y, g_post, mods5]
        in_specs += [row, vec, mod_spec(layer, gate_idx)]
        out_shape.append(jax.ShapeDtypeStruct(x.shape, F32))
        out_specs.append(row)
    if pre is not None:
        g_pre, layer, shift_idx, scale_idx = pre
        args += [g_pre, mods5, mods5]
        in_specs += [vec, mod_spec(layer, shift_idx), mod_spec(layer, scale_idx)]
        out_shape.append(jax.ShapeDtypeStruct(x.shape, BF16))
        out_specs.append(row)
    outs = pl.pallas_call(
        functools.partial(_norm_kernel, has_post=post is not None, has_pre=pre is not None, coef=coef),
        grid=(bsz, t // tt),
        in_specs=in_specs, out_specs=out_specs, out_shape=out_shape,
        compiler_params=_cparams("parallel", "parallel"),
        name="sandwich_norm",
    )(*args)
    return outs


def _mm_kernel(*refs, k_sizes):
    n_x = len(k_sizes)
    x_refs, w_ref, o_ref, wbf_ref = refs[:n_x], refs[n_x], refs[n_x + 1], refs[n_x + 2]

    @pl.when(pl.program_id(1) == 0)
    def _():
        wbf_ref[...] = w_ref[...].astype(BF16)

    acc, off = None, 0
    for x_ref, ks in zip(x_refs, k_sizes):
        part = jnp.dot(x_ref[...].astype(BF16), wbf_ref[off:off + ks, :], preferred_element_type=F32)
        acc = part if acc is None else acc + part
        off += ks
    o_ref[...] = acc.astype(o_ref.dtype)


def _mm(xs, w2d, *, k_block, n0, n, tn, tm, out_dtype=F32):
    m = xs[0].shape[0]
    k_sizes = tuple(x.shape[1] for x in xs)
    k = sum(k_sizes)
    tm = min(tm, m)
    assert m % tm == 0 and n % tn == 0 and n0 % tn == 0 and w2d.shape[0] % k == 0
    nb0 = n0 // tn
    in_specs = [pl.BlockSpec((tm, ks), lambda j, i: (i, 0)) for ks in k_sizes]
    in_specs.append(pl.BlockSpec((k, tn), lambda j, i: (k_block, nb0 + j)))
    return pl.pallas_call(
        functools.partial(_mm_kernel, k_sizes=k_sizes),
        grid=(n // tn, m // tm),
        in_specs=in_specs,
        out_specs=pl.BlockSpec((tm, tn), lambda j, i: (i, j)),
        out_shape=jax.ShapeDtypeStruct((m, n), out_dtype),
        scratch_shapes=[pltpu.VMEM((k, tn), BF16)],
        compiler_params=_cparams("parallel", "arbitrary"),
        name="matmul",
    )(*xs, w2d)


_SWIGLU_GROUP = 4


def _swiglu_kernel(*refs):
    ng = _SWIGLU_GROUP
    x_ref, w_refs, o_ref, wbf_ref = refs[0], refs[1:1 + 2 * ng], refs[1 + 2 * ng], refs[2 + 2 * ng]

    @pl.when(pl.program_id(1) == 0)
    def _():
        for g in range(ng):
            wbf_ref[:, (2 * g) * LANES:(2 * g + 1) * LANES] = w_refs[g][...].astype(BF16)
            wbf_ref[:, (2 * g + 1) * LANES:(2 * g + 2) * LANES] = w_refs[ng + g][...].astype(BF16)

    x = x_ref[...]
    for g in range(ng):
        r = jnp.dot(x, wbf_ref[:, 2 * g * LANES:(2 * g + 2) * LANES], preferred_element_type=F32)
        u, v = r[:, :LANES], r[:, LANES:]
        o_ref[:, g * LANES:(g + 1) * LANES] = (_silu(v) * u).astype(o_ref.dtype)


def _swiglu_in(h, wi2d, k_block, tm):
    m, k = h.shape
    d_ff = wi2d.shape[1] // 2
    assert d_ff % LANES == 0
    nblk = d_ff // LANES
    ng = _SWIGLU_GROUP
    tn = ng * LANES
    tm = min(tm, m)
    last = nblk - 1
    w_specs = [pl.BlockSpec((k, LANES), lambda j, i, g=g, base=base: (k_block, base + jnp.minimum(ng * j + g, last)))
               for base in (0, nblk) for g in range(ng)]
    return pl.pallas_call(
        _swiglu_kernel,
        grid=(pl.cdiv(nblk, ng), m // tm),
        in_specs=[pl.BlockSpec((tm, k), lambda j, i: (i, 0))] + w_specs,
        out_specs=pl.BlockSpec((tm, tn), lambda j, i: (i, j)),
        out_shape=jax.ShapeDtypeStruct((m, d_ff), BF16),
        scratch_shapes=[pltpu.VMEM((k, 2 * tn), BF16)],
        compiler_params=_cparams("parallel", "arbitrary"),
        name="swiglu_in",
    )(h, *([wi2d] * (2 * ng)))


def _rope_kernel(x_ref, cos_ref, sin_ref, q_ref, k_ref):
    cos, sin = cos_ref[...], sin_ref[...]
    for h in range(2 * A_HEADS):
        xh = x_ref[:, h * HEAD_DIM:(h + 1) * HEAD_DIM]
        r = xh * cos + pltpu.roll(xh, HEAD_DIM // 2, 1) * sin
        dst = q_ref if h < A_HEADS else k_ref
        hh = h % A_HEADS
        dst[:, hh * HEAD_DIM:(hh + 1) * HEAD_DIM] = r


def _rope(proj, cos_t, sin_t):
    bsz, t, _ = proj.shape
    a_dim = A_HEADS * HEAD_DIM
    tt = min(t, 256)
    out = jax.ShapeDtypeStruct((bsz, t, a_dim), F32)
    o_spec = pl.BlockSpec((None, tt, a_dim), lambda b, i: (b, i, 0))
    return pl.pallas_call(
        _rope_kernel,
        grid=(bsz, t // tt),
        in_specs=[pl.BlockSpec((None, tt, 2 * a_dim), lambda b, i: (b, i, 0)),
                  pl.BlockSpec((tt, HEAD_DIM), lambda b, i: (i, 0)),
                  pl.BlockSpec((tt, HEAD_DIM), lambda b, i: (i, 0))],
        out_specs=[o_spec, o_spec], out_shape=[out, out],
        compiler_params=_cparams("parallel", "parallel"),
        name="rope",
    )(proj, cos_t, sin_t)


def _rope_tables(pos):
    half = HEAD_DIM // 2
    inv_freq = jnp.power(ROPE_THETA, -jnp.arange(half, dtype=F32) / half)
    ang = pos.astype(F32)[:, None] * inv_freq[None, :]
    cos, sin = jnp.cos(ang), jnp.sin(ang)
    return jnp.concatenate([cos, cos], axis=1), jnp.concatenate([-sin, sin], axis=1)


_A_TOKEN_BLOCK = A_BLOCK * max(dil for _, dil in A_GROUPS)


def _dilated_attn_kernel(q_ref, kp_ref, kc_ref, vp_ref, vc_ref, o_ref, m_ref, l_ref, acc_ref):
    tb = pl.program_id(2)
    blk = A_BLOCK
    tokens = _A_TOKEN_BLOCK
    qi = _iota((blk, 2 * blk), 0)
    ki = _iota((blk, 2 * blk), 1)
    rel = qi + blk - ki
    has_prev = (ki >= blk) | (tb > 0)
    scale = HEAD_DIM ** -0.5

    for g, (window, dil) in enumerate(A_GROUPS):
        span = window // dil
        band = (rel >= 0) & (rel <= span)
        reach = blk * dil

        def rows(ref, base, r, dil=dil):
            if dil == 1:
                return ref[base:base + blk, :]
            return ref[pl.ds(base + r, blk, stride=dil), :]

        def put(ref, base, r, val, dil=dil):
            if dil == 1:
                ref[base:base + blk, :] = val
            else:
                ref[pl.ds(base + r, blk, stride=dil), :] = val

        for s in range(tokens // reach):
            base = s * reach
            for r in range(dil):
                q = rows(q_ref, base, r)
                if s == 0:
                    kp, vp = rows(kp_ref, tokens - reach, r), rows(vp_ref, tokens - reach, r)
                    valid = band & has_prev
                else:
                    kp, vp = rows(kc_ref, base - reach, r), rows(vc_ref, base - reach, r)
                    valid = band
                kcat = jnp.concatenate([kp.astype(BF16), rows(kc_ref, base, r).astype(BF16)], axis=0)
                vcat = jnp.concatenate([vp.astype(BF16), rows(vc_ref, base, r).astype(BF16)], axis=0)
                sc = jnp.where(valid, _dot_nt(q, kcat) * scale, -jnp.inf)
                m_loc = jnp.max(sc, axis=-1, keepdims=True)
                if g == 0:
                    p = jnp.exp(sc - m_loc)
                    put(m_ref, base, r, jnp.broadcast_to(m_loc, (blk, HEAD_DIM)))
                    put(l_ref, base, r, jnp.broadcast_to(jnp.sum(p, axis=-1, keepdims=True), (blk, HEAD_DIM)))
                    put(acc_ref, base, r, _dot(p, vcat))
                else:
                    m_old = rows(m_ref, base, r)
                    m_new = jnp.maximum(m_old, m_loc)
                    alpha = jnp.exp(m_old - m_new)
                    p = jnp.exp(sc - m_new[:, 0:1])
                    put(m_ref, base, r, m_new)
                    put(l_ref, base, r, alpha * rows(l_ref, base, r) + jnp.sum(p, axis=-1, keepdims=True))
                    put(acc_ref, base, r, alpha * rows(acc_ref, base, r) + _dot(p, vcat))

    o_ref[...] = (acc_ref[...] / l_ref[...]).astype(o_ref.dtype)


def _dilated_attention(q_r, k_r, proj):
    bsz, t, a_dim = q_r.shape
    tokens = _A_TOKEN_BLOCK
    assert t % tokens == 0 and all(window // dil <= A_BLOCK for window, dil in A_GROUPS)
    blk = (None, tokens, HEAD_DIM)
    v_off = 2 * a_dim // HEAD_DIM
    cur = lambda b, h, i: (b, i, h)
    prev = lambda b, h, i: (b, jnp.maximum(i - 1, 0), h)
    v_cur = lambda b, h, i: (b, i, v_off + h)
    v_prev = lambda b, h, i: (b, jnp.maximum(i - 1, 0), v_off + h)
    return pl.pallas_call(
        _dilated_attn_kernel,
        grid=(bsz, A_HEADS, t // tokens),
        in_specs=[pl.BlockSpec(blk, cur), pl.BlockSpec(blk, prev), pl.BlockSpec(blk, cur),
                  pl.BlockSpec(blk, v_prev), pl.BlockSpec(blk, v_cur)],
        out_specs=pl.BlockSpec(blk, cur),
        out_shape=jax.ShapeDtypeStruct((bsz, t, a_dim), BF16),
        scratch_shapes=[pltpu.VMEM((tokens, HEAD_DIM), F32)] * 3,
        compiler_params=_cparams("parallel", "parallel", "parallel"),
        name="dilated_attention",
    )(q_r, k_r, k_r, proj, proj)


def _group_count(d):
    cnt = jnp.zeros(d.shape, F32)
    for window, dil in A_GROUPS:
        hit = (d >= 0) & (d % dil == 0) & (d <= window)
        cnt = cnt + jnp.where(hit, 1.0, 0.0)
    return cnt


def _cache_attn_kernel(q_ref, kn_ref, vn_ref, kc_ref, vc_ref, o_ref, *, n_buf):
    tq = SUBLANES
    nh = A_HEADS
    scale = HEAD_DIM ** -0.5
    heads_of = lambda ref: jnp.concatenate([ref[0:tq, h * HEAD_DIM:(h + 1) * HEAD_DIM] for h in range(nh)], axis=0)
    q, k_new, v_new = heads_of(q_ref), heads_of(kn_ref), heads_of(vn_ref)
    k_c = kc_ref[...].reshape(n_buf * nh, HEAD_DIM)
    v_c = vc_ref[...].reshape(n_buf * nh, HEAD_DIM)

    r_c = _iota((nh * tq, n_buf * nh), 0)
    c_c = _iota((nh * tq, n_buf * nh), 1)
    same_c = (c_c % nh) == (r_c // tq)
    cnt_c = jnp.where(same_c, _group_count(n_buf + (r_c % tq) - (c_c // nh)), 0.0)
    r_n = _iota((nh * tq, nh * tq), 0)
    c_n = _iota((nh * tq, nh * tq), 1)
    same_n = (c_n // tq) == (r_n // tq)
    cnt_n = jnp.where(same_n, _group_count((r_n % tq) - (c_n % tq)), 0.0)

    s_c = jnp.where(cnt_c > 0, _dot_nt(q, k_c) * scale, -jnp.inf)
    s_n = jnp.where(cnt_n > 0, _dot_nt(q, k_new) * scale, -jnp.inf)
    m = jnp.maximum(jnp.max(s_c, axis=-1, keepdims=True), jnp.max(s_n, axis=-1, keepdims=True))
    p_c = cnt_c * jnp.exp(s_c - m)
    p_n = cnt_n * jnp.exp(s_n - m)
    den = jnp.sum(p_c, axis=-1, keepdims=True) + jnp.sum(p_n, axis=-1, keepdims=True)
    o = (_dot(p_c, v_c) + _dot(p_n, v_new)) / den
    o_rows = jnp.concatenate([o[h * tq:(h + 1) * tq, :] for h in range(nh)], axis=1)
    pad = jnp.zeros((o_ref.shape[0] - tq, nh * HEAD_DIM), F32)
    o_ref[...] = jnp.concatenate([o_rows, pad], axis=0).astype(o_ref.dtype)


def _cache_attention(q_r, k_r, proj, cache_k, cache_v, t_valid):
    bsz, tp, a_dim = q_r.shape
    n_buf = cache_k.shape[1]
    assert t_valid <= SUBLANES <= tp
    new = pl.BlockSpec((None, tp, a_dim), lambda b: (b, 0, 0))
    v_new = pl.BlockSpec((None, tp, a_dim), lambda b: (b, 0, 2))
    cache = pl.BlockSpec((None, n_buf, A_HEADS, HEAD_DIM), lambda b: (b, 0, 0, 0))
    return pl.pallas_call(
        functools.partial(_cache_attn_kernel, n_buf=n_buf),
        grid=(bsz,),
        in_specs=[new, new, v_new, cache, cache],
        out_specs=new,
        out_shape=jax.ShapeDtypeStruct((bsz, tp, a_dim), BF16),
        compiler_params=_cparams("parallel"),
        name="cache_attention",
    )(q_r, k_r, proj, cache_k, cache_v)


def _gdn_prep_kernel(raw_ref, prev_ref, buf_ref, cw_ref, gb_ref, gbt_ref, alr_ref, dtr_ref, alc_ref, dtc_ref,
                     u_ref, w_ref, qg_ref, kd_ref, aqk_ref, eg_ref, halo_ref, *, t_valid):
    c = pl.program_id(1)
    ch = B_CHUNK
    hd = HEAD_DIM
    b_dim = B_HEADS * hd

    halo_ref[0:SUBLANES, :] = jnp.where(c == 0, buf_ref[...], prev_ref[...])
    halo_ref[SUBLANES:SUBLANES + ch, :] = raw_ref[...]
    y = raw_ref[...] * cw_ref[B_CONV - 1:B_CONV, :]
    for j in range(B_CONV - 1):
        lag = B_CONV - 1 - j
        y = y + halo_ref[SUBLANES - lag:SUBLANES - lag + ch, :] * cw_ref[j:j + 1, :]
    act = _silu(y)

    row_ok = (c * ch + _iota((ch, 1), 0)) < t_valid
    col_ok = (c * ch + _iota((1, ch), 1)) < t_valid

    gb = gb_ref[...]
    g_col = jnp.where(row_ok, -jnp.exp(alr_ref[...]) * _softplus(gb + dtr_ref[...]), 0.0)
    beta_col = jnp.where(row_ok, _sigmoid(gb), 0.0)
    gbt = gbt_ref[...]
    g_row = jnp.where(col_ok, -jnp.exp(alc_ref[:, :ch]) * _softplus(gbt + dtc_ref[:, :ch]), 0.0)

    ri = _iota((ch, ch), 0)
    ci = _iota((ch, ch), 1)
    tri = ri >= ci
    strict = ri > ci
    eye = jnp.where(ri == ci, 1.0, 0.0)
    cum_col = _dot_ones(jnp.where(tri, 1.0, 0.0).astype(BF16), g_col, False)
    cum_row = _dot_ones(jnp.where(ri <= ci, 1.0, 0.0).astype(BF16), g_row, True)

    heads = range(B_HEADS)
    qs, ks, vs, ccs, betas, decays, lows = [], [], [], [], [], [], []
    for h in heads:
        q = act[:, h * hd:(h + 1) * hd]
        k = act[:, b_dim + h * hd:b_dim + (h + 1) * hd]
        v = act[:, 2 * b_dim + h * hd:2 * b_dim + (h + 1) * hd]
        q = q * lax.rsqrt(jnp.sum(q * q, axis=-1, keepdims=True) + 1e-6) * (hd ** -0.5)
        k = k * lax.rsqrt(jnp.sum(k * k, axis=-1, keepdims=True) + 1e-6)
        qs.append(jnp.where(row_ok, q, 0.0))
        ks.append(jnp.where(row_ok, k, 0.0))
        vs.append(jnp.where(row_ok, v, 0.0))
        cc = cum_col[:, h:h + 1]
        cr = cum_row[h:h + 1, :]
        ccs.append(cc)
        betas.append(beta_col[:, B_HEADS + h:B_HEADS + h + 1])
        decays.append(jnp.where(tri, jnp.exp(jnp.where(tri, cc - cr, 0.0)), 0.0))
    qk_kk = [_dot_nt(jnp.concatenate([qs[h], ks[h]], axis=0), ks[h]) for h in heads]
    lows = [jnp.where(strict, betas[h] * qk_kk[h][ch:, :] * decays[h], 0.0) for h in heads]
    invs = [eye - lows[h] for h in heads]
    pw_parts = [_split2(lows[h]) for h in heads]
    pws = [_dot_x3(pw_parts[h], pw_parts[h]) for h in heads]
    size = 2
    while size < ch:
        pw_parts = [_split2(pws[h]) for h in heads]
        invs = [invs[h] + _dot_x3(_split2(invs[h]), pw_parts[h]) for h in heads]
        size *= 2
        if size < ch:
            pws = [_dot_x3(pw_parts[h], pw_parts[h]) for h in heads]
    rhs = [jnp.concatenate([vs[h] * betas[h], ks[h] * (betas[h] * jnp.exp(ccs[h]))], axis=1) for h in heads]
    uws = [_dot_x3(_split2(invs[h]), _split2(rhs[h])) for h in heads]
    for h in heads:
        sl = slice(h * hd, (h + 1) * hd)
        cc = ccs[h]
        c_last = cc[ch - 1:ch, :]
        u_ref[:, sl] = uws[h][:, :hd]
        w_ref[:, sl] = uws[h][:, hd:]
        qg_ref[:, sl] = qs[h] * jnp.exp(cc)
        kd_ref[:, sl] = ks[h] * jnp.exp(c_last - cc)
        aqk_ref[:, sl] = jnp.concatenate([qk_kk[h][:ch, :] * decays[h], jnp.zeros((ch, hd - ch), F32)], axis=1)
        eg_ref[h:h + 1, :] = jnp.broadcast_to(jnp.exp(c_last), (1, hd))


def _gdn_prep(proj, buf8, conv_w, gb_src, gb_blk, gbt, a_log, dt_bias, t_valid):
    bsz, t, _ = proj.shape
    c3 = conv_w.shape[1]
    ch = B_CHUNK
    nc = t // ch
    b_dim = B_HEADS * HEAD_DIM
    pad = jnp.zeros((LANES - B_HEADS,), F32)
    al_row = jnp.concatenate([a_log.astype(F32), pad]).reshape(1, LANES)
    dt_row = jnp.concatenate([dt_bias.astype(F32), pad]).reshape(1, LANES)
    pad_c = jnp.zeros((2 * SUBLANES - B_HEADS,), F32)
    al_col = jnp.broadcast_to(jnp.concatenate([a_log.astype(F32), pad_c])[:, None], (2 * SUBLANES, LANES))
    dt_col = jnp.broadcast_to(jnp.concatenate([dt_bias.astype(F32), pad_c])[:, None], (2 * SUBLANES, LANES))
    full = lambda shape: pl.BlockSpec(shape, lambda b, c: (0,) * len(shape))
    row_out = jax.ShapeDtypeStruct((bsz, t, b_dim), F32)
    row_spec = pl.BlockSpec((None, ch, b_dim), lambda b, c: (b, c, 0))
    return pl.pallas_call(
        functools.partial(_gdn_prep_kernel, t_valid=t_valid),
        grid=(bsz, nc),
        in_specs=[pl.BlockSpec((None, ch, c3), lambda b, c: (b, c, 1)),
                  pl.BlockSpec((None, SUBLANES, c3), lambda b, c: (b, jnp.maximum(c * (ch // SUBLANES) - 1, 0), 1)),
                  pl.BlockSpec((None, SUBLANES, c3), lambda b, c: (b, 0, 0)),
                  full((B_CONV, c3)),
                  pl.BlockSpec((None, ch, LANES), lambda b, c: (b, c, gb_blk)),
                  pl.BlockSpec((None, None, 2 * SUBLANES, ch), lambda b, c: (b, c, 0, 0)),
                  full((1, LANES)), full((1, LANES)),
                  full((2 * SUBLANES, LANES)), full((2 * SUBLANES, LANES))],
        out_specs=[row_spec] * 5 + [pl.BlockSpec((None, None, B_HEADS, HEAD_DIM), lambda b, c: (b, c, 0, 0))],
        out_shape=[row_out] * 5 + [jax.ShapeDtypeStruct((bsz, nc, B_HEADS, HEAD_DIM), F32)],
        scratch_shapes=[pltpu.VMEM((SUBLANES + ch, c3), F32)],
        compiler_params=_cparams("parallel", "parallel"),
        name="gdn_prep",
    )(proj, proj, buf8, conv_w, gb_src, gbt, al_row, dt_row, al_col, dt_col)


def _gdn_scan_kernel(u_ref, w_ref, qg_ref, kd_ref, aqk_ref, eg_ref, s0_ref, z_ref, nw_ref,
                     o_ref, sfin_ref, s_ref):
    c = pl.program_id(1)
    ch = B_CHUNK
    hd = HEAD_DIM

    @pl.when(c == 0)
    def _():
        s_ref[...] = s0_ref[...]

    heads = range(B_HEADS)
    sls = [slice(h * hd, (h + 1) * hd) for h in heads]
    ss = [s_ref[h].astype(BF16) for h in heads]
    ws_qs = [_dot(jnp.concatenate([w_ref[:, sls[h]], qg_ref[:, sls[h]]], axis=0), ss[h]) for h in heads]
    v_news = [u_ref[:, sls[h]] - ws_qs[h][:ch, :] for h in heads]
    os_ = [ws_qs[h][ch:, :] + _dot(aqk_ref[:, h * hd:h * hd + ch], v_news[h]) for h in heads]
    upd = [_dot_tn(kd_ref[:, sls[h]], v_news[h]) for h in heads]
    for h in heads:
        s_ref[h] = s_ref[h] * eg_ref[h:h + 1, :] + upd[h]
        o_ref[:, sls[h]] = (_rms(os_[h], nw_ref[...]) * _silu(z_ref[:, sls[h]])).astype(o_ref.dtype)

    @pl.when(c == pl.num_programs(1) - 1)
    def _():
        sfin_ref[...] = s_ref[...]


def _gdn_scan(u, w, qg, kd, aqk, eg, s0, z_src, norm_w):
    bsz, t, b_dim = u.shape
    ch = B_CHUNK
    nc = t // ch
    row_spec = pl.BlockSpec((None, ch, b_dim), lambda b, c: (b, c, 0))
    st_spec = pl.BlockSpec((None, B_HEADS, HEAD_DIM, HEAD_DIM), lambda b, c: (b, 0, 0, 0))
    return pl.pallas_call(
        _gdn_scan_kernel,
        grid=(bsz, nc),
        in_specs=[row_spec] * 5 + [pl.BlockSpec((None, None, B_HEADS, HEAD_DIM), lambda b, c: (b, c, 0, 0)),
                                   st_spec, row_spec, pl.BlockSpec((1, HEAD_DIM), lambda b, c: (0, 0))],
        out_specs=[row_spec, st_spec],
        out_shape=[jax.ShapeDtypeStruct((bsz, t, b_dim), BF16),
                   jax.ShapeDtypeStruct((bsz, B_HEADS, HEAD_DIM, HEAD_DIM), F32)],
        scratch_shapes=[pltpu.VMEM((B_HEADS, HEAD_DIM, HEAD_DIM), F32)],
        compiler_params=_cparams("parallel", "arbitrary"),
        name="gdn_scan",
    )(u, w, qg, kd, aqk, eg, s0, z_src, norm_w)


def _hgrn_kernel(q_ref, f_ref, i_ref, g_ref, llb_ref, l1m_ref, oml_ref, s0_ref, nw_ref,
                 o_ref, sfin_ref, st_ref, *, t_valid):
    c = pl.program_id(1)
    ch = C_CHUNK
    hd = HEAD_DIM

    @pl.when(c == 0)
    def _():
        for h in range(C_HEADS):
            st_ref[h] = s0_ref[h].T

    ri = _iota((ch, ch), 0)
    ci = _iota((ch, ch), 1)
    sels, level_masks = [jnp.where(ri >= ci, 1.0, 0.0)], []
    half = SUBLANES
    while half < ch:
        base = (ri // (2 * half)) * (2 * half)
        ref_row = base + half - 1
        later = (ri - base) >= half
        lo_col = jnp.where(later, ref_row, ri)
        hi_col = jnp.where(later, ri, ref_row)
        sels.append(jnp.where(ci > lo_col, jnp.where(ci <= hi_col, 1.0, 0.0), 0.0))
        cbase = (ci // (2 * half)) * (2 * half)
        level_masks.append(later & (cbase == base) & ((ci - cbase) < half))
        half *= 2
    sel_all = jnp.concatenate(sels, axis=0).astype(BF16)
    lag_masks = [(ci == ri - lag) & ((ri % SUBLANES) >= lag) for lag in range(SUBLANES)]
    ones = jnp.ones((hd, hd), BF16)
    row_ok = None if t_valid is None else (c * ch + _iota((ch, 1), 0)) < t_valid

    def gates(h):
        sl = slice(h * hd, (h + 1) * hd)
        f = f_ref[:, sl]
        log_sig = jnp.minimum(f, 0.0) - jnp.log(1.0 + jnp.exp(-jnp.abs(f)))
        a = llb_ref[:, sl]
        b = l1m_ref[:, sl] + log_sig
        log2_f = (jnp.maximum(a, b) + jnp.log(1.0 + jnp.exp(-jnp.abs(a - b)))) * _LOG2_E
        k = oml_ref[:, sl] * _sigmoid(-f)
        q = q_ref[:, sl] * (hd ** -0.5)
        v = i_ref[:, sl]
        if row_ok is not None:
            log2_f = jnp.where(row_ok, log2_f, 0.0)
            k = jnp.where(row_ok, k, 0.0)
            q = jnp.where(row_ok, q, 0.0)
            v = jnp.where(row_ok, v, 0.0)
        return log2_f, q, k, v

    def level_att(sums, q, k):
        att = jnp.zeros((ch, ch), F32)
        for i, mask in enumerate(level_masks):
            e = jnp.exp2(sums[(i + 1) * ch:(i + 2) * ch, :])
            att = att + jnp.where(mask, _dot_nt(q * e, k * e), 0.0)
        return att

    def lag_stack(sums, q, k):
        cum = sums[:ch, :]
        terms = [q * k]
        for lag in range(1, SUBLANES):
            terms.append(q * _roll_in_blocks(k, lag) * jnp.exp2(cum - _roll_in_blocks(cum, lag)))
        return jnp.concatenate(terms, axis=0).astype(BF16)

    def add_lags(att, row_sums):
        for lag in range(SUBLANES):
            att = att + jnp.where(lag_masks[lag], row_sums[lag * ch:(lag + 1) * ch, :ch], 0.0)
        return att

    for h0 in range(0, C_HEADS, _HGRN_HEAD_GROUP):
        heads = range(h0, h0 + _HGRN_HEAD_GROUP)
        gs = [gates(h) for h in heads]
        sums = [_dot_ones(sel_all, g[0], False) for g in gs]
        atts = [level_att(s, g[1], g[2]) for s, g in zip(sums, gs)]
        row_sums = [jnp.dot(lag_stack(s, g[1], g[2]), ones, preferred_element_type=F32) for s, g in zip(sums, gs)]
        atts = [add_lags(a, r) for a, r in zip(atts, row_sums)]
        for h, s, (_, q, k, v), att in zip(heads, sums, gs, atts):
            cum = s[:ch, :]
            c_last = cum[ch - 1:ch, :]
            st = st_ref[h]
            o = _dot_nt(q * jnp.exp2(cum), st) + _dot(att, v)
            st_ref[h] = st * jnp.exp2(c_last) + _dot_tn(v, k * jnp.exp2(c_last - cum))
            sl = slice(h * hd, (h + 1) * hd)
            o_ref[:, sl] = (_rms(o, nw_ref[...]) * _silu(g_ref[:, sl])).astype(o_ref.dtype)

    @pl.when(c == pl.num_programs(1) - 1)
    def _():
        for h in range(C_HEADS):
            sfin_ref[h] = st_ref[h].T


def _hgrn(proj, lb, s0, norm_w, t_valid):
    bsz, t, four_w = proj.shape
    width = four_w // 4
    ch = C_CHUNK
    nc = t // ch
    lb = lb.astype(F32).reshape(1, width)
    col = lambda j: pl.BlockSpec((None, ch, width), lambda b, c: (b, c, j))
    vec = pl.BlockSpec((1, width), lambda b, c: (0, 0))
    st_spec = pl.BlockSpec((None, C_HEADS, HEAD_DIM, HEAD_DIM), lambda b, c: (b, 0, 0, 0))
    return pl.pallas_call(
        functools.partial(_hgrn_kernel, t_valid=None if t_valid == t else t_valid),
        grid=(bsz, nc),
        in_specs=[col(0), col(1), col(2), col(3), vec, vec, vec, st_spec,
                  pl.BlockSpec((1, HEAD_DIM), lambda b, c: (0, 0))],
        out_specs=[pl.BlockSpec((None, ch, width), lambda b, c: (b, c, 0)), st_spec],
        out_shape=[jax.ShapeDtypeStruct((bsz, t, width), BF16),
                   jax.ShapeDtypeStruct((bsz, C_HEADS, HEAD_DIM, HEAD_DIM), F32)],
        scratch_shapes=[pltpu.VMEM((C_HEADS, HEAD_DIM, HEAD_DIM), F32)],
        compiler_params=_cparams("parallel", "arbitrary"),
        name="hgrn2",
    )(proj, proj, proj, proj, jnp.log(lb), jnp.log1p(-lb), 1.0 - lb, s0, norm_w)


def _ab_mixer(h2d, bsz, t, t_valid, rope_tabs, kv_past, s0, conv_buf, w_in2d, j, w_zgb, w_out2d,
              conv_w, a_log, dt_bias, norm_w, tm):
    a_dim = A_HEADS * HEAD_DIM
    b_dim = B_HEADS * HEAD_DIM
    d = h2d.shape[1]
    main_cols = 3 * a_dim + 3 * b_dim
    proj = _mm([h2d], w_in2d, k_block=j, n0=0, n=main_cols, tn=1024, tm=tm).reshape(bsz, t, main_cols)
    zgb = _mm([h2d], w_zgb, k_block=0, n0=0, n=w_zgb.shape[1], tn=w_zgb.shape[1], tm=tm)
    zgb = zgb.reshape(bsz, t, w_zgb.shape[1])

    q_r, k_r = _rope(proj, *rope_tabs)
    if kv_past is None:
        o_a = _dilated_attention(q_r, k_r, proj)
        keep = min(A_GROUPS[-1][0], t_valid)
        k_rows = k_r[:, t_valid - keep:t_valid]
        v_rows = proj[:, t_valid - keep:t_valid, 2 * a_dim:3 * a_dim]
    else:
        o_a = _cache_attention(q_r, k_r, proj, kv_past[0], kv_past[1], t_valid)
        k_rows = k_r[:, :t_valid]
        v_rows = proj[:, :t_valid, 2 * a_dim:3 * a_dim]
    k_rows = k_rows.reshape(bsz, -1, A_HEADS, HEAD_DIM)
    v_rows = v_rows.reshape(bsz, -1, A_HEADS, HEAD_DIM)

    raw_tail = proj[:, max(t_valid - (B_CONV - 1), 0):t_valid, 3 * a_dim:]
    buf_new = jnp.concatenate([conv_buf, raw_tail], axis=1)[:, -(B_CONV - 1):]
    buf8 = jnp.pad(conv_buf, ((0, 0), (SUBLANES - (B_CONV - 1), 0), (0, 0)))
    nc = t // B_CHUNK
    gbt = zgb[:, :, b_dim:b_dim + 2 * SUBLANES].reshape(bsz, nc, B_CHUNK, 2 * SUBLANES).swapaxes(2, 3)
    u, w, qg, kd, aqk, eg = _gdn_prep(proj, buf8, conv_w, zgb, b_dim // LANES, gbt, a_log, dt_bias, t_valid)
    o_b, s_new = _gdn_scan(u, w, qg, kd, aqk, eg, s0, zgb, norm_w.reshape(1, HEAD_DIM))

    out = _mm([o_a.reshape(bsz * t, a_dim), o_b.reshape(bsz * t, b_dim)], w_out2d,
              k_block=j, n0=0, n=d, tn=1024, tm=tm)
    return out, (k_rows, v_rows, s_new, buf_new)


def _hgrn_mixer(h2d, bsz, t, t_valid, s0, w_in2d, j, w_out2d, lb, norm_w, tm):
    d = h2d.shape[1]
    n_proj = w_in2d.shape[1]
    proj = _mm([h2d], w_in2d, k_block=j, n0=0, n=n_proj, tn=1024, tm=tm).reshape(bsz, t, n_proj)
    o, s_new = _hgrn(proj, lb, s0, norm_w.reshape(1, HEAD_DIM), t_valid)
    out = _mm([o.reshape(bsz * t, n_proj // 4)], w_out2d, k_block=j, n0=0, n=d, tn=1024, tm=tm)
    return out, s_new


def _trunk(x, t_valid, pos, mods5, boff, past, p):
    bsz, t, d = x.shape
    depth = p["norm_pre"].shape[0]
    tm = 1024
    rope_tabs = _rope_tables(pos)
    a_dim = A_HEADS * HEAD_DIM
    b_dim = B_HEADS * HEAD_DIM
    d_ff = p["ffn_wo"].shape[2]
    wi2d = p["ffn_wi"].reshape(-1, 2 * d_ff)
    wo2d = p["ffn_wo"].reshape(-1, d)
    ab_in2d = p["ab_w_in"].reshape(-1, p["ab_w_in"].shape[2])
    ab_out2d = p["ab_w_out"].reshape(-1, d)
    c_in2d = p["c_w_in"].reshape(-1, p["c_w_in"].shape[2])
    c_out2d = p["c_w_out"].reshape(-1, d)
    norm_pre = p["norm_pre"].reshape(depth, 3, 1, d)
    norm_post = p["norm_post"].reshape(depth, 3, 1, d)
    lb_all = jnp.cumsum(jax.nn.softmax(p["c_lower_bounds"].astype(F32), axis=0), axis=0)
    lb_all = lb_all - lb_all[0:1]

    a_k, a_v, b_s, b_conv, c_s = [], [], [], [], []
    (h,) = _norm_call(x, mods5, boff, pre=(norm_pre[0, 0], 0, 0, 1))
    for layer in range(depth):
        j = layer // 2
        for sub in range(3):
            h2d = h.reshape(bsz * t, d)
            if sub == 1:
                if layer % 2 == 0:
                    w_zgb = p["w_zgb"][j]
                    if past is None:
                        kv_past = None
                        s0 = jnp.zeros((bsz, B_HEADS, HEAD_DIM, HEAD_DIM), F32)
                        buf = jnp.zeros((bsz, B_CONV - 1, 3 * b_dim), F32)
                    else:
                        kv_past, s0, buf = (past[0][j], past[1][j]), past[2][j], past[3][j]
                    y, (k_rows, v_rows, s_new, buf_new) = _ab_mixer(
                        h2d, bsz, t, t_valid, rope_tabs, kv_past, s0, buf, ab_in2d, j, w_zgb, ab_out2d,
                        p["b_conv_w"][j], p["b_a_log"][j], p["b_dt_bias"][j], p["b_norm"][j], tm)
                    a_k.append(k_rows)
                    a_v.append(v_rows)
                    b_s.append(s_new)
                    b_conv.append(buf_new)
                else:
                    s0 = (jnp.zeros((bsz, C_HEADS, HEAD_DIM, HEAD_DIM), F32) if past is None else past[4][j])
                    y, s_new = _hgrn_mixer(h2d, bsz, t, t_valid, s0, c_in2d, j, c_out2d, lb_all[j],
                                           p["c_norm"][j], tm)
                    c_s.append(s_new)
                coef = 1.0
            else:
                i_ffn = 0 if sub == 0 else 1
                act = _swiglu_in(h2d, wi2d, layer * 2 + i_ffn, tm)
                y = _mm([act], wo2d, k_block=layer * 2 + i_ffn, n0=0, n=d, tn=512, tm=512)
                coef = FFN_RESIDUAL
            post = (y.reshape(bsz, t, d), norm_post[layer, sub], layer, 3 * sub + 2, coef)
            if sub < 2:
                pre = (norm_pre[layer, sub + 1], layer, 3 * (sub + 1), 3 * (sub + 1) + 1)
            elif layer + 1 < depth:
                pre = (norm_pre[layer + 1, 0], layer + 1, 0, 1)
            else:
                pre = None
            outs = _norm_call(x, mods5, boff, post=post, pre=pre)
            x = outs[0]
            h = outs[1] if pre is not None else None
    return x, jnp.stack(a_k), jnp.stack(a_v), jnp.stack(b_s), jnp.stack(b_conv), jnp.stack(c_s)


def kernel(x_prompt, x_sample, cache_a_k, cache_a_v, state_b_s, state_b_conv, state_c_s, c_prompt, c_sample,
           ada_w, ada_b, norm_pre, norm_post, ffn_wi, ffn_wo, ab_w_in, ab_w_out, b_conv_w, b_a_log, b_dt_bias,
           b_norm, c_w_in, c_w_out, c_lower_bounds, c_norm):
    p = dict(norm_pre=norm_pre, norm_post=norm_post, ffn_wi=ffn_wi, ffn_wo=ffn_wo, ab_w_in=ab_w_in,
             ab_w_out=ab_w_out, b_conv_w=b_conv_w, b_a_log=b_a_log, b_dt_bias=b_dt_bias, b_norm=b_norm,
             c_w_in=c_w_in, c_w_out=c_w_out, c_lower_bounds=c_lower_bounds, c_norm=c_norm)
    depth, d = norm_pre.shape[0], x_prompt.shape[2]
    main_cols = (3 * A_HEADS + 3 * B_HEADS) * HEAD_DIM
    gb_cols = 2 * B_HEADS
    p["w_zgb"] = [jnp.concatenate([ab_w_in[j, :, main_cols + gb_cols:], ab_w_in[j, :, main_cols:main_cols + gb_cols],
                                   jnp.zeros((d, LANES - gb_cols), F32)], axis=1)
                  for j in range(ab_w_in.shape[0])]
    n_p, t_p = x_prompt.shape[0], x_prompt.shape[1]
    n_s, t_s = x_sample.shape[0], x_sample.shape[1]
    past_len = 16384

    rows = -(-(n_p + n_s) // (2 * SUBLANES)) * (2 * SUBLANES)
    c_all = jnp.concatenate([c_prompt, c_sample, jnp.zeros((rows - n_p - n_s, d), F32)], axis=0)
    mods5 = _ada_mods(c_all, ada_w, ada_b).reshape(depth, rows, N_MOD, 1, d)

    pos_p = jnp.arange(t_p, dtype=jnp.int32)
    y_p, ak_p, av_p, bs_p, bc_p, cs_p = _trunk(x_prompt, t_p, pos_p, mods5, 0, None, p)

    x_s = jnp.pad(x_sample, ((0, 0), (0, SAMPLE_T_PAD - t_s), (0, 0)))
    pos_s = past_len + jnp.arange(SAMPLE_T_PAD, dtype=jnp.int32)
    past = (cache_a_k, cache_a_v, state_b_s, state_b_conv, state_c_s)
    y_s, ak_s, av_s, bs_s, bc_s, cs_s = _trunk(x_s, t_s, pos_s, mods5, n_p, past, p)
    return (y_p, y_s[:, :t_s], ak_p, av_p, bs_p, bc_p, cs_p, ak_s, av_s, bs_s, bc_s, cs_s)
```

```python
import functools
import math

import jax
import jax.numpy as jnp
from jax import lax
from jax.experimental import pallas as pl
from jax.experimental.pallas import tpu as pltpu

F32 = jnp.float32
BF16 = jnp.bfloat16
HIGHEST = lax.Precision.HIGHEST

LANES = 128
SUBLANES = 8
VMEM_LIMIT_BYTES = 56 * 1024 * 1024

_LOG2_E = 1.4426950408889634
NORM_EPS = 1e-6
FFN_RESIDUAL = 0.5
N_MOD = 9
ROPE_THETA = 10000.0
A_HEADS = 8
A_GROUPS = ((128, 1), (512, 4), (2048, 16))
A_BLOCK = 128
B_HEADS = 8
B_CONV = 4
B_CHUNK = 64
C_HEADS = 16
HEAD_DIM = 128
SAMPLE_T_PAD = 64
C_CHUNK = 64
_HGRN_HEAD_GROUP = 16


def _cparams(*sem):
    return pltpu.CompilerParams(dimension_semantics=sem, vmem_limit_bytes=VMEM_LIMIT_BYTES)


def _sigmoid(x):
    return 1.0 / (1.0 + jnp.exp(-x))


def _silu(x):
    return x * _sigmoid(x)


def _softplus(x):
    return jnp.maximum(x, 0.0) + jnp.log1p(jnp.exp(-jnp.abs(x)))


def _rms(x, g):
    return x * lax.rsqrt(jnp.mean(x * x, axis=-1, keepdims=True) + NORM_EPS) * g


def _dot(a, b):
    return jnp.dot(a.astype(BF16), b.astype(BF16), preferred_element_type=F32)


def _dot_nt(a, b):
    return lax.dot_general(a.astype(BF16), b.astype(BF16), (((1,), (1,)), ((), ())),
                           preferred_element_type=F32)


def _dot_tn(a, b):
    return lax.dot_general(a.astype(BF16), b.astype(BF16), (((0,), (0,)), ((), ())),
                           preferred_element_type=F32)


def _split2(x):
    hi = x.astype(BF16)
    lo = (x - hi.astype(F32)).astype(BF16)
    return hi, lo


def _dot_x3(a_parts, b_parts):
    (ah, al), (bh, bl) = a_parts, b_parts
    dot = functools.partial(jnp.dot, preferred_element_type=F32)
    return dot(ah, bh) + (dot(ah, bl) + dot(al, bh))


def _dot_ones(ones_bf16, x, ones_on_right):
    hi = x.astype(BF16)
    r1 = x - hi.astype(F32)
    mid = r1.astype(BF16)
    lo = (r1 - mid.astype(F32)).astype(BF16)
    dot = functools.partial(jnp.dot, preferred_element_type=F32)
    if ones_on_right:
        return dot(hi, ones_bf16) + (dot(mid, ones_bf16) + dot(lo, ones_bf16))
    return dot(ones_bf16, hi) + (dot(ones_bf16, mid) + dot(ones_bf16, lo))


def _roll_in_blocks(x, lag):
    rows, cols = x.shape
    x3 = x.reshape(rows // SUBLANES, SUBLANES, cols)
    return pltpu.roll(x3, lag, 1).reshape(rows, cols)


def _iota(shape, dim):
    return lax.broadcasted_iota(jnp.int32, shape, dim)


def _ada_kernel(c_ref, w_ref, b_ref, o_ref):
    a = _silu(c_ref[...])
    o_ref[...] = _dot(a, w_ref[...]) + b_ref[...]


def _ada_mods(c_all, ada_w, ada_b):
    depth, d, n = ada_w.shape
    rows = c_all.shape[0]
    tn = 1024
    return pl.pallas_call(
        _ada_kernel,
        grid=(depth, n // tn),
        in_specs=[pl.BlockSpec((rows, d), lambda l, j: (0, 0)),
                  pl.BlockSpec((None, d, tn), lambda l, j: (l, 0, j)),
                  pl.BlockSpec((None, 1, tn), lambda l, j: (l, 0, j))],
        out_specs=pl.BlockSpec((None, rows, tn), lambda l, j: (l, 0, j)),
        out_shape=jax.ShapeDtypeStruct((depth, rows, n), F32),
        compiler_params=_cparams("parallel", "parallel"),
        name="ada_mods",
    )(c_all, ada_w, ada_b.reshape(depth, 1, n))


def _norm_kernel(*refs, has_post, has_pre, coef):
    it = iter(refs)
    x_ref = next(it)
    if has_post:
        y_ref, gpost_ref, gate_ref = next(it), next(it), next(it)
    if has_pre:
        gpre_ref, shift_ref, scale_ref = next(it), next(it), next(it)
    x = x_ref[...]
    if has_post:
        xo_ref = next(it)
        x = x + (coef * gate_ref[...]) * _rms(y_ref[...], gpost_ref[...])
        xo_ref[...] = x
    if has_pre:
        h_ref = next(it)
        h = _rms(x, gpre_ref[...]) * (1.0 + scale_ref[...]) + shift_ref[...]
        h_ref[...] = h.astype(BF16)


def _norm_call(x, mods5, boff, post=None, pre=None):
    bsz, t, d = x.shape
    tt = min(t, 512)
    row = pl.BlockSpec((None, tt, d), lambda b, i: (b, i, 0))
    vec = pl.BlockSpec((1, d), lambda b, i: (0, 0))

    def mod_spec(layer, k):
        return pl.BlockSpec((None, None, None, 1, d), lambda b, i: (layer, boff + b, k, 0, 0))

    args, in_specs, out_shape, out_specs = [x], [row], [], []
    coef = 1.0
    if post is not None:
        y, g_post, layer, gate_idx, coef = post
        args += [y, g_post, mods5]
        in_specs += [row, vec, mod_spec(layer, gate_idx)]
        out_shape.append(jax.ShapeDtypeStruct(x.shape, F32))
        out_specs.append(row)
    if pre is not None:
        g_pre, layer, shift_idx, scale_idx = pre
        args += [g_pre, mods5, mods5]
        in_specs += [vec, mod_spec(layer, shift_idx), mod_spec(layer, scale_idx)]
        out_shape.append(jax.ShapeDtypeStruct(x.shape, BF16))
        out_specs.append(row)
    outs = pl.pallas_call(
        functools.partial(_norm_kernel, has_post=post is not None, has_pre=pre is not None, coef=coef),
        grid=(bsz, t // tt),
        in_specs=in_specs, out_specs=out_specs, out_shape=out_shape,
        compiler_params=_cparams("parallel", "parallel"),
        name="sandwich_norm",
    )(*args)
    return outs


def _mm_kernel(*refs, k_sizes):
    n_x = len(k_sizes)
    x_refs, w_ref, o_ref, wbf_ref = refs[:n_x], refs[n_x], refs[n_x + 1], refs[n_x + 2]

    @pl.when(pl.program_id(1) == 0)
    def _():
        wbf_ref[...] = w_ref[...].astype(BF16)

    acc, off = None, 0
    for x_ref, ks in zip(x_refs, k_sizes):
        part = jnp.dot(x_ref[...].astype(BF16), wbf_ref[off:off + ks, :], preferred_element_type=F32)
        acc = part if acc is None else acc + part
        off += ks
    o_ref[...] = acc.astype(o_ref.dtype)


def _mm(xs, w2d, *, k_block, n0, n, tn, tm, out_dtype=F32):
    m = xs[0].shape[0]
    k_sizes = tuple(x.shape[1] for x in xs)
    k = sum(k_sizes)
    tm = min(tm, m)
    assert m % tm == 0 and n % tn == 0 and n0 % tn == 0 and w2d.shape[0] % k == 0
    nb0 = n0 // tn
    in_specs = [pl.BlockSpec((tm, ks), lambda j, i: (i, 0)) for ks in k_sizes]
    in_specs.append(pl.BlockSpec((k, tn), lambda j, i: (k_block, nb0 + j)))
    return pl.pallas_call(
        functools.partial(_mm_kernel, k_sizes=k_sizes),
        grid=(n // tn, m // tm),
        in_specs=in_specs,
        out_specs=pl.BlockSpec((tm, tn), lambda j, i: (i, j)),
        out_shape=jax.ShapeDtypeStruct((m, n), out_dtype),
        scratch_shapes=[pltpu.VMEM((k, tn), BF16)],
        compiler_params=_cparams("parallel", "arbitrary"),
        name="matmul",
    )(*xs, w2d)


def _mm_norm_kernel(*refs, k_sizes, coef):
    n_x = len(k_sizes)
    x_refs = refs[:n_x]
    (w_ref, xres_ref, gpost_ref, gate_ref, gpre_ref, shift_ref, scale_ref,
     xo_ref, h_ref, wbf_ref) = refs[n_x:]

    @pl.when(pl.program_id(0) == 0)
    def _():
        wbf_ref[...] = w_ref[...].astype(BF16)

    rows = xres_ref.shape[0]
    n_split = 4 if rows % (4 * 2 * SUBLANES) == 0 else 1
    step = rows // n_split
    for r in range(n_split):
        rs = slice(r * step, (r + 1) * step)
        y, off = None, 0
        for x_ref, ks in zip(x_refs, k_sizes):
            part = jnp.dot(x_ref[rs, :].astype(BF16), wbf_ref[off:off + ks, :], preferred_element_type=F32)
            y = part if y is None else y + part
            off += ks
        x = xres_ref[rs, :] + (coef * gate_ref[...]) * _rms(y, gpost_ref[...])
        xo_ref[rs, :] = x
        h_ref[rs, :] = (_rms(x, gpre_ref[...]) * (1.0 + scale_ref[...]) + shift_ref[...]).astype(BF16)


def _mm_norm(xs, w2d, k_block, x_res, mods5, boff, post, pre):
    bsz, t, d = x_res.shape
    m = bsz * t
    k_sizes = tuple(x.shape[1] for x in xs)
    k = sum(k_sizes)
    tm = min(t, 512)
    assert t % tm == 0 and w2d.shape == (w2d.shape[0] // k * k, d)
    g_post, layer_post, gate_idx, coef = post
    g_pre, layer_pre, shift_idx, scale_idx = pre
    row = pl.BlockSpec((tm, d), lambda i: (i, 0))
    vec = pl.BlockSpec((1, d), lambda i: (0, 0))

    def mod_spec(layer, idx):
        return pl.BlockSpec((None, None, None, 1, d), lambda i: (layer, boff + (i * tm) // t, idx, 0, 0))

    in_specs = [pl.BlockSpec((tm, ks), lambda i: (i, 0)) for ks in k_sizes]
    in_specs += [pl.BlockSpec((k, d), lambda i: (k_block, 0), pipeline_mode=pl.Buffered(1)),
                 row, vec, mod_spec(layer_post, gate_idx), vec, mod_spec(layer_pre, shift_idx),
                 mod_spec(layer_pre, scale_idx)]
    x_new, h = pl.pallas_call(
        functools.partial(_mm_norm_kernel, k_sizes=k_sizes, coef=coef),
        grid=(m // tm,),
        in_specs=in_specs,
        out_specs=[row, row],
        out_shape=[jax.ShapeDtypeStruct((m, d), F32), jax.ShapeDtypeStruct((m, d), BF16)],
        scratch_shapes=[pltpu.VMEM((k, d), BF16)],
        compiler_params=_cparams("arbitrary"),
        name="matmul_norm",
    )(*xs, w2d, x_res.reshape(m, d), g_post, mods5, g_pre, mods5, mods5)
    return x_new.reshape(bsz, t, d), h.reshape(bsz, t, d)


_SWIGLU_GROUP = 4


def _swiglu_kernel(*refs):
    ng = _SWIGLU_GROUP
    x_ref, w_refs, o_ref, wbf_ref = refs[0], refs[1:1 + 2 * ng], refs[1 + 2 * ng], refs[2 + 2 * ng]

    @pl.when(pl.program_id(1) == 0)
    def _():
        for g in range(ng):
            wbf_ref[:, (2 * g) * LANES:(2 * g + 1) * LANES] = w_refs[g][...].astype(BF16)
            wbf_ref[:, (2 * g + 1) * LANES:(2 * g + 2) * LANES] = w_refs[ng + g][...].astype(BF16)

    x = x_ref[...]
    for g in range(ng):
        r = jnp.dot(x, wbf_ref[:, 2 * g * LANES:(2 * g + 2) * LANES], preferred_element_type=F32)
        u, v = r[:, :LANES], r[:, LANES:]
        o_ref[:, g * LANES:(g + 1) * LANES] = (_silu(v) * u).astype(o_ref.dtype)


def _swiglu_in(h, wi2d, k_block, tm):
    m, k = h.shape
    d_ff = wi2d.shape[1] // 2
    assert d_ff % LANES == 0
    nblk = d_ff // LANES
    ng = _SWIGLU_GROUP
    tn = ng * LANES
    tm = min(tm, m)
    last = nblk - 1
    w_specs = [pl.BlockSpec((k, LANES), lambda j, i, g=g, base=base: (k_block, base + jnp.minimum(ng * j + g, last)))
               for base in (0, nblk) for g in range(ng)]
    return pl.pallas_call(
        _swiglu_kernel,
        grid=(pl.cdiv(nblk, ng), m // tm),
        in_specs=[pl.BlockSpec((tm, k), lambda j, i: (i, 0))] + w_specs,
        out_specs=pl.BlockSpec((tm, tn), lambda j, i: (i, j)),
        out_shape=jax.ShapeDtypeStruct((m, d_ff), BF16),
        scratch_shapes=[pltpu.VMEM((k, 2 * tn), BF16)],
        compiler_params=_cparams("parallel", "arbitrary"),
        name="swiglu_in",
    )(h, *([wi2d] * (2 * ng)))


def _rope_kernel(x_ref, cos_ref, sin_ref, q_ref, k_ref):
    cos, sin = cos_ref[...], sin_ref[...]
    for h in range(2 * A_HEADS):
        xh = x_ref[:, h * HEAD_DIM:(h + 1) * HEAD_DIM]
        r = xh * cos + pltpu.roll(xh, HEAD_DIM // 2, 1) * sin
        dst = q_ref if h < A_HEADS else k_ref
        hh = h % A_HEADS
        dst[:, hh * HEAD_DIM:(hh + 1) * HEAD_DIM] = r


def _rope(proj, cos_t, sin_t):
    bsz, t, _ = proj.shape
    a_dim = A_HEADS * HEAD_DIM
    tt = min(t, 256)
    out = jax.ShapeDtypeStruct((bsz, t, a_dim), F32)
    o_spec = pl.BlockSpec((None, tt, a_dim), lambda b, i: (b, i, 0))
    return pl.pallas_call(
        _rope_kernel,
        grid=(bsz, t // tt),
        in_specs=[pl.BlockSpec((None, tt, 2 * a_dim), lambda b, i: (b, i, 0)),
                  pl.BlockSpec((tt, HEAD_DIM), lambda b, i: (i, 0)),
                  pl.BlockSpec((tt, HEAD_DIM), lambda b, i: (i, 0))],
        out_specs=[o_spec, o_spec], out_shape=[out, out],
        compiler_params=_cparams("parallel", "parallel"),
        name="rope",
    )(proj, cos_t, sin_t)


def _rope_tables(pos):
    half = HEAD_DIM // 2
    inv_freq = jnp.power(ROPE_THETA, -jnp.arange(half, dtype=F32) / half)
    ang = pos.astype(F32)[:, None] * inv_freq[None, :]
    cos, sin = jnp.cos(ang), jnp.sin(ang)
    return jnp.concatenate([cos, cos], axis=1), jnp.concatenate([-sin, sin], axis=1)


_A_TOKEN_BLOCK = A_BLOCK * max(dil for _, dil in A_GROUPS)


def _dilated_attn_kernel(q_ref, kp_ref, kc_ref, vp_ref, vc_ref, o_ref, m_ref, l_ref, acc_ref):
    tb = pl.program_id(2)
    blk = A_BLOCK
    tokens = _A_TOKEN_BLOCK
    qi = _iota((blk, 2 * blk), 0)
    ki = _iota((blk, 2 * blk), 1)
    rel = qi + blk - ki
    has_prev = (ki >= blk) | (tb > 0)
    scale = HEAD_DIM ** -0.5

    for g, (window, dil) in enumerate(A_GROUPS):
        span = window // dil
        band = (rel >= 0) & (rel <= span)
        reach = blk * dil

        def rows(ref, base, r, dil=dil):
            if dil == 1:
                return ref[base:base + blk, :]
            return ref[pl.ds(base + r, blk, stride=dil), :]

        def put(ref, base, r, val, dil=dil):
            if dil == 1:
                ref[base:base + blk, :] = val
            else:
                ref[pl.ds(base + r, blk, stride=dil), :] = val

        for r in range(dil):
            k_prev = rows(kp_ref, tokens - reach, r).astype(BF16)
            v_prev = rows(vp_ref, tokens - reach, r).astype(BF16)
            for s in range(tokens // reach):
                base = s * reach
                q = rows(q_ref, base, r)
                k_cur = rows(kc_ref, base, r).astype(BF16)
                v_cur = rows(vc_ref, base, r).astype(BF16)
                kcat = jnp.concatenate([k_prev, k_cur], axis=0)
                vcat = jnp.concatenate([v_prev, v_cur], axis=0)
                k_prev, v_prev = k_cur, v_cur
                valid = band & has_prev if s == 0 else band
                sc = jnp.where(valid, _dot_nt(q, kcat) * scale, -jnp.inf)
                m_loc = jnp.max(sc, axis=-1, keepdims=True)
                if g == 0:
                    p = jnp.exp(sc - m_loc)
                    put(m_ref, base, r, jnp.broadcast_to(m_loc, (blk, HEAD_DIM)))
                    put(l_ref, base, r, jnp.broadcast_to(jnp.sum(p, axis=-1, keepdims=True), (blk, HEAD_DIM)))
                    put(acc_ref, base, r, _dot(p, vcat))
                else:
                    m_old = rows(m_ref, base, r)
                    m_new = jnp.maximum(m_old, m_loc)
                    alpha = jnp.exp(m_old - m_new)
                    p = jnp.exp(sc - m_new[:, 0:1])
                    put(m_ref, base, r, m_new)
                    put(l_ref, base, r, alpha * rows(l_ref, base, r) + jnp.sum(p, axis=-1, keepdims=True))
                    put(acc_ref, base, r, alpha * rows(acc_ref, base, r) + _dot(p, vcat))

    o_ref[...] = (acc_ref[...] / l_ref[...]).astype(o_ref.dtype)


def _dilated_attention(q_r, k_r, proj):
    bsz, t, a_dim = q_r.shape
    tokens = _A_TOKEN_BLOCK
    assert t % tokens == 0 and all(window // dil <= A_BLOCK for window, dil in A_GROUPS)
    blk = (None, tokens, HEAD_DIM)
    v_off = 2 * a_dim // HEAD_DIM
    cur = lambda b, h, i: (b, i, h)
    prev = lambda b, h, i: (b, jnp.maximum(i - 1, 0), h)
    v_cur = lambda b, h, i: (b, i, v_off + h)
    v_prev = lambda b, h, i: (b, jnp.maximum(i - 1, 0), v_off + h)
    return pl.pallas_call(
        _dilated_attn_kernel,
        grid=(bsz, A_HEADS, t // tokens),
        in_specs=[pl.BlockSpec(blk, cur), pl.BlockSpec(blk, prev), pl.BlockSpec(blk, cur),
                  pl.BlockSpec(blk, v_prev), pl.BlockSpec(blk, v_cur)],
        out_specs=pl.BlockSpec(blk, cur),
        out_shape=jax.ShapeDtypeStruct((bsz, t, a_dim), BF16),
        scratch_shapes=[pltpu.VMEM((tokens, HEAD_DIM), F32)] * 3,
        compiler_params=_cparams("parallel", "parallel", "parallel"),
        name="dilated_attention",
    )(q_r, k_r, k_r, proj, proj)


def _group_count(d):
    cnt = jnp.zeros(d.shape, F32)
    for window, dil in A_GROUPS:
        hit = (d >= 0) & (d % dil == 0) & (d <= window)
        cnt = cnt + jnp.where(hit, 1.0, 0.0)
    return cnt


def _cache_attn_kernel(q_ref, kn_ref, vn_ref, kc_ref, vc_ref, o_ref, *, n_buf):
    tq = SUBLANES
    nh = A_HEADS
    scale = HEAD_DIM ** -0.5
    heads_of = lambda ref: jnp.concatenate([ref[0:tq, h * HEAD_DIM:(h + 1) * HEAD_DIM] for h in range(nh)], axis=0)
    q, k_new, v_new = heads_of(q_ref), heads_of(kn_ref), heads_of(vn_ref)
    k_c = kc_ref[...].reshape(n_buf * nh, HEAD_DIM)
    v_c = vc_ref[...].reshape(n_buf * nh, HEAD_DIM)

    r_c = _iota((nh * tq, n_buf * nh), 0)
    c_c = _iota((nh * tq, n_buf * nh), 1)
    same_c = (c_c % nh) == (r_c // tq)
    cnt_c = jnp.where(same_c, _group_count(n_buf + (r_c % tq) - (c_c // nh)), 0.0)
    r_n = _iota((nh * tq, nh * tq), 0)
    c_n = _iota((nh * tq, nh * tq), 1)
    same_n = (c_n // tq) == (r_n // tq)
    cnt_n = jnp.where(same_n, _group_count((r_n % tq) - (c_n % tq)), 0.0)

    s_c = jnp.where(cnt_c > 0, _dot_nt(q, k_c) * scale, -jnp.inf)
    s_n = jnp.where(cnt_n > 0, _dot_nt(q, k_new) * scale, -jnp.inf)
    m = jnp.maximum(jnp.max(s_c, axis=-1, keepdims=True), jnp.max(s_n, axis=-1, keepdims=True))
    p_c = cnt_c * jnp.exp(s_c - m)
    p_n = cnt_n * jnp.exp(s_n - m)
    den = jnp.sum(p_c, axis=-1, keepdims=True) + jnp.sum(p_n, axis=-1, keepdims=True)
    o = (_dot(p_c, v_c) + _dot(p_n, v_new)) / den
    o_rows = jnp.concatenate([o[h * tq:(h + 1) * tq, :] for h in range(nh)], axis=1)
    pad = jnp.zeros((o_ref.shape[0] - tq, nh * HEAD_DIM), F32)
    o_ref[...] = jnp.concatenate([o_rows, pad], axis=0).astype(o_ref.dtype)


def _cache_attention(q_r, k_r, proj, cache_k, cache_v, layer, t_valid):
    bsz, tp, a_dim = q_r.shape
    n_buf = cache_k.shape[2]
    assert t_valid <= SUBLANES <= tp
    new = pl.BlockSpec((None, tp, a_dim), lambda b: (b, 0, 0))
    v_new = pl.BlockSpec((None, tp, a_dim), lambda b: (b, 0, 2))
    cache = pl.BlockSpec((None, None, n_buf, A_HEADS, HEAD_DIM), lambda b: (layer, b, 0, 0, 0))
    return pl.pallas_call(
        functools.partial(_cache_attn_kernel, n_buf=n_buf),
        grid=(bsz,),
        in_specs=[new, new, v_new, cache, cache],
        out_specs=new,
        out_shape=jax.ShapeDtypeStruct((bsz, tp, a_dim), BF16),
        compiler_params=_cparams("parallel"),
        name="cache_attention",
    )(q_r, k_r, proj, cache_k, cache_v)


def _gdn_prep_kernel(raw_ref, prev_ref, buf_ref, cw_ref, gb_ref, gbt_ref, alr_ref, dtr_ref, alc_ref, dtc_ref,
                     u_ref, w_ref, qg_ref, kd_ref, aqk_ref, eg_ref, halo_ref, *, t_valid):
    c = pl.program_id(1)
    ch = B_CHUNK
    hd = HEAD_DIM
    b_dim = B_HEADS * hd

    halo_ref[0:SUBLANES, :] = jnp.where(c == 0, buf_ref[...], prev_ref[...])
    halo_ref[SUBLANES:SUBLANES + ch, :] = raw_ref[...]
    y = raw_ref[...] * cw_ref[B_CONV - 1:B_CONV, :]
    for j in range(B_CONV - 1):
        lag = B_CONV - 1 - j
        y = y + halo_ref[SUBLANES - lag:SUBLANES - lag + ch, :] * cw_ref[j:j + 1, :]
    act = _silu(y)

    row_ok = (c * ch + _iota((ch, 1), 0)) < t_valid
    col_ok = (c * ch + _iota((1, ch), 1)) < t_valid

    gb = gb_ref[...]
    g_col = jnp.where(row_ok, -jnp.exp(alr_ref[...]) * _softplus(gb + dtr_ref[...]), 0.0)
    beta_col = jnp.where(row_ok, _sigmoid(gb), 0.0)
    gbt = gbt_ref[...]
    g_row = jnp.where(col_ok, -jnp.exp(alc_ref[:, :ch]) * _softplus(gbt + dtc_ref[:, :ch]), 0.0)

    ri = _iota((ch, ch), 0)
    ci = _iota((ch, ch), 1)
    tri = ri >= ci
    strict = ri > ci
    eye = jnp.where(ri == ci, 1.0, 0.0)
    cum_col = _dot_ones(jnp.where(tri, 1.0, 0.0).astype(BF16), g_col, False)
    cum_row = _dot_ones(jnp.where(ri <= ci, 1.0, 0.0).astype(BF16), g_row, True)

    heads = range(B_HEADS)
    qs, ks, vs, ccs, betas, decays, lows = [], [], [], [], [], [], []
    for h in heads:
        q = act[:, h * hd:(h + 1) * hd]
        k = act[:, b_dim + h * hd:b_dim + (h + 1) * hd]
        v = act[:, 2 * b_dim + h * hd:2 * b_dim + (h + 1) * hd]
        q = q * lax.rsqrt(jnp.sum(q * q, axis=-1, keepdims=True) + 1e-6) * (hd ** -0.5)
        k = k * lax.rsqrt(jnp.sum(k * k, axis=-1, keepdims=True) + 1e-6)
        qs.append(jnp.where(row_ok, q, 0.0))
        ks.append(jnp.where(row_ok, k, 0.0))
        vs.append(jnp.where(row_ok, v, 0.0))
        cc = cum_col[:, h:h + 1]
        cr = cum_row[h:h + 1, :]
        ccs.append(cc)
        betas.append(beta_col[:, B_HEADS + h:B_HEADS + h + 1])
        decays.append(jnp.where(tri, jnp.exp(jnp.where(tri, cc - cr, 0.0)), 0.0))
    qk_kk = [_dot_nt(jnp.concatenate([qs[h], ks[h]], axis=0), ks[h]) for h in heads]
    lows = [jnp.where(strict, betas[h] * qk_kk[h][ch:, :] * decays[h], 0.0) for h in heads]
    invs = [eye - lows[h] for h in heads]
    pw_parts = [_split2(lows[h]) for h in heads]
    pws = [_dot_x3(pw_parts[h], pw_parts[h]) for h in heads]
    size = 2
    while size < ch:
        pw_parts = [_split2(pws[h]) for h in heads]
        invs = [invs[h] + _dot_x3(_split2(invs[h]), pw_parts[h]) for h in heads]
        size *= 2
        if size < ch:
            pws = [_dot_x3(pw_parts[h], pw_parts[h]) for h in heads]
    rhs = [jnp.concatenate([vs[h] * betas[h], ks[h] * (betas[h] * jnp.exp(ccs[h]))], axis=1) for h in heads]
    uws = [_dot_x3(_split2(invs[h]), _split2(rhs[h])) for h in heads]
    for h in heads:
        sl = slice(h * hd, (h + 1) * hd)
        cc = ccs[h]
        c_last = cc[ch - 1:ch, :]
        u_ref[:, sl] = uws[h][:, :hd]
        w_ref[:, sl] = uws[h][:, hd:]
        qg_ref[:, sl] = qs[h] * jnp.exp(cc)
        kd_ref[:, sl] = ks[h] * jnp.exp(c_last - cc)
        aqk_ref[:, sl] = jnp.concatenate([qk_kk[h][:ch, :] * decays[h], jnp.zeros((ch, hd - ch), F32)], axis=1)
        eg_ref[h:h + 1, :] = jnp.broadcast_to(jnp.exp(c_last), (1, hd))


def _gdn_prep(proj, buf8, conv_w, gb_src, gb_blk, gbt, a_log, dt_bias, t_valid):
    bsz, t, _ = proj.shape
    c3 = conv_w.shape[1]
    ch = B_CHUNK
    nc = t // ch
    b_dim = B_HEADS * HEAD_DIM
    pad = jnp.zeros((LANES - B_HEADS,), F32)
    al_row = jnp.concatenate([a_log.astype(F32), pad]).reshape(1, LANES)
    dt_row = jnp.concatenate([dt_bias.astype(F32), pad]).reshape(1, LANES)
    pad_c = jnp.zeros((2 * SUBLANES - B_HEADS,), F32)
    al_col = jnp.broadcast_to(jnp.concatenate([a_log.astype(F32), pad_c])[:, None], (2 * SUBLANES, LANES))
    dt_col = jnp.broadcast_to(jnp.concatenate([dt_bias.astype(F32), pad_c])[:, None], (2 * SUBLANES, LANES))
    full = lambda shape: pl.BlockSpec(shape, lambda b, c: (0,) * len(shape))
    row_out = jax.ShapeDtypeStruct((bsz, t, b_dim), F32)
    row_spec = pl.BlockSpec((None, ch, b_dim), lambda b, c: (b, c, 0))
    return pl.pallas_call(
        functools.partial(_gdn_prep_kernel, t_valid=t_valid),
        grid=(bsz, nc),
        in_specs=[pl.BlockSpec((None, ch, c3), lambda b, c: (b, c, 1)),
                  pl.BlockSpec((None, SUBLANES, c3), lambda b, c: (b, jnp.maximum(c * (ch // SUBLANES) - 1, 0), 1)),
                  pl.BlockSpec((None, SUBLANES, c3), lambda b, c: (b, 0, 0)),
                  full((B_CONV, c3)),
                  pl.BlockSpec((None, ch, LANES), lambda b, c: (b, c, gb_blk)),
                  pl.BlockSpec((None, None, 2 * SUBLANES, ch), lambda b, c: (b, c, 0, 0)),
                  full((1, LANES)), full((1, LANES)),
                  full((2 * SUBLANES, LANES)), full((2 * SUBLANES, LANES))],
        out_specs=[row_spec] * 5 + [pl.BlockSpec((None, None, B_HEADS, HEAD_DIM), lambda b, c: (b, c, 0, 0))],
        out_shape=[row_out] * 5 + [jax.ShapeDtypeStruct((bsz, nc, B_HEADS, HEAD_DIM), F32)],
        scratch_shapes=[pltpu.VMEM((SUBLANES + ch, c3), F32)],
        compiler_params=_cparams("parallel", "parallel"),
        name="gdn_prep",
    )(proj, proj, buf8, conv_w, gb_src, gbt, al_row, dt_row, al_col, dt_col)


def _gdn_scan_kernel(u_ref, w_ref, qg_ref, kd_ref, aqk_ref, eg_ref, s0_ref, z_ref, nw_ref,
                     o_ref, sfin_ref, s_ref):
    c = pl.program_id(1)
    ch = B_CHUNK
    hd = HEAD_DIM

    @pl.when(c == 0)
    def _():
        s_ref[...] = s0_ref[...]

    heads = range(B_HEADS)
    sls = [slice(h * hd, (h + 1) * hd) for h in heads]
    ss = [s_ref[h].astype(BF16) for h in heads]
    ws_qs = [_dot(jnp.concatenate([w_ref[:, sls[h]], qg_ref[:, sls[h]]], axis=0), ss[h]) for h in heads]
    v_news = [u_ref[:, sls[h]] - ws_qs[h][:ch, :] for h in heads]
    os_ = [ws_qs[h][ch:, :] + _dot(aqk_ref[:, h * hd:h * hd + ch], v_news[h]) for h in heads]
    upd = [_dot_tn(kd_ref[:, sls[h]], v_news[h]) for h in heads]
    for h in heads:
        s_ref[h] = s_ref[h] * eg_ref[h:h + 1, :] + upd[h]
        o_ref[:, sls[h]] = (_rms(os_[h], nw_ref[...]) * _silu(z_ref[:, sls[h]])).astype(o_ref.dtype)

    @pl.when(c == pl.num_programs(1) - 1)
    def _():
        sfin_ref[...] = s_ref[...]


def _gdn_scan(u, w, qg, kd, aqk, eg, s0, z_src, norm_w):
    bsz, t, b_dim = u.shape
    ch = B_CHUNK
    nc = t // ch
    row_spec = pl.BlockSpec((None, ch, b_dim), lambda b, c: (b, c, 0))
    st_spec = pl.BlockSpec((None, B_HEADS, HEAD_DIM, HEAD_DIM), lambda b, c: (b, 0, 0, 0))
    return pl.pallas_call(
        _gdn_scan_kernel,
        grid=(bsz, nc),
        in_specs=[row_spec] * 5 + [pl.BlockSpec((None, None, B_HEADS, HEAD_DIM), lambda b, c: (b, c, 0, 0)),
                                   st_spec, row_spec, pl.BlockSpec((1, HEAD_DIM), lambda b, c: (0, 0))],
        out_specs=[row_spec, st_spec],
        out_shape=[jax.ShapeDtypeStruct((bsz, t, b_dim), BF16),
                   jax.ShapeDtypeStruct((bsz, B_HEADS, HEAD_DIM, HEAD_DIM), F32)],
        scratch_shapes=[pltpu.VMEM((B_HEADS, HEAD_DIM, HEAD_DIM), F32)],
        compiler_params=_cparams("parallel", "arbitrary"),
        name="gdn_scan",
    )(u, w, qg, kd, aqk, eg, s0, z_src, norm_w)


def _hgrn_kernel(q_ref, f_ref, i_ref, g_ref, llb_ref, l1m_ref, oml_ref, s0_ref, nw_ref,
                 o_ref, sfin_ref, st_ref, *, t_valid):
    c = pl.program_id(1)
    ch = C_CHUNK
    hd = HEAD_DIM

    @pl.when(c == 0)
    def _():
        for h in range(C_HEADS):
            st_ref[h] = s0_ref[h].T

    ri = _iota((ch, ch), 0)
    ci = _iota((ch, ch), 1)
    sels, level_masks = [jnp.where(ri >= ci, 1.0, 0.0)], []
    half = SUBLANES
    while half < ch:
        base = (ri // (2 * half)) * (2 * half)
        ref_row = base + half - 1
        later = (ri - base) >= half
        lo_col = jnp.where(later, ref_row, ri)
        hi_col = jnp.where(later, ri, ref_row)
        sels.append(jnp.where(ci > lo_col, jnp.where(ci <= hi_col, 1.0, 0.0), 0.0))
        cbase = (ci // (2 * half)) * (2 * half)
        level_masks.append(later & (cbase == base) & ((ci - cbase) < half))
        half *= 2
    sel_all = jnp.concatenate(sels, axis=0).astype(BF16)
    lag_masks = [(ci == ri - lag) & ((ri % SUBLANES) >= lag) for lag in range(SUBLANES)]
    ones = jnp.ones((hd, hd), BF16)
    row_ok = None if t_valid is None else (c * ch + _iota((ch, 1), 0)) < t_valid

    def gates(h):
        sl = slice(h * hd, (h + 1) * hd)
        f = f_ref[:, sl]
        log_sig = jnp.minimum(f, 0.0) - jnp.log(1.0 + jnp.exp(-jnp.abs(f)))
        a = llb_ref[:, sl]
        b = l1m_ref[:, sl] + log_sig
        log2_f = (jnp.maximum(a, b) + jnp.log(1.0 + jnp.exp(-jnp.abs(a - b)))) * _LOG2_E
        k = oml_ref[:, sl] * _sigmoid(-f)
        q = q_ref[:, sl] * (hd ** -0.5)
        v = i_ref[:, sl]
        if row_ok is not None:
            log2_f = jnp.where(row_ok, log2_f, 0.0)
            k = jnp.where(row_ok, k, 0.0)
            q = jnp.where(row_ok, q, 0.0)
            v = jnp.where(row_ok, v, 0.0)
        return log2_f, q, k, v

    def level_att(sums, q, k):
        att = jnp.zeros((ch, ch), F32)
        for i, mask in enumerate(level_masks):
            e = jnp.exp2(sums[(i + 1) * ch:(i + 2) * ch, :])
            att = att + jnp.where(mask, _dot_nt(q * e, k * e), 0.0)
        return att

    def lag_stack(sums, q, k):
        cum = sums[:ch, :]
        terms = [q * k]
        for lag in range(1, SUBLANES):
            terms.append(q * _roll_in_blocks(k, lag) * jnp.exp2(cum - _roll_in_blocks(cum, lag)))
        return jnp.concatenate(terms, axis=0).astype(BF16)

    def add_lags(att, row_sums):
        for lag in range(SUBLANES):
            att = att + jnp.where(lag_masks[lag], row_sums[lag * ch:(lag + 1) * ch, :ch], 0.0)
        return att

    for h0 in range(0, C_HEADS, _HGRN_HEAD_GROUP):
        heads = range(h0, h0 + _HGRN_HEAD_GROUP)
        gs = [gates(h) for h in heads]
        sums = [_dot_ones(sel_all, g[0], False) for g in gs]
        atts = [level_att(s, g[1], g[2]) for s, g in zip(sums, gs)]
        row_sums = [jnp.dot(lag_stack(s, g[1], g[2]), ones, preferred_element_type=F32) for s, g in zip(sums, gs)]
        atts = [add_lags(a, r) for a, r in zip(atts, row_sums)]
        for h, s, (_, q, k, v), att in zip(heads, sums, gs, atts):
            cum = s[:ch, :]
            c_last = cum[ch - 1:ch, :]
            st = st_ref[h]
            o = _dot_nt(q * jnp.exp2(cum), st) + _dot(att, v)
            st_ref[h] = st * jnp.exp2(c_last) + _dot_tn(v, k * jnp.exp2(c_last - cum))
            sl = slice(h * hd, (h + 1) * hd)
            o_ref[:, sl] = (_rms(o, nw_ref[...]) * _silu(g_ref[:, sl])).astype(o_ref.dtype)

    @pl.when(c == pl.num_programs(1) - 1)
    def _():
        for h in range(C_HEADS):
            sfin_ref[h] = st_ref[h].T


def _hgrn(proj, lb, s0, norm_w, t_valid):
    bsz, t, four_w = proj.shape
    width = four_w // 4
    ch = C_CHUNK
    nc = t // ch
    lb = lb.astype(F32).reshape(1, width)
    col = lambda j: pl.BlockSpec((None, ch, width), lambda b, c: (b, c, j))
    vec = pl.BlockSpec((1, width), lambda b, c: (0, 0))
    st_spec = pl.BlockSpec((None, C_HEADS, HEAD_DIM, HEAD_DIM), lambda b, c: (b, 0, 0, 0))
    return pl.pallas_call(
        functools.partial(_hgrn_kernel, t_valid=None if t_valid == t else t_valid),
        grid=(bsz, nc),
        in_specs=[col(0), col(1), col(2), col(3), vec, vec, vec, st_spec,
                  pl.BlockSpec((1, HEAD_DIM), lambda b, c: (0, 0))],
        out_specs=[pl.BlockSpec((None, ch, width), lambda b, c: (b, c, 0)), st_spec],
        out_shape=[jax.ShapeDtypeStruct((bsz, t, width), BF16),
                   jax.ShapeDtypeStruct((bsz, C_HEADS, HEAD_DIM, HEAD_DIM), F32)],
        scratch_shapes=[pltpu.VMEM((C_HEADS, HEAD_DIM, HEAD_DIM), F32)],
        compiler_params=_cparams("parallel", "arbitrary"),
        name="hgrn2",
    )(proj, proj, proj, proj, jnp.log(lb), jnp.log1p(-lb), 1.0 - lb, s0, norm_w)


def _ab_mixer(h2d, bsz, t, t_valid, rope_tabs, kv_past, s0, conv_buf, w_in2d, j, w_zgb, w_out2d,
              conv_w, a_log, dt_bias, norm_w, tm):
    a_dim = A_HEADS * HEAD_DIM
    b_dim = B_HEADS * HEAD_DIM
    d = h2d.shape[1]
    main_cols = 3 * a_dim + 3 * b_dim
    proj = _mm([h2d], w_in2d, k_block=j, n0=0, n=main_cols, tn=1024, tm=tm).reshape(bsz, t, main_cols)
    zgb = _mm([h2d], w_zgb, k_block=0, n0=0, n=w_zgb.shape[1], tn=w_zgb.shape[1], tm=tm)
    zgb = zgb.reshape(bsz, t, w_zgb.shape[1])

    q_r, k_r = _rope(proj, *rope_tabs)
    if kv_past is None:
        o_a = _dilated_attention(q_r, k_r, proj)
        keep = min(A_GROUPS[-1][0], t_valid)
        k_rows = k_r[:, t_valid - keep:t_valid]
        v_rows = proj[:, t_valid - keep:t_valid, 2 * a_dim:3 * a_dim]
    else:
        o_a = _cache_attention(q_r, k_r, proj, kv_past[0], kv_past[1], j, t_valid)
        k_rows = k_r[:, :t_valid]
        v_rows = proj[:, :t_valid, 2 * a_dim:3 * a_dim]
    k_rows = k_rows.reshape(bsz, -1, A_HEADS, HEAD_DIM)
    v_rows = v_rows.reshape(bsz, -1, A_HEADS, HEAD_DIM)

    raw_tail = proj[:, max(t_valid - (B_CONV - 1), 0):t_valid, 3 * a_dim:]
    buf_new = jnp.concatenate([conv_buf, raw_tail], axis=1)[:, -(B_CONV - 1):]
    buf8 = jnp.pad(conv_buf, ((0, 0), (SUBLANES - (B_CONV - 1), 0), (0, 0)))
    nc = t // B_CHUNK
    gbt = zgb[:, :, b_dim:b_dim + 2 * SUBLANES].reshape(bsz, nc, B_CHUNK, 2 * SUBLANES).swapaxes(2, 3)
    u, w, qg, kd, aqk, eg = _gdn_prep(proj, buf8, conv_w, zgb, b_dim // LANES, gbt, a_log, dt_bias, t_valid)
    o_b, s_new = _gdn_scan(u, w, qg, kd, aqk, eg, s0, zgb, norm_w.reshape(1, HEAD_DIM))

    mixed = [o_a.reshape(bsz * t, a_dim), o_b.reshape(bsz * t, b_dim)]
    return mixed, (k_rows, v_rows, s_new, buf_new)


def _hgrn_mixer(h2d, bsz, t, t_valid, s0, w_in2d, j, w_out2d, lb, norm_w, tm):
    d = h2d.shape[1]
    n_proj = w_in2d.shape[1]
    proj = _mm([h2d], w_in2d, k_block=j, n0=0, n=n_proj, tn=1024, tm=tm).reshape(bsz, t, n_proj)
    o, s_new = _hgrn(proj, lb, s0, norm_w.reshape(1, HEAD_DIM), t_valid)
    return [o.reshape(bsz * t, n_proj // 4)], s_new


def _trunk(x, t_valid, pos, mods5, boff, past, p):
    bsz, t, d = x.shape
    depth = p["norm_pre"].shape[0]
    tm = 1024
    rope_tabs = _rope_tables(pos)
    a_dim = A_HEADS * HEAD_DIM
    b_dim = B_HEADS * HEAD_DIM
    d_ff = p["ffn_wo"].shape[2]
    wi2d = p["ffn_wi"].reshape(-1, 2 * d_ff)
    wo2d = p["ffn_wo"].reshape(-1, d)
    ab_in2d = p["ab_w_in"].reshape(-1, p["ab_w_in"].shape[2])
    ab_out2d = p["ab_w_out"].reshape(-1, d)
    c_in2d = p["c_w_in"].reshape(-1, p["c_w_in"].shape[2])
    c_out2d = p["c_w_out"].reshape(-1, d)
    norm_pre = p["norm_pre"].reshape(depth, 3, 1, d)
    norm_post = p["norm_post"].reshape(depth, 3, 1, d)
    lb_all = jnp.cumsum(jax.nn.softmax(p["c_lower_bounds"].astype(F32), axis=0), axis=0)
    lb_all = lb_all - lb_all[0:1]

    a_k, a_v, b_s, b_conv, c_s = [], [], [], [], []
    (h,) = _norm_call(x, mods5, boff, pre=(norm_pre[0, 0], 0, 0, 1))
    for layer in range(depth):
        j = layer // 2
        for sub in range(3):
            h2d = h.reshape(bsz * t, d)
            if sub == 1:
                if layer % 2 == 0:
                    w_zgb = p["w_zgb"][j]
                    if past is None:
                        kv_past = None
                        s0 = jnp.zeros((bsz, B_HEADS, HEAD_DIM, HEAD_DIM), F32)
                        buf = jnp.zeros((bsz, B_CONV - 1, 3 * b_dim), F32)
                    else:
                        kv_past, s0, buf = (past[0], past[1]), past[2][j], past[3][j]
                    mixed, (k_rows, v_rows, s_new, buf_new) = _ab_mixer(
                        h2d, bsz, t, t_valid, rope_tabs, kv_past, s0, buf, ab_in2d, j, w_zgb, ab_out2d,
                        p["b_conv_w"][j], p["b_a_log"][j], p["b_dt_bias"][j], p["b_norm"][j], tm)
                    a_k.append(k_rows)
                    a_v.append(v_rows)
                    b_s.append(s_new)
                    b_conv.append(buf_new)
                else:
                    s0 = (jnp.zeros((bsz, C_HEADS, HEAD_DIM, HEAD_DIM), F32) if past is None else past[4][j])
                    mixed, s_new = _hgrn_mixer(h2d, bsz, t, t_valid, s0, c_in2d, j, c_out2d, lb_all[j],
                                               p["c_norm"][j], tm)
                    c_s.append(s_new)
                w_out2d = ab_out2d if layer % 2 == 0 else c_out2d
                x, h = _mm_norm(mixed, w_out2d, j, x, mods5, boff,
                                post=(norm_post[layer, 1], layer, 5, 1.0),
                                pre=(norm_pre[layer, 2], layer, 6, 7))
                continue
            i_ffn = 0 if sub == 0 else 1
            act = _swiglu_in(h2d, wi2d, layer * 2 + i_ffn, 2 * tm)
            y = _mm([act], wo2d, k_block=layer * 2 + i_ffn, n0=0, n=d, tn=512, tm=512)
            post = (y.reshape(bsz, t, d), norm_post[layer, sub], layer, 3 * sub + 2, FFN_RESIDUAL)
            if sub == 0:
                pre = (norm_pre[layer, 1], layer, 3, 4)
            elif layer + 1 < depth:
                pre = (norm_pre[layer + 1, 0], layer + 1, 0, 1)
            else:
                pre = None
            outs = _norm_call(x, mods5, boff, post=post, pre=pre)
            x = outs[0]
            h = outs[1] if pre is not None else None
    return x, jnp.stack(a_k), jnp.stack(a_v), jnp.stack(b_s), jnp.stack(b_conv), jnp.stack(c_s)


def kernel(x_prompt, x_sample, cache_a_k, cache_a_v, state_b_s, state_b_conv, state_c_s, c_prompt, c_sample,
           ada_w, ada_b, norm_pre, norm_post, ffn_wi, ffn_wo, ab_w_in, ab_w_out, b_conv_w, b_a_log, b_dt_bias,
           b_norm, c_w_in, c_w_out, c_lower_bounds, c_norm):
    p = dict(norm_pre=norm_pre, norm_post=norm_post, ffn_wi=ffn_wi, ffn_wo=ffn_wo, ab_w_in=ab_w_in,
             ab_w_out=ab_w_out, b_conv_w=b_conv_w, b_a_log=b_a_log, b_dt_bias=b_dt_bias, b_norm=b_norm,
             c_w_in=c_w_in, c_w_out=c_w_out, c_lower_bounds=c_lower_bounds, c_norm=c_norm)
    depth, d = norm_pre.shape[0], x_prompt.shape[2]
    main_cols = (3 * A_HEADS + 3 * B_HEADS) * HEAD_DIM
    gb_cols = 2 * B_HEADS
    p["w_zgb"] = [jnp.concatenate([ab_w_in[j, :, main_cols + gb_cols:], ab_w_in[j, :, main_cols:main_cols + gb_cols],
                                   jnp.zeros((d, LANES - gb_cols), F32)], axis=1)
                  for j in range(ab_w_in.shape[0])]
    n_p, t_p = x_prompt.shape[0], x_prompt.shape[1]
    n_s, t_s = x_sample.shape[0], x_sample.shape[1]
    past_len = 16384

    rows = -(-(n_p + n_s) // (2 * SUBLANES)) * (2 * SUBLANES)
    c_all = jnp.concatenate([c_prompt, c_sample, jnp.zeros((rows - n_p - n_s, d), F32)], axis=0)
    mods5 = _ada_mods(c_all, ada_w, ada_b).reshape(depth, rows, N_MOD, 1, d)

    pos_p = jnp.arange(t_p, dtype=jnp.int32)
    y_p, ak_p, av_p, bs_p, bc_p, cs_p = _trunk(x_prompt, t_p, pos_p, mods5, 0, None, p)

    x_s = jnp.pad(x_sample, ((0, 0), (0, SAMPLE_T_PAD - t_s), (0, 0)))
    pos_s = past_len + jnp.arange(SAMPLE_T_PAD, dtype=jnp.int32)
    past = (cache_a_k, cache_a_v, state_b_s, state_b_conv, state_c_s)
    y_s, ak_s, av_s, bs_s, bc_s, cs_s = _trunk(x_s, t_s, pos_s, mods5, n_p, past, p)
    return (y_p, y_s[:, :t_s], ak_p, av_p, bs_p, bc_p, cs_p, ak_s, av_s, bs_s, bc_s, cs_s)
```

```python
import functools
import math

import jax
import jax.numpy as jnp
from jax import lax
from jax.experimental import pallas as pl
from jax.experimental.pallas import tpu as pltpu

F32 = jnp.float32
BF16 = jnp.bfloat16
HIGHEST = lax.Precision.HIGHEST

LANES = 128
SUBLANES = 8
VMEM_LIMIT_BYTES = 56 * 1024 * 1024

_LOG2_E = 1.4426950408889634
NORM_EPS = 1e-6
FFN_RESIDUAL = 0.5
N_MOD = 9
ROPE_THETA = 10000.0
A_HEADS = 8
A_GROUPS = ((128, 1), (512, 4), (2048, 16))
A_BLOCK = 128
B_HEADS = 8
B_CONV = 4
B_CHUNK = 64
C_HEADS = 16
HEAD_DIM = 128
SAMPLE_T_PAD = 16
C_CHUNK = 64
_MM_NORM_MIN_ROWS = 512
_HGRN_HEAD_GROUP = 16


def _cparams(*sem):
    return pltpu.CompilerParams(dimension_semantics=sem, vmem_limit_bytes=VMEM_LIMIT_BYTES)


def _sigmoid(x):
    return 1.0 / (1.0 + jnp.exp(-x))


def _silu(x):
    return x * _sigmoid(x)


def _softplus(x):
    return jnp.maximum(x, 0.0) + jnp.log1p(jnp.exp(-jnp.abs(x)))


def _rms(x, g):
    return x * lax.rsqrt(jnp.mean(x * x, axis=-1, keepdims=True) + NORM_EPS) * g


def _dot(a, b):
    return jnp.dot(a.astype(BF16), b.astype(BF16), preferred_element_type=F32)


def _dot_nt(a, b):
    return lax.dot_general(a.astype(BF16), b.astype(BF16), (((1,), (1,)), ((), ())),
                           preferred_element_type=F32)


def _dot_tn(a, b):
    return lax.dot_general(a.astype(BF16), b.astype(BF16), (((0,), (0,)), ((), ())),
                           preferred_element_type=F32)


def _split2(x):
    hi = x.astype(BF16)
    lo = (x - hi.astype(F32)).astype(BF16)
    return hi, lo


def _dot_x3(a_parts, b_parts):
    (ah, al), (bh, bl) = a_parts, b_parts
    dot = functools.partial(jnp.dot, preferred_element_type=F32)
    return dot(ah, bh) + (dot(ah, bl) + dot(al, bh))


def _dot_ones(ones_bf16, x, ones_on_right):
    hi = x.astype(BF16)
    r1 = x - hi.astype(F32)
    mid = r1.astype(BF16)
    lo = (r1 - mid.astype(F32)).astype(BF16)
    dot = functools.partial(jnp.dot, preferred_element_type=F32)
    if ones_on_right:
        return dot(hi, ones_bf16) + (dot(mid, ones_bf16) + dot(lo, ones_bf16))
    return dot(ones_bf16, hi) + (dot(ones_bf16, mid) + dot(ones_bf16, lo))


def _roll_in_blocks(x, lag):
    rows, cols = x.shape
    x3 = x.reshape(rows // SUBLANES, SUBLANES, cols)
    return pltpu.roll(x3, lag, 1).reshape(rows, cols)


def _iota(shape, dim):
    return lax.broadcasted_iota(jnp.int32, shape, dim)


def _ada_kernel(c_ref, w_ref, b_ref, o_ref):
    a = _silu(c_ref[...])
    o_ref[...] = _dot(a, w_ref[...]) + b_ref[...]


def _ada_mods(c_all, ada_w, ada_b):
    depth, d, n = ada_w.shape
    rows = c_all.shape[0]
    tn = 1024
    return pl.pallas_call(
        _ada_kernel,
        grid=(depth, n // tn),
        in_specs=[pl.BlockSpec((rows, d), lambda l, j: (0, 0)),
                  pl.BlockSpec((None, d, tn), lambda l, j: (l, 0, j)),
                  pl.BlockSpec((None, 1, tn), lambda l, j: (l, 0, j))],
        out_specs=pl.BlockSpec((None, rows, tn), lambda l, j: (l, 0, j)),
        out_shape=jax.ShapeDtypeStruct((depth, rows, n), F32),
        compiler_params=_cparams("parallel", "parallel"),
        name="ada_mods",
    )(c_all, ada_w, ada_b.reshape(depth, 1, n))


def _norm_kernel(*refs, has_post, has_pre, coef):
    it = iter(refs)
    x_ref = next(it)
    if has_post:
        y_ref, gpost_ref, gate_ref = next(it), next(it), next(it)
    if has_pre:
        gpre_ref, shift_ref, scale_ref = next(it), next(it), next(it)
    x = x_ref[...]
    if has_post:
        xo_ref = next(it)
        x = x + (coef * gate_ref[...]) * _rms(y_ref[...], gpost_ref[...])
        xo_ref[...] = x
    if has_pre:
        h_ref = next(it)
        h = _rms(x, gpre_ref[...]) * (1.0 + scale_ref[...]) + shift_ref[...]
        h_ref[...] = h.astype(BF16)


def _norm_call(x, mods5, boff, post=None, pre=None):
    bsz, t, d = x.shape
    tt = min(t, 512)
    row = pl.BlockSpec((None, tt, d), lambda b, i: (b, i, 0))
    vec = pl.BlockSpec((1, d), lambda b, i: (0, 0))

    def mod_spec(layer, k):
        return pl.BlockSpec((None, None, None, 1, d), lambda b, i: (layer, boff + b, k, 0, 0))

    args, in_specs, out_shape, out_specs = [x], [row], [], []
    coef = 1.0
    if post is not None:
        y, g_post, layer, gate_idx, coef = post
        args += [y, g_post, mods5]
        in_specs += [row, vec, mod_spec(layer, gate_idx)]
        out_shape.append(jax.ShapeDtypeStruct(x.shape, F32))
        out_specs.append(row)
    if pre is not None:
        g_pre, layer, shift_idx, scale_idx = pre
        args += [g_pre, mods5, mods5]
        in_specs += [vec, mod_spec(layer, shift_idx), mod_spec(layer, scale_idx)]
        out_shape.append(jax.ShapeDtypeStruct(x.shape, BF16))
        out_specs.append(row)
    outs = pl.pallas_call(
        functools.partial(_norm_kernel, has_post=post is not None, has_pre=pre is not None, coef=coef),
        grid=(bsz, t // tt),
        in_specs=in_specs, out_specs=out_specs, out_shape=out_shape,
        compiler_params=_cparams("parallel", "parallel"),
        name="sandwich_norm",
    )(*args)
    return outs


def _mm_kernel(*refs, k_sizes):
    n_x = len(k_sizes)
    x_refs, w_ref, o_ref, wbf_ref = refs[:n_x], refs[n_x], refs[n_x + 1], refs[n_x + 2]

    @pl.when(pl.program_id(1) == 0)
    def _():
        wbf_ref[...] = w_ref[...].astype(BF16)

    acc, off = None, 0
    for x_ref, ks in zip(x_refs, k_sizes):
        part = jnp.dot(x_ref[...].astype(BF16), wbf_ref[off:off + ks, :], preferred_element_type=F32)
        acc = part if acc is None else acc + part
        off += ks
    o_ref[...] = acc.astype(o_ref.dtype)


def _mm(xs, w2d, *, k_block, n0, n, tn, tm, out_dtype=F32):
    m = xs[0].shape[0]
    k_sizes = tuple(x.shape[1] for x in xs)
    k = sum(k_sizes)
    tm = min(tm, m)
    assert m % tm == 0 and n % tn == 0 and n0 % tn == 0 and w2d.shape[0] % k == 0
    nb0 = n0 // tn
    in_specs = [pl.BlockSpec((tm, ks), lambda j, i: (i, 0)) for ks in k_sizes]
    in_specs.append(pl.BlockSpec((k, tn), lambda j, i: (k_block, nb0 + j)))
    return pl.pallas_call(
        functools.partial(_mm_kernel, k_sizes=k_sizes),
        grid=(n // tn, m // tm),
        in_specs=in_specs,
        out_specs=pl.BlockSpec((tm, tn), lambda j, i: (i, j)),
        out_shape=jax.ShapeDtypeStruct((m, n), out_dtype),
        scratch_shapes=[pltpu.VMEM((k, tn), BF16)],
        compiler_params=_cparams("parallel", "arbitrary"),
        name="matmul",
    )(*xs, w2d)


def _mm_norm_kernel(*refs, k_sizes, coef):
    n_x = len(k_sizes)
    x_refs = refs[:n_x]
    (w_ref, xres_ref, gpost_ref, gate_ref, gpre_ref, shift_ref, scale_ref,
     xo_ref, h_ref, wbf_ref) = refs[n_x:]

    @pl.when(pl.program_id(0) == 0)
    def _():
        wbf_ref[...] = w_ref[...].astype(BF16)

    rows = xres_ref.shape[0]
    n_split = 4 if rows % (4 * 2 * SUBLANES) == 0 else 1
    step = rows // n_split
    for r in range(n_split):
        rs = slice(r * step, (r + 1) * step)
        y, off = None, 0
        for x_ref, ks in zip(x_refs, k_sizes):
            part = jnp.dot(x_ref[rs, :].astype(BF16), wbf_ref[off:off + ks, :], preferred_element_type=F32)
            y = part if y is None else y + part
            off += ks
        x = xres_ref[rs, :] + (coef * gate_ref[...]) * _rms(y, gpost_ref[...])
        xo_ref[rs, :] = x
        h_ref[rs, :] = (_rms(x, gpre_ref[...]) * (1.0 + scale_ref[...]) + shift_ref[...]).astype(BF16)


def _mm_norm(xs, w2d, k_block, x_res, mods5, boff, post, pre):
    bsz, t, d = x_res.shape
    m = bsz * t
    k_sizes = tuple(x.shape[1] for x in xs)
    k = sum(k_sizes)
    tm = min(t, 512)
    assert t % tm == 0 and w2d.shape == (w2d.shape[0] // k * k, d)
    g_post, layer_post, gate_idx, coef = post
    g_pre, layer_pre, shift_idx, scale_idx = pre
    row = pl.BlockSpec((tm, d), lambda i: (i, 0))
    vec = pl.BlockSpec((1, d), lambda i: (0, 0))

    def mod_spec(layer, idx):
        return pl.BlockSpec((None, None, None, 1, d), lambda i: (layer, boff + (i * tm) // t, idx, 0, 0))

    in_specs = [pl.BlockSpec((tm, ks), lambda i: (i, 0)) for ks in k_sizes]
    in_specs += [pl.BlockSpec((k, d), lambda i: (k_block, 0), pipeline_mode=pl.Buffered(1)),
                 row, vec, mod_spec(layer_post, gate_idx), vec, mod_spec(layer_pre, shift_idx),
                 mod_spec(layer_pre, scale_idx)]
    x_new, h = pl.pallas_call(
        functools.partial(_mm_norm_kernel, k_sizes=k_sizes, coef=coef),
        grid=(m // tm,),
        in_specs=in_specs,
        out_specs=[row, row],
        out_shape=[jax.ShapeDtypeStruct((m, d), F32), jax.ShapeDtypeStruct((m, d), BF16)],
        scratch_shapes=[pltpu.VMEM((k, d), BF16)],
        compiler_params=_cparams("arbitrary"),
        name="matmul_norm",
    )(*xs, w2d, x_res.reshape(m, d), g_post, mods5, g_pre, mods5, mods5)
    return x_new.reshape(bsz, t, d), h.reshape(bsz, t, d)


_SWIGLU_GROUP = 4


def _swiglu_kernel(*refs):
    ng = _SWIGLU_GROUP
    x_ref, w_refs, o_ref, wbf_ref = refs[0], refs[1:1 + 2 * ng], refs[1 + 2 * ng], refs[2 + 2 * ng]

    @pl.when(pl.program_id(1) == 0)
    def _():
        for g in range(ng):
            wbf_ref[:, (2 * g) * LANES:(2 * g + 1) * LANES] = w_refs[g][...].astype(BF16)
            wbf_ref[:, (2 * g + 1) * LANES:(2 * g + 2) * LANES] = w_refs[ng + g][...].astype(BF16)

    x = x_ref[...]
    for g in range(ng):
        r = jnp.dot(x, wbf_ref[:, 2 * g * LANES:(2 * g + 2) * LANES], preferred_element_type=F32)
        u, v = r[:, :LANES], r[:, LANES:]
        o_ref[:, g * LANES:(g + 1) * LANES] = (_silu(v) * u).astype(o_ref.dtype)


def _swiglu_in(h, wi2d, k_block, tm):
    m, k = h.shape
    d_ff = wi2d.shape[1] // 2
    assert d_ff % LANES == 0
    nblk = d_ff // LANES
    ng = _SWIGLU_GROUP
    tn = ng * LANES
    tm = min(tm, m)
    last = nblk - 1
    w_specs = [pl.BlockSpec((k, LANES), lambda j, i, g=g, base=base: (k_block, base + jnp.minimum(ng * j + g, last)))
               for base in (0, nblk) for g in range(ng)]
    return pl.pallas_call(
        _swiglu_kernel,
        grid=(pl.cdiv(nblk, ng), m // tm),
        in_specs=[pl.BlockSpec((tm, k), lambda j, i: (i, 0))] + w_specs,
        out_specs=pl.BlockSpec((tm, tn), lambda j, i: (i, j)),
        out_shape=jax.ShapeDtypeStruct((m, d_ff), BF16),
        scratch_shapes=[pltpu.VMEM((k, 2 * tn), BF16)],
        compiler_params=_cparams("parallel", "arbitrary"),
        name="swiglu_in",
    )(h, *([wi2d] * (2 * ng)))


def _rope_kernel(x_ref, cos_ref, sin_ref, q_ref, k_ref):
    cos, sin = cos_ref[...], sin_ref[...]
    for h in range(2 * A_HEADS):
        xh = x_ref[:, h * HEAD_DIM:(h + 1) * HEAD_DIM]
        r = xh * cos + pltpu.roll(xh, HEAD_DIM // 2, 1) * sin
        dst = q_ref if h < A_HEADS else k_ref
        hh = h % A_HEADS
        dst[:, hh * HEAD_DIM:(hh + 1) * HEAD_DIM] = r


def _rope(proj, cos_t, sin_t):
    bsz, t, _ = proj.shape
    a_dim = A_HEADS * HEAD_DIM
    tt = min(t, 256)
    out = jax.ShapeDtypeStruct((bsz, t, a_dim), F32)
    o_spec = pl.BlockSpec((None, tt, a_dim), lambda b, i: (b, i, 0))
    return pl.pallas_call(
        _rope_kernel,
        grid=(bsz, t // tt),
        in_specs=[pl.BlockSpec((None, tt, 2 * a_dim), lambda b, i: (b, i, 0)),
                  pl.BlockSpec((tt, HEAD_DIM), lambda b, i: (i, 0)),
                  pl.BlockSpec((tt, HEAD_DIM), lambda b, i: (i, 0))],
        out_specs=[o_spec, o_spec], out_shape=[out, out],
        compiler_params=_cparams("parallel", "parallel"),
        name="rope",
    )(proj, cos_t, sin_t)


def _rope_tables(pos):
    half = HEAD_DIM // 2
    inv_freq = jnp.power(ROPE_THETA, -jnp.arange(half, dtype=F32) / half)
    ang = pos.astype(F32)[:, None] * inv_freq[None, :]
    cos, sin = jnp.cos(ang), jnp.sin(ang)
    return jnp.concatenate([cos, cos], axis=1), jnp.concatenate([-sin, sin], axis=1)


_A_TOKEN_BLOCK = A_BLOCK * max(dil for _, dil in A_GROUPS)


def _dilated_attn_kernel(q_ref, kp_ref, kc_ref, vp_ref, vc_ref, o_ref, m_ref, l_ref, acc_ref):
    tb = pl.program_id(2)
    blk = A_BLOCK
    tokens = _A_TOKEN_BLOCK
    qi = _iota((blk, 2 * blk), 0)
    ki = _iota((blk, 2 * blk), 1)
    rel = qi + blk - ki
    has_prev = (ki >= blk) | (tb > 0)
    scale = HEAD_DIM ** -0.5

    for g, (window, dil) in enumerate(A_GROUPS):
        span = window // dil
        band = (rel >= 0) & (rel <= span)
        reach = blk * dil

        def rows(ref, base, r, dil=dil):
            if dil == 1:
                return ref[base:base + blk, :]
            return ref[pl.ds(base + r, blk, stride=dil), :]

        def put(ref, base, r, val, dil=dil):
            if dil == 1:
                ref[base:base + blk, :] = val
            else:
                ref[pl.ds(base + r, blk, stride=dil), :] = val

        for r in range(dil):
            k_prev = rows(kp_ref, tokens - reach, r).astype(BF16)
            v_prev = rows(vp_ref, tokens - reach, r).astype(BF16)
            for s in range(tokens // reach):
                base = s * reach
                q = rows(q_ref, base, r)
                k_cur = rows(kc_ref, base, r).astype(BF16)
                v_cur = rows(vc_ref, base, r).astype(BF16)
                kcat = jnp.concatenate([k_prev, k_cur], axis=0)
                vcat = jnp.concatenate([v_prev, v_cur], axis=0)
                k_prev, v_prev = k_cur, v_cur
                valid = band & has_prev if s == 0 else band
                sc = jnp.where(valid, _dot_nt(q, kcat) * scale, -jnp.inf)
                m_loc = jnp.max(sc, axis=-1, keepdims=True)
                if g == 0:
                    p = jnp.exp(sc - m_loc)
                    put(m_ref, base, r, jnp.broadcast_to(m_loc, (blk, HEAD_DIM)))
                    put(l_ref, base, r, jnp.broadcast_to(jnp.sum(p, axis=-1, keepdims=True), (blk, HEAD_DIM)))
                    put(acc_ref, base, r, _dot(p, vcat))
                else:
                    m_old = rows(m_ref, base, r)
                    m_new = jnp.maximum(m_old, m_loc)
                    alpha = jnp.exp(m_old - m_new)
                    p = jnp.exp(sc - m_new[:, 0:1])
                    put(m_ref, base, r, m_new)
                    put(l_ref, base, r, alpha * rows(l_ref, base, r) + jnp.sum(p, axis=-1, keepdims=True))
                    put(acc_ref, base, r, alpha * rows(acc_ref, base, r) + _dot(p, vcat))

    o_ref[...] = (acc_ref[...] / l_ref[...]).astype(o_ref.dtype)


def _dilated_attention(q_r, k_r, proj):
    bsz, t, a_dim = q_r.shape
    tokens = _A_TOKEN_BLOCK
    assert t % tokens == 0 and all(window // dil <= A_BLOCK for window, dil in A_GROUPS)
    blk = (None, tokens, HEAD_DIM)
    v_off = 2 * a_dim // HEAD_DIM
    cur = lambda b, h, i: (b, i, h)
    prev = lambda b, h, i: (b, jnp.maximum(i - 1, 0), h)
    v_cur = lambda b, h, i: (b, i, v_off + h)
    v_prev = lambda b, h, i: (b, jnp.maximum(i - 1, 0), v_off + h)
    return pl.pallas_call(
        _dilated_attn_kernel,
        grid=(bsz, A_HEADS, t // tokens),
        in_specs=[pl.BlockSpec(blk, cur), pl.BlockSpec(blk, prev), pl.BlockSpec(blk, cur),
                  pl.BlockSpec(blk, v_prev), pl.BlockSpec(blk, v_cur)],
        out_specs=pl.BlockSpec(blk, cur),
        out_shape=jax.ShapeDtypeStruct((bsz, t, a_dim), BF16),
        scratch_shapes=[pltpu.VMEM((tokens, HEAD_DIM), F32)] * 3,
        compiler_params=_cparams("parallel", "parallel", "parallel"),
        name="dilated_attention",
    )(q_r, k_r, k_r, proj, proj)


def _group_count(d):
    cnt = jnp.zeros(d.shape, F32)
    for window, dil in A_GROUPS:
        hit = (d >= 0) & (d % dil == 0) & (d <= window)
        cnt = cnt + jnp.where(hit, 1.0, 0.0)
    return cnt


def _cache_attn_kernel(q_ref, kn_ref, vn_ref, kc_ref, vc_ref, o_ref, *, n_buf):
    tq = SUBLANES
    nh = A_HEADS
    scale = HEAD_DIM ** -0.5
    heads_of = lambda ref: jnp.concatenate([ref[0:tq, h * HEAD_DIM:(h + 1) * HEAD_DIM] for h in range(nh)], axis=0)
    q, k_new, v_new = heads_of(q_ref), heads_of(kn_ref), heads_of(vn_ref)
    k_c = kc_ref[...].reshape(n_buf * nh, HEAD_DIM)
    v_c = vc_ref[...].reshape(n_buf * nh, HEAD_DIM)

    r_c = _iota((nh * tq, n_buf * nh), 0)
    c_c = _iota((nh * tq, n_buf * nh), 1)
    same_c = (c_c % nh) == (r_c // tq)
    cnt_c = jnp.where(same_c, _group_count(n_buf + (r_c % tq) - (c_c // nh)), 0.0)
    r_n = _iota((nh * tq, nh * tq), 0)
    c_n = _iota((nh * tq, nh * tq), 1)
    same_n = (c_n // tq) == (r_n // tq)
    cnt_n = jnp.where(same_n, _group_count((r_n % tq) - (c_n % tq)), 0.0)

    s_c = jnp.where(cnt_c > 0, _dot_nt(q, k_c) * scale, -jnp.inf)
    s_n = jnp.where(cnt_n > 0, _dot_nt(q, k_new) * scale, -jnp.inf)
    m = jnp.maximum(jnp.max(s_c, axis=-1, keepdims=True), jnp.max(s_n, axis=-1, keepdims=True))
    p_c = cnt_c * jnp.exp(s_c - m)
    p_n = cnt_n * jnp.exp(s_n - m)
    den = jnp.sum(p_c, axis=-1, keepdims=True) + jnp.sum(p_n, axis=-1, keepdims=True)
    o = (_dot(p_c, v_c) + _dot(p_n, v_new)) / den
    o_rows = jnp.concatenate([o[h * tq:(h + 1) * tq, :] for h in range(nh)], axis=1)
    pad = jnp.zeros((o_ref.shape[0] - tq, nh * HEAD_DIM), F32)
    o_ref[...] = jnp.concatenate([o_rows, pad], axis=0).astype(o_ref.dtype)


def _cache_attention(q_r, k_r, proj, cache_k, cache_v, layer, t_valid):
    bsz, tp, a_dim = q_r.shape
    n_buf = cache_k.shape[2]
    assert t_valid <= SUBLANES <= tp
    new = pl.BlockSpec((None, tp, a_dim), lambda b: (b, 0, 0))
    v_new = pl.BlockSpec((None, tp, a_dim), lambda b: (b, 0, 2))
    cache = pl.BlockSpec((None, None, n_buf, A_HEADS, HEAD_DIM), lambda b: (layer, b, 0, 0, 0))
    return pl.pallas_call(
        functools.partial(_cache_attn_kernel, n_buf=n_buf),
        grid=(bsz,),
        in_specs=[new, new, v_new, cache, cache],
        out_specs=new,
        out_shape=jax.ShapeDtypeStruct((bsz, tp, a_dim), BF16),
        compiler_params=_cparams("parallel"),
        name="cache_attention",
    )(q_r, k_r, proj, cache_k, cache_v)


def _gdn_kernel(raw_ref, prev_ref, buf_ref, cw_ref, gb_ref, gbt_ref, alr_ref, dtr_ref, alc_ref, dtc_ref,
                s0_ref, z_ref, nw_ref, o_ref, sfin_ref, halo_ref, s_ref, *, t_valid):
    c = pl.program_id(1)
    ch = B_CHUNK
    hd = HEAD_DIM
    b_dim = B_HEADS * hd

    halo_ref[0:SUBLANES, :] = jnp.where(c == 0, buf_ref[...], prev_ref[...])
    halo_ref[SUBLANES:SUBLANES + ch, :] = raw_ref[...]
    y = raw_ref[...] * cw_ref[B_CONV - 1:B_CONV, :]
    for j in range(B_CONV - 1):
        lag = B_CONV - 1 - j
        y = y + halo_ref[SUBLANES - lag:SUBLANES - lag + ch, :] * cw_ref[j:j + 1, :]
    act = _silu(y)

    row_ok = (c * ch + _iota((ch, 1), 0)) < t_valid
    col_ok = (c * ch + _iota((1, ch), 1)) < t_valid

    gb = gb_ref[...]
    g_col = jnp.where(row_ok, -jnp.exp(alr_ref[...]) * _softplus(gb + dtr_ref[...]), 0.0)
    beta_col = jnp.where(row_ok, _sigmoid(gb), 0.0)
    gbt = gbt_ref[...]
    g_row = jnp.where(col_ok, -jnp.exp(alc_ref[:, :ch]) * _softplus(gbt + dtc_ref[:, :ch]), 0.0)

    ri = _iota((ch, ch), 0)
    ci = _iota((ch, ch), 1)
    tri = ri >= ci
    strict = ri > ci
    eye = jnp.where(ri == ci, 1.0, 0.0)
    cum_col = _dot_ones(jnp.where(tri, 1.0, 0.0).astype(BF16), g_col, False)
    cum_row = _dot_ones(jnp.where(ri <= ci, 1.0, 0.0).astype(BF16), g_row, True)

    heads = range(B_HEADS)
    qs, ks, vs, ccs, betas, decays, lows = [], [], [], [], [], [], []
    for h in heads:
        q = act[:, h * hd:(h + 1) * hd]
        k = act[:, b_dim + h * hd:b_dim + (h + 1) * hd]
        v = act[:, 2 * b_dim + h * hd:2 * b_dim + (h + 1) * hd]
        q = q * lax.rsqrt(jnp.sum(q * q, axis=-1, keepdims=True) + 1e-6) * (hd ** -0.5)
        k = k * lax.rsqrt(jnp.sum(k * k, axis=-1, keepdims=True) + 1e-6)
        qs.append(jnp.where(row_ok, q, 0.0))
        ks.append(jnp.where(row_ok, k, 0.0))
        vs.append(jnp.where(row_ok, v, 0.0))
        cc = cum_col[:, h:h + 1]
        cr = cum_row[h:h + 1, :]
        ccs.append(cc)
        betas.append(beta_col[:, B_HEADS + h:B_HEADS + h + 1])
        decays.append(jnp.where(tri, jnp.exp(jnp.where(tri, cc - cr, 0.0)), 0.0))
    qk_kk = [_dot_nt(jnp.concatenate([qs[h], ks[h]], axis=0), ks[h]) for h in heads]
    lows = [jnp.where(strict, betas[h] * qk_kk[h][ch:, :] * decays[h], 0.0) for h in heads]
    invs = [eye - lows[h] for h in heads]
    pw_parts = [_split2(lows[h]) for h in heads]
    pws = [_dot_x3(pw_parts[h], pw_parts[h]) for h in heads]
    size = 2
    while size < ch:
        pw_parts = [_split2(pws[h]) for h in heads]
        invs = [invs[h] + _dot_x3(_split2(invs[h]), pw_parts[h]) for h in heads]
        size *= 2
        if size < ch:
            pws = [_dot_x3(pw_parts[h], pw_parts[h]) for h in heads]
    rhs = [jnp.concatenate([vs[h] * betas[h], ks[h] * (betas[h] * jnp.exp(ccs[h]))], axis=1) for h in heads]
    uws = [_dot_x3(_split2(invs[h]), _split2(rhs[h])) for h in heads]

    @pl.when(c == 0)
    def _():
        s_ref[...] = s0_ref[...]

    c_lasts = [ccs[h][ch - 1:ch, :] for h in heads]
    ss = [s_ref[h].astype(BF16) for h in heads]
    ws_qs = [_dot(jnp.concatenate([uws[h][:, hd:], qs[h] * jnp.exp(ccs[h])], axis=0), ss[h]) for h in heads]
    v_news = [uws[h][:, :hd] - ws_qs[h][:ch, :] for h in heads]
    os_ = [ws_qs[h][ch:, :] + _dot(qk_kk[h][:ch, :] * decays[h], v_news[h]) for h in heads]
    upd = [_dot_tn(ks[h] * jnp.exp(c_lasts[h] - ccs[h]), v_news[h]) for h in heads]
    for h in heads:
        sl = slice(h * hd, (h + 1) * hd)
        s_ref[h] = s_ref[h] * jnp.exp(c_lasts[h]) + upd[h]
        o_ref[:, sl] = (_rms(os_[h], nw_ref[...]) * _silu(z_ref[:, sl])).astype(o_ref.dtype)

    @pl.when(c == pl.num_programs(1) - 1)
    def _():
        sfin_ref[...] = s_ref[...]


def _gdn(proj, buf8, conv_w, zgb, gbt, a_log, dt_bias, s0, norm_w, t_valid):
    bsz, t, _ = proj.shape
    c3 = conv_w.shape[1]
    ch = B_CHUNK
    nc = t // ch
    b_dim = B_HEADS * HEAD_DIM
    gb_blk = b_dim // LANES
    pad = jnp.zeros((LANES - B_HEADS,), F32)
    al_row = jnp.concatenate([a_log.astype(F32), pad]).reshape(1, LANES)
    dt_row = jnp.concatenate([dt_bias.astype(F32), pad]).reshape(1, LANES)
    pad_c = jnp.zeros((2 * SUBLANES - B_HEADS,), F32)
    al_col = jnp.broadcast_to(jnp.concatenate([a_log.astype(F32), pad_c])[:, None], (2 * SUBLANES, LANES))
    dt_col = jnp.broadcast_to(jnp.concatenate([dt_bias.astype(F32), pad_c])[:, None], (2 * SUBLANES, LANES))
    full = lambda shape: pl.BlockSpec(shape, lambda b, c: (0,) * len(shape))
    row_spec = pl.BlockSpec((None, ch, b_dim), lambda b, c: (b, c, 0))
    st_spec = pl.BlockSpec((None, B_HEADS, HEAD_DIM, HEAD_DIM), lambda b, c: (b, 0, 0, 0))
    return pl.pallas_call(
        functools.partial(_gdn_kernel, t_valid=t_valid),
        grid=(bsz, nc),
        in_specs=[pl.BlockSpec((None, ch, c3), lambda b, c: (b, c, 1)),
                  pl.BlockSpec((None, SUBLANES, c3), lambda b, c: (b, jnp.maximum(c * (ch // SUBLANES) - 1, 0), 1)),
                  pl.BlockSpec((None, SUBLANES, c3), lambda b, c: (b, 0, 0)),
                  full((B_CONV, c3)),
                  pl.BlockSpec((None, ch, LANES), lambda b, c: (b, c, gb_blk)),
                  pl.BlockSpec((None, None, 2 * SUBLANES, ch), lambda b, c: (b, c, 0, 0)),
                  full((1, LANES)), full((1, LANES)),
                  full((2 * SUBLANES, LANES)), full((2 * SUBLANES, LANES)),
                  st_spec, row_spec, full((1, HEAD_DIM))],
        out_specs=[row_spec, st_spec],
        out_shape=[jax.ShapeDtypeStruct((bsz, t, b_dim), BF16),
                   jax.ShapeDtypeStruct((bsz, B_HEADS, HEAD_DIM, HEAD_DIM), F32)],
        scratch_shapes=[pltpu.VMEM((SUBLANES + ch, c3), F32),
                        pltpu.VMEM((B_HEADS, HEAD_DIM, HEAD_DIM), F32)],
        compiler_params=_cparams("parallel", "arbitrary"),
        name="gdn",
    )(proj, proj, buf8, conv_w, zgb, gbt, al_row, dt_row, al_col, dt_col, s0, zgb, norm_w)


def _hgrn_kernel(q_ref, f_ref, i_ref, g_ref, llb_ref, l1m_ref, oml_ref, s0_ref, nw_ref,
                 o_ref, sfin_ref, st_ref, *, t_valid):
    c = pl.program_id(1)
    ch = C_CHUNK
    hd = HEAD_DIM

    @pl.when(c == 0)
    def _():
        for h in range(C_HEADS):
            st_ref[h] = s0_ref[h].T

    ri = _iota((ch, ch), 0)
    ci = _iota((ch, ch), 1)
    sels, level_masks = [jnp.where(ri >= ci, 1.0, 0.0)], []
    half = SUBLANES
    while half < ch:
        base = (ri // (2 * half)) * (2 * half)
        ref_row = base + half - 1
        later = (ri - base) >= half
        lo_col = jnp.where(later, ref_row, ri)
        hi_col = jnp.where(later, ri, ref_row)
        sels.append(jnp.where(ci > lo_col, jnp.where(ci <= hi_col, 1.0, 0.0), 0.0))
        cbase = (ci // (2 * half)) * (2 * half)
        level_masks.append(later & (cbase == base) & ((ci - cbase) < half))
        half *= 2
    sel_all = jnp.concatenate(sels, axis=0).astype(BF16)
    lag_masks = [(ci == ri - lag) & ((ri % SUBLANES) >= lag) for lag in range(SUBLANES)]
    ones = jnp.ones((hd, hd), BF16)
    row_ok = None if t_valid is None else (c * ch + _iota((ch, 1), 0)) < t_valid

    def gates(h):
        sl = slice(h * hd, (h + 1) * hd)
        f = f_ref[:, sl]
        log_sig = jnp.minimum(f, 0.0) - jnp.log(1.0 + jnp.exp(-jnp.abs(f)))
        a = llb_ref[:, sl]
        b = l1m_ref[:, sl] + log_sig
        log2_f = (jnp.maximum(a, b) + jnp.log(1.0 + jnp.exp(-jnp.abs(a - b)))) * _LOG2_E
        k = oml_ref[:, sl] * _sigmoid(-f)
        q = q_ref[:, sl] * (hd ** -0.5)
        v = i_ref[:, sl]
        if row_ok is not None:
            log2_f = jnp.where(row_ok, log2_f, 0.0)
            k = jnp.where(row_ok, k, 0.0)
            q = jnp.where(row_ok, q, 0.0)
            v = jnp.where(row_ok, v, 0.0)
        return log2_f, q, k, v

    def level_att(sums, q, k):
        att = jnp.zeros((ch, ch), F32)
        for i, mask in enumerate(level_masks):
            e = jnp.exp2(sums[(i + 1) * ch:(i + 2) * ch, :])
            att = att + jnp.where(mask, _dot_nt(q * e, k * e), 0.0)
        return att

    def lag_stack(sums, q, k):
        cum = sums[:ch, :]
        terms = [q * k]
        for lag in range(1, SUBLANES):
            terms.append(q * _roll_in_blocks(k, lag) * jnp.exp2(cum - _roll_in_blocks(cum, lag)))
        return jnp.concatenate(terms, axis=0).astype(BF16)

    def add_lags(att, row_sums):
        for lag in range(SUBLANES):
            att = att + jnp.where(lag_masks[lag], row_sums[lag * ch:(lag + 1) * ch, :ch], 0.0)
        return att

    for h0 in range(0, C_HEADS, _HGRN_HEAD_GROUP):
        heads = range(h0, h0 + _HGRN_HEAD_GROUP)
        gs = [gates(h) for h in heads]
        sums = [_dot_ones(sel_all, g[0], False) for g in gs]
        atts = [level_att(s, g[1], g[2]) for s, g in zip(sums, gs)]
        row_sums = [jnp.dot(lag_stack(s, g[1], g[2]), ones, preferred_element_type=F32) for s, g in zip(sums, gs)]
        atts = [add_lags(a, r) for a, r in zip(atts, row_sums)]
        for h, s, (_, q, k, v), att in zip(heads, sums, gs, atts):
            cum = s[:ch, :]
            c_last = cum[ch - 1:ch, :]
            st = st_ref[h]
            o = _dot_nt(q * jnp.exp2(cum), st) + _dot(att, v)
            st_ref[h] = st * jnp.exp2(c_last) + _dot_tn(v, k * jnp.exp2(c_last - cum))
            sl = slice(h * hd, (h + 1) * hd)
            o_ref[:, sl] = (_rms(o, nw_ref[...]) * _silu(g_ref[:, sl])).astype(o_ref.dtype)

    @pl.when(c == pl.num_programs(1) - 1)
    def _():
        for h in range(C_HEADS):
            sfin_ref[h] = st_ref[h].T


def _hgrn(proj, lb, s0, norm_w, t_valid):
    bsz, t, four_w = proj.shape
    width = four_w // 4
    ch = C_CHUNK
    nc = t // ch
    lb = lb.astype(F32).reshape(1, width)
    col = lambda j: pl.BlockSpec((None, ch, width), lambda b, c: (b, c, j))
    vec = pl.BlockSpec((1, width), lambda b, c: (0, 0))
    st_spec = pl.BlockSpec((None, C_HEADS, HEAD_DIM, HEAD_DIM), lambda b, c: (b, 0, 0, 0))
    return pl.pallas_call(
        functools.partial(_hgrn_kernel, t_valid=None if t_valid == t else t_valid),
        grid=(bsz, nc),
        in_specs=[col(0), col(1), col(2), col(3), vec, vec, vec, st_spec,
                  pl.BlockSpec((1, HEAD_DIM), lambda b, c: (0, 0))],
        out_specs=[pl.BlockSpec((None, ch, width), lambda b, c: (b, c, 0)), st_spec],
        out_shape=[jax.ShapeDtypeStruct((bsz, t, width), BF16),
                   jax.ShapeDtypeStruct((bsz, C_HEADS, HEAD_DIM, HEAD_DIM), F32)],
        scratch_shapes=[pltpu.VMEM((C_HEADS, HEAD_DIM, HEAD_DIM), F32)],
        compiler_params=_cparams("parallel", "arbitrary"),
        name="hgrn2",
    )(proj, proj, proj, proj, jnp.log(lb), jnp.log1p(-lb), 1.0 - lb, s0, norm_w)


def _pad_time(x, multiple):
    t = x.shape[1]
    t_pad = -(-t // multiple) * multiple
    return x if t_pad == t else jnp.pad(x, ((0, 0), (0, t_pad - t), (0, 0)))


def _ab_mixer(h2d, bsz, t, t_valid, rope_tabs, kv_past, s0, conv_buf, w_in2d, j, w_zgb, w_out2d,
              conv_w, a_log, dt_bias, norm_w, tm):
    a_dim = A_HEADS * HEAD_DIM
    b_dim = B_HEADS * HEAD_DIM
    d = h2d.shape[1]
    main_cols = 3 * a_dim + 3 * b_dim
    proj = _mm([h2d], w_in2d, k_block=j, n0=0, n=main_cols, tn=1024, tm=tm).reshape(bsz, t, main_cols)
    zgb = _mm([h2d], w_zgb, k_block=0, n0=0, n=w_zgb.shape[1], tn=w_zgb.shape[1], tm=tm)
    zgb = zgb.reshape(bsz, t, w_zgb.shape[1])

    q_r, k_r = _rope(proj, *rope_tabs)
    if kv_past is None:
        o_a = _dilated_attention(q_r, k_r, proj)
        keep = min(A_GROUPS[-1][0], t_valid)
        k_rows = k_r[:, t_valid - keep:t_valid]
        v_rows = proj[:, t_valid - keep:t_valid, 2 * a_dim:3 * a_dim]
    else:
        o_a = _cache_attention(q_r, k_r, proj, kv_past[0], kv_past[1], j, t_valid)
        k_rows = k_r[:, :t_valid]
        v_rows = proj[:, :t_valid, 2 * a_dim:3 * a_dim]
    k_rows = k_rows.reshape(bsz, -1, A_HEADS, HEAD_DIM)
    v_rows = v_rows.reshape(bsz, -1, A_HEADS, HEAD_DIM)

    raw_tail = proj[:, max(t_valid - (B_CONV - 1), 0):t_valid, 3 * a_dim:]
    buf_new = jnp.concatenate([conv_buf, raw_tail], axis=1)[:, -(B_CONV - 1):]
    buf8 = jnp.pad(conv_buf, ((0, 0), (SUBLANES - (B_CONV - 1), 0), (0, 0)))
    proj_c, zgb_c = _pad_time(proj, B_CHUNK), _pad_time(zgb, B_CHUNK)
    nc = proj_c.shape[1] // B_CHUNK
    gbt = zgb_c[:, :, b_dim:b_dim + 2 * SUBLANES].reshape(bsz, nc, B_CHUNK, 2 * SUBLANES).swapaxes(2, 3)
    o_b, s_new = _gdn(proj_c, buf8, conv_w, zgb_c, gbt, a_log, dt_bias, s0, norm_w.reshape(1, HEAD_DIM), t_valid)

    mixed = [o_a.reshape(bsz * t, a_dim), o_b[:, :t].reshape(bsz * t, b_dim)]
    return mixed, (k_rows, v_rows, s_new, buf_new)


def _hgrn_mixer(h2d, bsz, t, t_valid, s0, w_in2d, j, w_out2d, lb, norm_w, tm):
    d = h2d.shape[1]
    n_proj = w_in2d.shape[1]
    proj = _mm([h2d], w_in2d, k_block=j, n0=0, n=n_proj, tn=1024, tm=tm).reshape(bsz, t, n_proj)
    o, s_new = _hgrn(_pad_time(proj, C_CHUNK), lb, s0, norm_w.reshape(1, HEAD_DIM), t_valid)
    return [o[:, :t].reshape(bsz * t, n_proj // 4)], s_new


def _trunk(x, t_valid, pos, mods5, boff, past, p):
    bsz, t, d = x.shape
    depth = p["norm_pre"].shape[0]
    tm = 1024
    rope_tabs = _rope_tables(pos)
    a_dim = A_HEADS * HEAD_DIM
    b_dim = B_HEADS * HEAD_DIM
    d_ff = p["ffn_wo"].shape[2]
    wi2d = p["ffn_wi"].reshape(-1, 2 * d_ff)
    wo2d = p["ffn_wo"].reshape(-1, d)
    ab_in2d = p["ab_w_in"].reshape(-1, p["ab_w_in"].shape[2])
    ab_out2d = p["ab_w_out"].reshape(-1, d)
    c_in2d = p["c_w_in"].reshape(-1, p["c_w_in"].shape[2])
    c_out2d = p["c_w_out"].reshape(-1, d)
    norm_pre = p["norm_pre"].reshape(depth, 3, 1, d)
    norm_post = p["norm_post"].reshape(depth, 3, 1, d)
    lb_all = jnp.cumsum(jax.nn.softmax(p["c_lower_bounds"].astype(F32), axis=0), axis=0)
    lb_all = lb_all - lb_all[0:1]

    a_k, a_v, b_s, b_conv, c_s = [], [], [], [], []
    (h,) = _norm_call(x, mods5, boff, pre=(norm_pre[0, 0], 0, 0, 1))
    for layer in range(depth):
        j = layer // 2
        for sub in range(3):
            h2d = h.reshape(bsz * t, d)
            if sub == 1:
                if layer % 2 == 0:
                    w_zgb = p["w_zgb"][j]
                    if past is None:
                        kv_past = None
                        s0 = jnp.zeros((bsz, B_HEADS, HEAD_DIM, HEAD_DIM), F32)
                        buf = jnp.zeros((bsz, B_CONV - 1, 3 * b_dim), F32)
                    else:
                        kv_past, s0, buf = (past[0], past[1]), past[2][j], past[3][j]
                    mixed, (k_rows, v_rows, s_new, buf_new) = _ab_mixer(
                        h2d, bsz, t, t_valid, rope_tabs, kv_past, s0, buf, ab_in2d, j, w_zgb, ab_out2d,
                        p["b_conv_w"][j], p["b_a_log"][j], p["b_dt_bias"][j], p["b_norm"][j], tm)
                    a_k.append(k_rows)
                    a_v.append(v_rows)
                    b_s.append(s_new)
                    b_conv.append(buf_new)
                else:
                    s0 = (jnp.zeros((bsz, C_HEADS, HEAD_DIM, HEAD_DIM), F32) if past is None else past[4][j])
                    mixed, s_new = _hgrn_mixer(h2d, bsz, t, t_valid, s0, c_in2d, j, c_out2d, lb_all[j],
                                               p["c_norm"][j], tm)
                    c_s.append(s_new)
                w_out2d = ab_out2d if layer % 2 == 0 else c_out2d
                post = (norm_post[layer, 1], layer, 5, 1.0)
                pre = (norm_pre[layer, 2], layer, 6, 7)
                if t >= _MM_NORM_MIN_ROWS:
                    x, h = _mm_norm(mixed, w_out2d, j, x, mods5, boff, post=post, pre=pre)
                else:
                    y = _mm(mixed, w_out2d, k_block=j, n0=0, n=d, tn=1024, tm=tm)
                    x, h = _norm_call(x, mods5, boff, post=(y.reshape(bsz, t, d),) + post, pre=pre)
                continue
            i_ffn = 0 if sub == 0 else 1
            act = _swiglu_in(h2d, wi2d, layer * 2 + i_ffn, 2 * tm)
            y = _mm([act], wo2d, k_block=layer * 2 + i_ffn, n0=0, n=d, tn=512, tm=512)
            post = (y.reshape(bsz, t, d), norm_post[layer, sub], layer, 3 * sub + 2, FFN_RESIDUAL)
            if sub == 0:
                pre = (norm_pre[layer, 1], layer, 3, 4)
            elif layer + 1 < depth:
                pre = (norm_pre[layer + 1, 0], layer + 1, 0, 1)
            else:
                pre = None
            outs = _norm_call(x, mods5, boff, post=post, pre=pre)
            x = outs[0]
            h = outs[1] if pre is not None else None
    return x, jnp.stack(a_k), jnp.stack(a_v), jnp.stack(b_s), jnp.stack(b_conv), jnp.stack(c_s)


def kernel(x_prompt, x_sample, cache_a_k, cache_a_v, state_b_s, state_b_conv, state_c_s, c_prompt, c_sample,
           ada_w, ada_b, norm_pre, norm_post, ffn_wi, ffn_wo, ab_w_in, ab_w_out, b_conv_w, b_a_log, b_dt_bias,
           b_norm, c_w_in, c_w_out, c_lower_bounds, c_norm):
    p = dict(norm_pre=norm_pre, norm_post=norm_post, ffn_wi=ffn_wi, ffn_wo=ffn_wo, ab_w_in=ab_w_in,
             ab_w_out=ab_w_out, b_conv_w=b_conv_w, b_a_log=b_a_log, b_dt_bias=b_dt_bias, b_norm=b_norm,
             c_w_in=c_w_in, c_w_out=c_w_out, c_lower_bounds=c_lower_bounds, c_norm=c_norm)
    depth, d = norm_pre.shape[0], x_prompt.shape[2]
    main_cols = (3 * A_HEADS + 3 * B_HEADS) * HEAD_DIM
    gb_cols = 2 * B_HEADS
    p["w_zgb"] = [jnp.concatenate([ab_w_in[j, :, main_cols + gb_cols:], ab_w_in[j, :, main_cols:main_cols + gb_cols],
                                   jnp.zeros((d, LANES - gb_cols), F32)], axis=1)
                  for j in range(ab_w_in.shape[0])]
    n_p, t_p = x_prompt.shape[0], x_prompt.shape[1]
    n_s, t_s = x_sample.shape[0], x_sample.shape[1]
    past_len = 16384

    rows = -(-(n_p + n_s) // (2 * SUBLANES)) * (2 * SUBLANES)
    c_all = jnp.concatenate([c_prompt, c_sample, jnp.zeros((rows - n_p - n_s, d), F32)], axis=0)
    mods5 = _ada_mods(c_all, ada_w, ada_b).reshape(depth, rows, N_MOD, 1, d)

    pos_p = jnp.arange(t_p, dtype=jnp.int32)
    y_p, ak_p, av_p, bs_p, bc_p, cs_p = _trunk(x_prompt, t_p, pos_p, mods5, 0, None, p)

    x_s = jnp.pad(x_sample, ((0, 0), (0, SAMPLE_T_PAD - t_s), (0, 0)))
    pos_s = past_len + jnp.arange(SAMPLE_T_PAD, dtype=jnp.int32)
    past = (cache_a_k, cache_a_v, state_b_s, state_b_conv, state_c_s)
    y_s, ak_s, av_s, bs_s, bc_s, cs_s = _trunk(x_s, t_s, pos_s, mods5, n_p, past, p)
    return (y_p, y_s[:, :t_s], ak_p, av_p, bs_p, bc_p, cs_p, ak_s, av_s, bs_s, bc_s, cs_s)
```

```python
import functools
import math

import jax
import jax.numpy as jnp
from jax import lax
from jax.experimental import pallas as pl
from jax.experimental.pallas import tpu as pltpu

F32 = jnp.float32
BF16 = jnp.bfloat16
HIGHEST = lax.Precision.HIGHEST

LANES = 128
SUBLANES = 8
VMEM_LIMIT_BYTES = 56 * 1024 * 1024

_LOG2_E = 1.4426950408889634
NORM_EPS = 1e-6
FFN_RESIDUAL = 0.5
N_MOD = 9
ROPE_THETA = 10000.0
A_HEADS = 8
A_GROUPS = ((128, 1), (512, 4), (2048, 16))
A_BLOCK = 128
B_HEADS = 8
B_CONV = 4
B_CHUNK = 64
C_HEADS = 16
HEAD_DIM = 128
SAMPLE_T_PAD = 16
C_CHUNK = 64
_MM_NORM_MIN_ROWS = 512
_HGRN_HEAD_GROUP = 16


def _cparams(*sem):
    return pltpu.CompilerParams(dimension_semantics=sem, vmem_limit_bytes=VMEM_LIMIT_BYTES)


def _sigmoid(x):
    return 1.0 / (1.0 + jnp.exp(-x))


def _silu(x):
    return x * _sigmoid(x)


def _softplus(x):
    return jnp.maximum(x, 0.0) + jnp.log1p(jnp.exp(-jnp.abs(x)))


def _rms(x, g):
    return x * lax.rsqrt(jnp.mean(x * x, axis=-1, keepdims=True) + NORM_EPS) * g


def _dot(a, b):
    return jnp.dot(a.astype(BF16), b.astype(BF16), preferred_element_type=F32)


def _dot_nt(a, b):
    return lax.dot_general(a.astype(BF16), b.astype(BF16), (((1,), (1,)), ((), ())),
                           preferred_element_type=F32)


def _dot_tn(a, b):
    return lax.dot_general(a.astype(BF16), b.astype(BF16), (((0,), (0,)), ((), ())),
                           preferred_element_type=F32)


def _split2(x):
    hi = x.astype(BF16)
    lo = (x - hi.astype(F32)).astype(BF16)
    return hi, lo


def _dot_x3(a_parts, b_parts):
    (ah, al), (bh, bl) = a_parts, b_parts
    dot = functools.partial(jnp.dot, preferred_element_type=F32)
    return dot(ah, bh) + (dot(ah, bl) + dot(al, bh))


def _dot_ones(ones_bf16, x, ones_on_right):
    hi = x.astype(BF16)
    r1 = x - hi.astype(F32)
    mid = r1.astype(BF16)
    lo = (r1 - mid.astype(F32)).astype(BF16)
    dot = functools.partial(jnp.dot, preferred_element_type=F32)
    if ones_on_right:
        return dot(hi, ones_bf16) + (dot(mid, ones_bf16) + dot(lo, ones_bf16))
    return dot(ones_bf16, hi) + (dot(ones_bf16, mid) + dot(ones_bf16, lo))


def _roll_in_blocks(x, lag):
    rows, cols = x.shape
    x3 = x.reshape(rows // SUBLANES, SUBLANES, cols)
    return pltpu.roll(x3, lag, 1).reshape(rows, cols)


def _iota(shape, dim):
    return lax.broadcasted_iota(jnp.int32, shape, dim)


def _ada_kernel(c_ref, w_ref, b_ref, o_ref):
    a = _silu(c_ref[...])
    o_ref[...] = _dot(a, w_ref[...]) + b_ref[...]


def _ada_mods(c_all, ada_w, ada_b):
    depth, d, n = ada_w.shape
    rows = c_all.shape[0]
    tn = 1024
    return pl.pallas_call(
        _ada_kernel,
        grid=(depth, n // tn),
        in_specs=[pl.BlockSpec((rows, d), lambda l, j: (0, 0)),
                  pl.BlockSpec((None, d, tn), lambda l, j: (l, 0, j)),
                  pl.BlockSpec((None, 1, tn), lambda l, j: (l, 0, j))],
        out_specs=pl.BlockSpec((None, rows, tn), lambda l, j: (l, 0, j)),
        out_shape=jax.ShapeDtypeStruct((depth, rows, n), F32),
        compiler_params=_cparams("parallel", "parallel"),
        name="ada_mods",
    )(c_all, ada_w, ada_b.reshape(depth, 1, n))


def _norm_kernel(*refs, has_post, has_pre, coef):
    it = iter(refs)
    x_ref = next(it)
    if has_post:
        y_ref, gpost_ref, gate_ref = next(it), next(it), next(it)
    if has_pre:
        gpre_ref, shift_ref, scale_ref = next(it), next(it), next(it)
    x = x_ref[...]
    if has_post:
        xo_ref = next(it)
        x = x + (coef * gate_ref[...]) * _rms(y_ref[...], gpost_ref[...])
        xo_ref[...] = x
    if has_pre:
        h_ref = next(it)
        h = _rms(x, gpre_ref[...]) * (1.0 + scale_ref[...]) + shift_ref[...]
        h_ref[...] = h.astype(BF16)


def _norm_call(x, mods5, boff, post=None, pre=None):
    bsz, t, d = x.shape
    tt = min(t, 512)
    row = pl.BlockSpec((None, tt, d), lambda b, i: (b, i, 0))
    vec = pl.BlockSpec((1, d), lambda b, i: (0, 0))

    def mod_spec(layer, k):
        return pl.BlockSpec((None, None, None, 1, d), lambda b, i: (layer, boff + b, k, 0, 0))

    args, in_specs, out_shape, out_specs = [x], [row], [], []
    coef = 1.0
    if post is not None:
        y, g_post, layer, gate_idx, coef = post
        args += [y, g_post, mods5]
        in_specs += [row, vec, mod_spec(layer, gate_idx)]
        out_shape.append(jax.ShapeDtypeStruct(x.shape, F32))
        out_specs.append(row)
    if pre is not None:
        g_pre, layer, shift_idx, scale_idx = pre
        args += [g_pre, mods5, mods5]
        in_specs += [vec, mod_spec(layer, shift_idx), mod_spec(layer, scale_idx)]
        out_shape.append(jax.ShapeDtypeStruct(x.shape, BF16))
        out_specs.append(row)
    outs = pl.pallas_call(
        functools.partial(_norm_kernel, has_post=post is not None, has_pre=pre is not None, coef=coef),
        grid=(bsz, t // tt),
        in_specs=in_specs, out_specs=out_specs, out_shape=out_shape,
        compiler_params=_cparams("parallel", "parallel"),
        name="sandwich_norm",
    )(*args)
    return outs


def _mm_kernel(*refs, k_sizes, has_rider):
    n_x = len(k_sizes)
    x_refs = refs[:n_x]
    r_refs = refs[n_x:2 * n_x] if has_rider else ()
    rest = refs[n_x + len(r_refs):]
    w_ref, o_ref = rest[0], rest[1]
    ro_ref = rest[2] if has_rider else None
    wbf_ref = rest[-1]

    def product(parts):
        acc, off = None, 0
        for x_ref, ks in zip(parts, k_sizes):
            part = jnp.dot(x_ref[...].astype(BF16), wbf_ref[off:off + ks, :], preferred_element_type=F32)
            acc = part if acc is None else acc + part
            off += ks
        return acc

    @pl.when(pl.program_id(1) == 0)
    def _():
        wbf_ref[...] = w_ref[...].astype(BF16)
        if has_rider:
            ro_ref[...] = product(r_refs).astype(ro_ref.dtype)

    o_ref[...] = product(x_refs).astype(o_ref.dtype)


def _mm(xs, w2d, *, k_block, n0, n, tn, tm, out_dtype=F32, rider=None):
    m = xs[0].shape[0]
    k_sizes = tuple(x.shape[1] for x in xs)
    k = sum(k_sizes)
    tm = min(tm, m)
    assert m % tm == 0 and n % tn == 0 and n0 % tn == 0 and w2d.shape[0] % k == 0
    nb0 = n0 // tn
    in_specs = [pl.BlockSpec((tm, ks), lambda j, i: (i, 0)) for ks in k_sizes]
    out_specs = [pl.BlockSpec((tm, tn), lambda j, i: (i, j))]
    out_shape = [jax.ShapeDtypeStruct((m, n), out_dtype)]
    args = list(xs)
    if rider is not None:
        m2 = rider[0].shape[0]
        assert tuple(x.shape[1] for x in rider) == k_sizes
        in_specs += [pl.BlockSpec((m2, ks), lambda j, i: (0, 0)) for ks in k_sizes]
        out_specs.append(pl.BlockSpec((m2, tn), lambda j, i: (0, j)))
        out_shape.append(jax.ShapeDtypeStruct((m2, n), out_dtype))
        args += list(rider)
    in_specs.append(pl.BlockSpec((k, tn), lambda j, i: (k_block, nb0 + j)))
    outs = pl.pallas_call(
        functools.partial(_mm_kernel, k_sizes=k_sizes, has_rider=rider is not None),
        grid=(n // tn, m // tm),
        in_specs=in_specs,
        out_specs=out_specs,
        out_shape=out_shape,
        scratch_shapes=[pltpu.VMEM((k, tn), BF16)],
        compiler_params=_cparams("parallel", "arbitrary"),
        name="matmul",
    )(*args, w2d)
    return outs[0] if rider is None else (outs[0], outs[1])


def _mm_norm_kernel(*refs, k_sizes, coef):
    n_x = len(k_sizes)
    x_refs = refs[:n_x]
    (w_ref, xres_ref, gpost_ref, gate_ref, gpre_ref, shift_ref, scale_ref,
     xo_ref, h_ref, wbf_ref) = refs[n_x:]

    @pl.when(pl.program_id(0) == 0)
    def _():
        wbf_ref[...] = w_ref[...].astype(BF16)

    rows = xres_ref.shape[0]
    n_split = 4 if rows % (4 * 2 * SUBLANES) == 0 else 1
    step = rows // n_split
    for r in range(n_split):
        rs = slice(r * step, (r + 1) * step)
        y, off = None, 0
        for x_ref, ks in zip(x_refs, k_sizes):
            part = jnp.dot(x_ref[rs, :].astype(BF16), wbf_ref[off:off + ks, :], preferred_element_type=F32)
            y = part if y is None else y + part
            off += ks
        x = xres_ref[rs, :] + (coef * gate_ref[...]) * _rms(y, gpost_ref[...])
        xo_ref[rs, :] = x
        h_ref[rs, :] = (_rms(x, gpre_ref[...]) * (1.0 + scale_ref[...]) + shift_ref[...]).astype(BF16)


def _mm_norm(xs, w2d, k_block, x_res, mods5, boff, post, pre):
    bsz, t, d = x_res.shape
    m = bsz * t
    k_sizes = tuple(x.shape[1] for x in xs)
    k = sum(k_sizes)
    tm = min(t, 512)
    assert t % tm == 0 and w2d.shape == (w2d.shape[0] // k * k, d)
    g_post, layer_post, gate_idx, coef = post
    g_pre, layer_pre, shift_idx, scale_idx = pre
    row = pl.BlockSpec((tm, d), lambda i: (i, 0))
    vec = pl.BlockSpec((1, d), lambda i: (0, 0))

    def mod_spec(layer, idx):
        return pl.BlockSpec((None, None, None, 1, d), lambda i: (layer, boff + (i * tm) // t, idx, 0, 0))

    in_specs = [pl.BlockSpec((tm, ks), lambda i: (i, 0)) for ks in k_sizes]
    in_specs += [pl.BlockSpec((k, d), lambda i: (k_block, 0), pipeline_mode=pl.Buffered(1)),
                 row, vec, mod_spec(layer_post, gate_idx), vec, mod_spec(layer_pre, shift_idx),
                 mod_spec(layer_pre, scale_idx)]
    x_new, h = pl.pallas_call(
        functools.partial(_mm_norm_kernel, k_sizes=k_sizes, coef=coef),
        grid=(m // tm,),
        in_specs=in_specs,
        out_specs=[row, row],
        out_shape=[jax.ShapeDtypeStruct((m, d), F32), jax.ShapeDtypeStruct((m, d), BF16)],
        scratch_shapes=[pltpu.VMEM((k, d), BF16)],
        compiler_params=_cparams("arbitrary"),
        name="matmul_norm",
    )(*xs, w2d, x_res.reshape(m, d), g_post, mods5, g_pre, mods5, mods5)
    return x_new.reshape(bsz, t, d), h.reshape(bsz, t, d)


_SWIGLU_GROUP = 4


def _swiglu_kernel(*refs, has_rider):
    ng = _SWIGLU_GROUP
    n_x = 2 if has_rider else 1
    x_ref, w_refs = refs[0], refs[n_x:n_x + 2 * ng]
    o_ref, wbf_ref = refs[n_x + 2 * ng], refs[-1]

    def gated(x, dst_ref):
        for g in range(ng):
            r = jnp.dot(x, wbf_ref[:, 2 * g * LANES:(2 * g + 2) * LANES], preferred_element_type=F32)
            u, v = r[:, :LANES], r[:, LANES:]
            dst_ref[:, g * LANES:(g + 1) * LANES] = (_silu(v) * u).astype(dst_ref.dtype)

    @pl.when(pl.program_id(1) == 0)
    def _():
        for g in range(ng):
            wbf_ref[:, (2 * g) * LANES:(2 * g + 1) * LANES] = w_refs[g][...].astype(BF16)
            wbf_ref[:, (2 * g + 1) * LANES:(2 * g + 2) * LANES] = w_refs[ng + g][...].astype(BF16)
        if has_rider:
            gated(refs[1][...], refs[n_x + 2 * ng + 1])

    gated(x_ref[...], o_ref)


def _swiglu_in(h, wi2d, k_block, tm, rider=None):
    m, k = h.shape
    d_ff = wi2d.shape[1] // 2
    assert d_ff % LANES == 0
    nblk = d_ff // LANES
    ng = _SWIGLU_GROUP
    tn = ng * LANES
    tm = min(tm, m)
    last = nblk - 1
    w_specs = [pl.BlockSpec((k, LANES), lambda j, i, g=g, base=base: (k_block, base + jnp.minimum(ng * j + g, last)))
               for base in (0, nblk) for g in range(ng)]
    x_specs = [pl.BlockSpec((tm, k), lambda j, i: (i, 0))]
    out_specs = [pl.BlockSpec((tm, tn), lambda j, i: (i, j))]
    out_shape = [jax.ShapeDtypeStruct((m, d_ff), BF16)]
    args = [h]
    if rider is not None:
        m2 = rider.shape[0]
        x_specs.append(pl.BlockSpec((m2, k), lambda j, i: (0, 0)))
        out_specs.append(pl.BlockSpec((m2, tn), lambda j, i: (0, j)))
        out_shape.append(jax.ShapeDtypeStruct((m2, d_ff), BF16))
        args.append(rider)
    outs = pl.pallas_call(
        functools.partial(_swiglu_kernel, has_rider=rider is not None),
        grid=(pl.cdiv(nblk, ng), m // tm),
        in_specs=x_specs + w_specs,
        out_specs=out_specs,
        out_shape=out_shape,
        scratch_shapes=[pltpu.VMEM((k, 2 * tn), BF16)],
        compiler_params=_cparams("parallel", "arbitrary"),
        name="swiglu_in",
    )(*args, *([wi2d] * (2 * ng)))
    return outs[0] if rider is None else (outs[0], outs[1])


def _rope_kernel(x_ref, cos_ref, sin_ref, q_ref, k_ref):
    cos, sin = cos_ref[...], sin_ref[...]
    for h in range(2 * A_HEADS):
        xh = x_ref[:, h * HEAD_DIM:(h + 1) * HEAD_DIM]
        r = xh * cos + pltpu.roll(xh, HEAD_DIM // 2, 1) * sin
        dst = q_ref if h < A_HEADS else k_ref
        hh = h % A_HEADS
        dst[:, hh * HEAD_DIM:(hh + 1) * HEAD_DIM] = r


def _rope(proj, cos_t, sin_t):
    bsz, t, _ = proj.shape
    a_dim = A_HEADS * HEAD_DIM
    tt = min(t, 256)
    out = jax.ShapeDtypeStruct((bsz, t, a_dim), F32)
    o_spec = pl.BlockSpec((None, tt, a_dim), lambda b, i: (b, i, 0))
    return pl.pallas_call(
        _rope_kernel,
        grid=(bsz, t // tt),
        in_specs=[pl.BlockSpec((None, tt, 2 * a_dim), lambda b, i: (b, i, 0)),
                  pl.BlockSpec((tt, HEAD_DIM), lambda b, i: (i, 0)),
                  pl.BlockSpec((tt, HEAD_DIM), lambda b, i: (i, 0))],
        out_specs=[o_spec, o_spec], out_shape=[out, out],
        compiler_params=_cparams("parallel", "parallel"),
        name="rope",
    )(proj, cos_t, sin_t)


def _rope_tables(pos):
    half = HEAD_DIM // 2
    inv_freq = jnp.power(ROPE_THETA, -jnp.arange(half, dtype=F32) / half)
    ang = pos.astype(F32)[:, None] * inv_freq[None, :]
    cos, sin = jnp.cos(ang), jnp.sin(ang)
    return jnp.concatenate([cos, cos], axis=1), jnp.concatenate([-sin, sin], axis=1)


_A_TOKEN_BLOCK = A_BLOCK * max(dil for _, dil in A_GROUPS)


def _dilated_attn_kernel(q_ref, kp_ref, kc_ref, vp_ref, vc_ref, o_ref, m_ref, l_ref, acc_ref):
    tb = pl.program_id(2)
    blk = A_BLOCK
    tokens = _A_TOKEN_BLOCK
    qi = _iota((blk, 2 * blk), 0)
    ki = _iota((blk, 2 * blk), 1)
    rel = qi + blk - ki
    has_prev = (ki >= blk) | (tb > 0)
    scale = HEAD_DIM ** -0.5

    for g, (window, dil) in enumerate(A_GROUPS):
        span = window // dil
        band = (rel >= 0) & (rel <= span)
        reach = blk * dil

        def rows(ref, base, r, dil=dil):
            if dil == 1:
                return ref[base:base + blk, :]
            return ref[pl.ds(base + r, blk, stride=dil), :]

        def put(ref, base, r, val, dil=dil):
            if dil == 1:
                ref[base:base + blk, :] = val
            else:
                ref[pl.ds(base + r, blk, stride=dil), :] = val

        for r in range(dil):
            k_prev = rows(kp_ref, tokens - reach, r).astype(BF16)
            v_prev = rows(vp_ref, tokens - reach, r).astype(BF16)
            for s in range(tokens // reach):
                base = s * reach
                q = rows(q_ref, base, r)
                k_cur = rows(kc_ref, base, r).astype(BF16)
                v_cur = rows(vc_ref, base, r).astype(BF16)
                kcat = jnp.concatenate([k_prev, k_cur], axis=0)
                vcat = jnp.concatenate([v_prev, v_cur], axis=0)
                k_prev, v_prev = k_cur, v_cur
                valid = band & has_prev if s == 0 else band
                sc = jnp.where(valid, _dot_nt(q, kcat) * scale, -jnp.inf)
                m_loc = jnp.max(sc, axis=-1, keepdims=True)
                if g == 0:
                    p = jnp.exp(sc - m_loc)
                    put(m_ref, base, r, jnp.broadcast_to(m_loc, (blk, HEAD_DIM)))
                    put(l_ref, base, r, jnp.broadcast_to(jnp.sum(p, axis=-1, keepdims=True), (blk, HEAD_DIM)))
                    put(acc_ref, base, r, _dot(p, vcat))
                else:
                    m_old = rows(m_ref, base, r)
                    m_new = jnp.maximum(m_old, m_loc)
                    alpha = jnp.exp(m_old - m_new)
                    p = jnp.exp(sc - m_new[:, 0:1])
                    put(m_ref, base, r, m_new)
                    put(l_ref, base, r, alpha * rows(l_ref, base, r) + jnp.sum(p, axis=-1, keepdims=True))
                    put(acc_ref, base, r, alpha * rows(acc_ref, base, r) + _dot(p, vcat))

    o_ref[...] = (acc_ref[...] / l_ref[...]).astype(o_ref.dtype)


def _dilated_attention(q_r, k_r, proj):
    bsz, t, a_dim = q_r.shape
    tokens = _A_TOKEN_BLOCK
    assert t % tokens == 0 and all(window // dil <= A_BLOCK for window, dil in A_GROUPS)
    blk = (None, tokens, HEAD_DIM)
    v_off = 2 * a_dim // HEAD_DIM
    cur = lambda b, h, i: (b, i, h)
    prev = lambda b, h, i: (b, jnp.maximum(i - 1, 0), h)
    v_cur = lambda b, h, i: (b, i, v_off + h)
    v_prev = lambda b, h, i: (b, jnp.maximum(i - 1, 0), v_off + h)
    return pl.pallas_call(
        _dilated_attn_kernel,
        grid=(bsz, A_HEADS, t // tokens),
        in_specs=[pl.BlockSpec(blk, cur), pl.BlockSpec(blk, prev), pl.BlockSpec(blk, cur),
                  pl.BlockSpec(blk, v_prev), pl.BlockSpec(blk, v_cur)],
        out_specs=pl.BlockSpec(blk, cur),
        out_shape=jax.ShapeDtypeStruct((bsz, t, a_dim), BF16),
        scratch_shapes=[pltpu.VMEM((tokens, HEAD_DIM), F32)] * 3,
        compiler_params=_cparams("parallel", "parallel", "parallel"),
        name="dilated_attention",
    )(q_r, k_r, k_r, proj, proj)


def _group_count(d):
    cnt = jnp.zeros(d.shape, F32)
    for window, dil in A_GROUPS:
        hit = (d >= 0) & (d % dil == 0) & (d <= window)
        cnt = cnt + jnp.where(hit, 1.0, 0.0)
    return cnt


def _cache_attn_kernel(q_ref, kn_ref, vn_ref, kc_ref, vc_ref, o_ref, *, n_buf):
    tq = SUBLANES
    nh = A_HEADS
    scale = HEAD_DIM ** -0.5
    heads_of = lambda ref: jnp.concatenate([ref[0:tq, h * HEAD_DIM:(h + 1) * HEAD_DIM] for h in range(nh)], axis=0)
    q, k_new, v_new = heads_of(q_ref), heads_of(kn_ref), heads_of(vn_ref)
    k_c = kc_ref[...].reshape(n_buf * nh, HEAD_DIM)
    v_c = vc_ref[...].reshape(n_buf * nh, HEAD_DIM)

    r_c = _iota((nh * tq, n_buf * nh), 0)
    c_c = _iota((nh * tq, n_buf * nh), 1)
    same_c = (c_c % nh) == (r_c // tq)
    cnt_c = jnp.where(same_c, _group_count(n_buf + (r_c % tq) - (c_c // nh)), 0.0)
    r_n = _iota((nh * tq, nh * tq), 0)
    c_n = _iota((nh * tq, nh * tq), 1)
    same_n = (c_n // tq) == (r_n // tq)
    cnt_n = jnp.where(same_n, _group_count((r_n % tq) - (c_n % tq)), 0.0)

    s_c = jnp.where(cnt_c > 0, _dot_nt(q, k_c) * scale, -jnp.inf)
    s_n = jnp.where(cnt_n > 0, _dot_nt(q, k_new) * scale, -jnp.inf)
    m = jnp.maximum(jnp.max(s_c, axis=-1, keepdims=True), jnp.max(s_n, axis=-1, keepdims=True))
    p_c = cnt_c * jnp.exp(s_c - m)
    p_n = cnt_n * jnp.exp(s_n - m)
    den = jnp.sum(p_c, axis=-1, keepdims=True) + jnp.sum(p_n, axis=-1, keepdims=True)
    o = (_dot(p_c, v_c) + _dot(p_n, v_new)) / den
    o_rows = jnp.concatenate([o[h * tq:(h + 1) * tq, :] for h in range(nh)], axis=1)
    pad = jnp.zeros((o_ref.shape[0] - tq, nh * HEAD_DIM), F32)
    o_ref[...] = jnp.concatenate([o_rows, pad], axis=0).astype(o_ref.dtype)


def _cache_attention(q_r, k_r, proj, cache_k, cache_v, layer, t_valid):
    bsz, tp, a_dim = q_r.shape
    n_buf = cache_k.shape[2]
    assert t_valid <= SUBLANES <= tp
    new = pl.BlockSpec((None, tp, a_dim), lambda b: (b, 0, 0))
    v_new = pl.BlockSpec((None, tp, a_dim), lambda b: (b, 0, 2))
    cache = pl.BlockSpec((None, None, n_buf, A_HEADS, HEAD_DIM), lambda b: (layer, b, 0, 0, 0))
    return pl.pallas_call(
        functools.partial(_cache_attn_kernel, n_buf=n_buf),
        grid=(bsz,),
        in_specs=[new, new, v_new, cache, cache],
        out_specs=new,
        out_shape=jax.ShapeDtypeStruct((bsz, tp, a_dim), BF16),
        compiler_params=_cparams("parallel"),
        name="cache_attention",
    )(q_r, k_r, proj, cache_k, cache_v)


def _gdn_kernel(raw_ref, prev_ref, buf_ref, cw_ref, gb_ref, gbt_ref, alr_ref, dtr_ref, alc_ref, dtc_ref,
                s0_ref, z_ref, nw_ref, o_ref, sfin_ref, halo_ref, s_ref, *, t_valid):
    c = pl.program_id(1)
    ch = B_CHUNK
    hd = HEAD_DIM
    b_dim = B_HEADS * hd

    halo_ref[0:SUBLANES, :] = jnp.where(c == 0, buf_ref[...], prev_ref[...])
    halo_ref[SUBLANES:SUBLANES + ch, :] = raw_ref[...]
    y = raw_ref[...] * cw_ref[B_CONV - 1:B_CONV, :]
    for j in range(B_CONV - 1):
        lag = B_CONV - 1 - j
        y = y + halo_ref[SUBLANES - lag:SUBLANES - lag + ch, :] * cw_ref[j:j + 1, :]
    act = _silu(y)

    row_ok = (c * ch + _iota((ch, 1), 0)) < t_valid
    col_ok = (c * ch + _iota((1, ch), 1)) < t_valid

    gb = gb_ref[...]
    g_col = jnp.where(row_ok, -jnp.exp(alr_ref[...]) * _softplus(gb + dtr_ref[...]), 0.0)
    beta_col = jnp.where(row_ok, _sigmoid(gb), 0.0)
    gbt = gbt_ref[...]
    g_row = jnp.where(col_ok, -jnp.exp(alc_ref[:, :ch]) * _softplus(gbt + dtc_ref[:, :ch]), 0.0)

    ri = _iota((ch, ch), 0)
    ci = _iota((ch, ch), 1)
    tri = ri >= ci
    strict = ri > ci
    eye = jnp.where(ri == ci, 1.0, 0.0)
    cum_col = _dot_ones(jnp.where(tri, 1.0, 0.0).astype(BF16), g_col, False)
    cum_row = _dot_ones(jnp.where(ri <= ci, 1.0, 0.0).astype(BF16), g_row, True)

    heads = range(B_HEADS)
    qs, ks, vs, ccs, betas, decays, lows = [], [], [], [], [], [], []
    for h in heads:
        q = act[:, h * hd:(h + 1) * hd]
        k = act[:, b_dim + h * hd:b_dim + (h + 1) * hd]
        v = act[:, 2 * b_dim + h * hd:2 * b_dim + (h + 1) * hd]
        q = q * lax.rsqrt(jnp.sum(q * q, axis=-1, keepdims=True) + 1e-6) * (hd ** -0.5)
        k = k * lax.rsqrt(jnp.sum(k * k, axis=-1, keepdims=True) + 1e-6)
        qs.append(jnp.where(row_ok, q, 0.0))
        ks.append(jnp.where(row_ok, k, 0.0))
        vs.append(jnp.where(row_ok, v, 0.0))
        cc = cum_col[:, h:h + 1]
        cr = cum_row[h:h + 1, :]
        ccs.append(cc)
        betas.append(beta_col[:, B_HEADS + h:B_HEADS + h + 1])
        decays.append(jnp.where(tri, jnp.exp(jnp.where(tri, cc - cr, 0.0)), 0.0))
    qk_kk = [_dot_nt(jnp.concatenate([qs[h], ks[h]], axis=0), ks[h]) for h in heads]
    lows = [jnp.where(strict, betas[h] * qk_kk[h][ch:, :] * decays[h], 0.0) for h in heads]
    invs = [eye - lows[h] for h in heads]
    pw_parts = [_split2(lows[h]) for h in heads]
    pws = [_dot_x3(pw_parts[h], pw_parts[h]) for h in heads]
    size = 2
    while size < ch:
        pw_parts = [_split2(pws[h]) for h in heads]
        invs = [invs[h] + _dot_x3(_split2(invs[h]), pw_parts[h]) for h in heads]
        size *= 2
        if size < ch:
            pws = [_dot_x3(pw_parts[h], pw_parts[h]) for h in heads]
    rhs = [jnp.concatenate([vs[h] * betas[h], ks[h] * (betas[h] * jnp.exp(ccs[h]))], axis=1) for h in heads]
    uws = [_dot_x3(_split2(invs[h]), _split2(rhs[h])) for h in heads]

    @pl.when(c == 0)
    def _():
        s_ref[...] = s0_ref[...]

    c_lasts = [ccs[h][ch - 1:ch, :] for h in heads]
    ss = [s_ref[h].astype(BF16) for h in heads]
    ws_qs = [_dot(jnp.concatenate([uws[h][:, hd:], qs[h] * jnp.exp(ccs[h])], axis=0), ss[h]) for h in heads]
    v_news = [uws[h][:, :hd] - ws_qs[h][:ch, :] for h in heads]
    os_ = [ws_qs[h][ch:, :] + _dot(qk_kk[h][:ch, :] * decays[h], v_news[h]) for h in heads]
    upd = [_dot_tn(ks[h] * jnp.exp(c_lasts[h] - ccs[h]), v_news[h]) for h in heads]
    for h in heads:
        sl = slice(h * hd, (h + 1) * hd)
        s_ref[h] = s_ref[h] * jnp.exp(c_lasts[h]) + upd[h]
        o_ref[:, sl] = (_rms(os_[h], nw_ref[...]) * _silu(z_ref[:, sl])).astype(o_ref.dtype)

    @pl.when(c == pl.num_programs(1) - 1)
    def _():
        sfin_ref[...] = s_ref[...]


def _gdn(proj, buf8, conv_w, zgb, gbt, a_log, dt_bias, s0, norm_w, t_valid):
    bsz, t, _ = proj.shape
    c3 = conv_w.shape[1]
    ch = B_CHUNK
    nc = t // ch
    b_dim = B_HEADS * HEAD_DIM
    gb_blk = b_dim // LANES
    pad = jnp.zeros((LANES - B_HEADS,), F32)
    al_row = jnp.concatenate([a_log.astype(F32), pad]).reshape(1, LANES)
    dt_row = jnp.concatenate([dt_bias.astype(F32), pad]).reshape(1, LANES)
    pad_c = jnp.zeros((2 * SUBLANES - B_HEADS,), F32)
    al_col = jnp.broadcast_to(jnp.concatenate([a_log.astype(F32), pad_c])[:, None], (2 * SUBLANES, LANES))
    dt_col = jnp.broadcast_to(jnp.concatenate([dt_bias.astype(F32), pad_c])[:, None], (2 * SUBLANES, LANES))
    full = lambda shape: pl.BlockSpec(shape, lambda b, c: (0,) * len(shape))
    row_spec = pl.BlockSpec((None, ch, b_dim), lambda b, c: (b, c, 0))
    st_spec = pl.BlockSpec((None, B_HEADS, HEAD_DIM, HEAD_DIM), lambda b, c: (b, 0, 0, 0))
    return pl.pallas_call(
        functools.partial(_gdn_kernel, t_valid=t_valid),
        grid=(bsz, nc),
        in_specs=[pl.BlockSpec((None, ch, c3), lambda b, c: (b, c, 1)),
                  pl.BlockSpec((None, SUBLANES, c3), lambda b, c: (b, jnp.maximum(c * (ch // SUBLANES) - 1, 0), 1)),
                  pl.BlockSpec((None, SUBLANES, c3), lambda b, c: (b, 0, 0)),
                  full((B_CONV, c3)),
                  pl.BlockSpec((None, ch, LANES), lambda b, c: (b, c, gb_blk)),
                  pl.BlockSpec((None, None, 2 * SUBLANES, ch), lambda b, c: (b, c, 0, 0)),
                  full((1, LANES)), full((1, LANES)),
                  full((2 * SUBLANES, LANES)), full((2 * SUBLANES, LANES)),
                  st_spec, row_spec, full((1, HEAD_DIM))],
        out_specs=[row_spec, st_spec],
        out_shape=[jax.ShapeDtypeStruct((bsz, t, b_dim), BF16),
                   jax.ShapeDtypeStruct((bsz, B_HEADS, HEAD_DIM, HEAD_DIM), F32)],
        scratch_shapes=[pltpu.VMEM((SUBLANES + ch, c3), F32),
                        pltpu.VMEM((B_HEADS, HEAD_DIM, HEAD_DIM), F32)],
        compiler_params=_cparams("parallel", "arbitrary"),
        name="gdn",
    )(proj, proj, buf8, conv_w, zgb, gbt, al_row, dt_row, al_col, dt_col, s0, zgb, norm_w)


def _hgrn_kernel(q_ref, f_ref, i_ref, g_ref, llb_ref, l1m_ref, oml_ref, s0_ref, nw_ref,
                 o_ref, sfin_ref, st_ref, *, t_valid):
    c = pl.program_id(1)
    ch = C_CHUNK
    hd = HEAD_DIM

    @pl.when(c == 0)
    def _():
        for h in range(C_HEADS):
            st_ref[h] = s0_ref[h].T

    ri = _iota((ch, ch), 0)
    ci = _iota((ch, ch), 1)
    sels, level_masks = [jnp.where(ri >= ci, 1.0, 0.0)], []
    half = SUBLANES
    while half < ch:
        base = (ri // (2 * half)) * (2 * half)
        ref_row = base + half - 1
        later = (ri - base) >= half
        lo_col = jnp.where(later, ref_row, ri)
        hi_col = jnp.where(later, ri, ref_row)
        sels.append(jnp.where(ci > lo_col, jnp.where(ci <= hi_col, 1.0, 0.0), 0.0))
        cbase = (ci // (2 * half)) * (2 * half)
        level_masks.append(later & (cbase == base) & ((ci - cbase) < half))
        half *= 2
    sel_all = jnp.concatenate(sels, axis=0).astype(BF16)
    lag_masks = [(ci == ri - lag) & ((ri % SUBLANES) >= lag) for lag in range(SUBLANES)]
    ones = jnp.ones((hd, hd), BF16)
    row_ok = None if t_valid is None else (c * ch + _iota((ch, 1), 0)) < t_valid

    def gates(h):
        sl = slice(h * hd, (h + 1) * hd)
        f = f_ref[:, sl]
        log_sig = jnp.minimum(f, 0.0) - jnp.log(1.0 + jnp.exp(-jnp.abs(f)))
        a = llb_ref[:, sl]
        b = l1m_ref[:, sl] + log_sig
        log2_f = (jnp.maximum(a, b) + jnp.log(1.0 + jnp.exp(-jnp.abs(a - b)))) * _LOG2_E
        k = oml_ref[:, sl] * _sigmoid(-f)
        q = q_ref[:, sl] * (hd ** -0.5)
        v = i_ref[:, sl]
        if row_ok is not None:
            log2_f = jnp.where(row_ok, log2_f, 0.0)
            k = jnp.where(row_ok, k, 0.0)
            q = jnp.where(row_ok, q, 0.0)
            v = jnp.where(row_ok, v, 0.0)
        return log2_f, q, k, v

    def level_att(sums, q, k):
        att = jnp.zeros((ch, ch), F32)
        for i, mask in enumerate(level_masks):
            e = jnp.exp2(sums[(i + 1) * ch:(i + 2) * ch, :])
            att = att + jnp.where(mask, _dot_nt(q * e, k * e), 0.0)
        return att

    def lag_stack(sums, q, k):
        cum = sums[:ch, :]
        terms = [q * k]
        for lag in range(1, SUBLANES):
            terms.append(q * _roll_in_blocks(k, lag) * jnp.exp2(cum - _roll_in_blocks(cum, lag)))
        return jnp.concatenate(terms, axis=0).astype(BF16)

    def add_lags(att, row_sums):
        for lag in range(SUBLANES):
            att = att + jnp.where(lag_masks[lag], row_sums[lag * ch:(lag + 1) * ch, :ch], 0.0)
        return att

    for h0 in range(0, C_HEADS, _HGRN_HEAD_GROUP):
        heads = range(h0, h0 + _HGRN_HEAD_GROUP)
        gs = [gates(h) for h in heads]
        sums = [_dot_ones(sel_all, g[0], False) for g in gs]
        atts = [level_att(s, g[1], g[2]) for s, g in zip(sums, gs)]
        row_sums = [jnp.dot(lag_stack(s, g[1], g[2]), ones, preferred_element_type=F32) for s, g in zip(sums, gs)]
        atts = [add_lags(a, r) for a, r in zip(atts, row_sums)]
        for h, s, (_, q, k, v), att in zip(heads, sums, gs, atts):
            cum = s[:ch, :]
            c_last = cum[ch - 1:ch, :]
            st = st_ref[h]
            o = _dot_nt(q * jnp.exp2(cum), st) + _dot(att, v)
            st_ref[h] = st * jnp.exp2(c_last) + _dot_tn(v, k * jnp.exp2(c_last - cum))
            sl = slice(h * hd, (h + 1) * hd)
            o_ref[:, sl] = (_rms(o, nw_ref[...]) * _silu(g_ref[:, sl])).astype(o_ref.dtype)

    @pl.when(c == pl.num_programs(1) - 1)
    def _():
        for h in range(C_HEADS):
            sfin_ref[h] = st_ref[h].T


def _hgrn(proj, lb, s0, norm_w, t_valid):
    bsz, t, four_w = proj.shape
    width = four_w // 4
    ch = C_CHUNK
    nc = t // ch
    lb = lb.astype(F32).reshape(1, width)
    col = lambda j: pl.BlockSpec((None, ch, width), lambda b, c: (b, c, j))
    vec = pl.BlockSpec((1, width), lambda b, c: (0, 0))
    st_spec = pl.BlockSpec((None, C_HEADS, HEAD_DIM, HEAD_DIM), lambda b, c: (b, 0, 0, 0))
    return pl.pallas_call(
        functools.partial(_hgrn_kernel, t_valid=None if t_valid == t else t_valid),
        grid=(bsz, nc),
        in_specs=[col(0), col(1), col(2), col(3), vec, vec, vec, st_spec,
                  pl.BlockSpec((1, HEAD_DIM), lambda b, c: (0, 0))],
        out_specs=[pl.BlockSpec((None, ch, width), lambda b, c: (b, c, 0)), st_spec],
        out_shape=[jax.ShapeDtypeStruct((bsz, t, width), BF16),
                   jax.ShapeDtypeStruct((bsz, C_HEADS, HEAD_DIM, HEAD_DIM), F32)],
        scratch_shapes=[pltpu.VMEM((C_HEADS, HEAD_DIM, HEAD_DIM), F32)],
        compiler_params=_cparams("parallel", "arbitrary"),
        name="hgrn2",
    )(proj, proj, proj, proj, jnp.log(lb), jnp.log1p(-lb), 1.0 - lb, s0, norm_w)


def _pad_time(x, multiple):
    t = x.shape[1]
    t_pad = -(-t // multiple) * multiple
    return x if t_pad == t else jnp.pad(x, ((0, 0), (0, t_pad - t), (0, 0)))


def _ab_mixer(proj, zgb, t_valid, rope_tabs, kv_past, s0, conv_buf, j, conv_w, a_log, dt_bias, norm_w):
    bsz, t, _ = proj.shape
    a_dim = A_HEADS * HEAD_DIM
    b_dim = B_HEADS * HEAD_DIM

    q_r, k_r = _rope(proj, *rope_tabs)
    if kv_past is None:
        o_a = _dilated_attention(q_r, k_r, proj)
        keep = min(A_GROUPS[-1][0], t_valid)
        k_rows = k_r[:, t_valid - keep:t_valid]
        v_rows = proj[:, t_valid - keep:t_valid, 2 * a_dim:3 * a_dim]
    else:
        o_a = _cache_attention(q_r, k_r, proj, kv_past[0], kv_past[1], j, t_valid)
        k_rows = k_r[:, :t_valid]
        v_rows = proj[:, :t_valid, 2 * a_dim:3 * a_dim]
    k_rows = k_rows.reshape(bsz, -1, A_HEADS, HEAD_DIM)
    v_rows = v_rows.reshape(bsz, -1, A_HEADS, HEAD_DIM)

    raw_tail = proj[:, max(t_valid - (B_CONV - 1), 0):t_valid, 3 * a_dim:]
    buf_new = jnp.concatenate([conv_buf, raw_tail], axis=1)[:, -(B_CONV - 1):]
    buf8 = jnp.pad(conv_buf, ((0, 0), (SUBLANES - (B_CONV - 1), 0), (0, 0)))
    proj_c, zgb_c = _pad_time(proj, B_CHUNK), _pad_time(zgb, B_CHUNK)
    nc = proj_c.shape[1] // B_CHUNK
    gbt = zgb_c[:, :, b_dim:b_dim + 2 * SUBLANES].reshape(bsz, nc, B_CHUNK, 2 * SUBLANES).swapaxes(2, 3)
    o_b, s_new = _gdn(proj_c, buf8, conv_w, zgb_c, gbt, a_log, dt_bias, s0, norm_w.reshape(1, HEAD_DIM), t_valid)

    mixed = [o_a.reshape(bsz * t, a_dim), o_b[:, :t].reshape(bsz * t, b_dim)]
    return mixed, (k_rows, v_rows, s_new, buf_new)


def _hgrn_mixer(proj, t_valid, s0, lb, norm_w):
    bsz, t, n_proj = proj.shape
    o, s_new = _hgrn(_pad_time(proj, C_CHUNK), lb, s0, norm_w.reshape(1, HEAD_DIM), t_valid)
    return [o[:, :t].reshape(bsz * t, n_proj // 4)], s_new


def _trunk(groups, mods5, p):
    main, rider = groups
    d = main["x"].shape[2]
    depth = p["norm_pre"].shape[0]
    tm = 1024
    a_dim = A_HEADS * HEAD_DIM
    b_dim = B_HEADS * HEAD_DIM
    d_ff = p["ffn_wo"].shape[2]
    wi2d = p["ffn_wi"].reshape(-1, 2 * d_ff)
    wo2d = p["ffn_wo"].reshape(-1, d)
    ab_in2d = p["ab_w_in"].reshape(-1, p["ab_w_in"].shape[2])
    ab_out2d = p["ab_w_out"].reshape(-1, d)
    c_in2d = p["c_w_in"].reshape(-1, p["c_w_in"].shape[2])
    c_out2d = p["c_w_out"].reshape(-1, d)
    norm_pre = p["norm_pre"].reshape(depth, 3, 1, d)
    norm_post = p["norm_post"].reshape(depth, 3, 1, d)
    lb_all = jnp.cumsum(jax.nn.softmax(p["c_lower_bounds"].astype(F32), axis=0), axis=0)
    lb_all = lb_all - lb_all[0:1]

    def rows2d(g, a):
        return a.reshape(g["x"].shape[0] * g["x"].shape[1], a.shape[-1])

    def rows3d(g, a):
        return a.reshape(g["x"].shape[0], g["x"].shape[1], a.shape[-1])

    def project(w2d, k_block, n, tn):
        outs = _mm([rows2d(main, main["h"])], w2d, k_block=k_block, n0=0, n=n, tn=tn, tm=tm,
                   rider=[rows2d(rider, rider["h"])])
        return [rows3d(g, o) for g, o in zip(groups, outs)]

    for g in groups:
        g["rope"] = _rope_tables(g["pos"])
        g["states"] = ([], [], [], [], [])
        (g["h"],) = _norm_call(g["x"], mods5, g["boff"], pre=(norm_pre[0, 0], 0, 0, 1))
    for layer in range(depth):
        j = layer // 2
        for sub in range(3):
            if sub == 1:
                if layer % 2 == 0:
                    w_zgb = p["w_zgb"][j]
                    projs = project(ab_in2d, j, 3 * a_dim + 3 * b_dim, 1024)
                    zgbs = project(w_zgb, 0, w_zgb.shape[1], w_zgb.shape[1])
                    for g, proj, zgb in zip(groups, projs, zgbs):
                        bsz, past = g["x"].shape[0], g["past"]
                        if past is None:
                            kv_past = None
                            s0 = jnp.zeros((bsz, B_HEADS, HEAD_DIM, HEAD_DIM), F32)
                            buf = jnp.zeros((bsz, B_CONV - 1, 3 * b_dim), F32)
                        else:
                            kv_past, s0, buf = (past[0], past[1]), past[2][j], past[3][j]
                        g["mixed"], new = _ab_mixer(proj, zgb, g["t_valid"], g["rope"], kv_past, s0, buf, j,
                                                    p["b_conv_w"][j], p["b_a_log"][j], p["b_dt_bias"][j],
                                                    p["b_norm"][j])
                        for dst, val in zip(g["states"][:4], new):
                            dst.append(val)
                else:
                    projs = project(c_in2d, j, c_in2d.shape[1], 1024)
                    for g, proj in zip(groups, projs):
                        bsz, past = g["x"].shape[0], g["past"]
                        s0 = jnp.zeros((bsz, C_HEADS, HEAD_DIM, HEAD_DIM), F32) if past is None else past[4][j]
                        g["mixed"], s_new = _hgrn_mixer(proj, g["t_valid"], s0, lb_all[j], p["c_norm"][j])
                        g["states"][4].append(s_new)
                w_out2d = ab_out2d if layer % 2 == 0 else c_out2d
                post = (norm_post[layer, 1], layer, 5, 1.0)
                pre = (norm_pre[layer, 2], layer, 6, 7)
                for g in groups:
                    if g["x"].shape[1] >= _MM_NORM_MIN_ROWS:
                        g["x"], g["h"] = _mm_norm(g["mixed"], w_out2d, j, g["x"], mods5, g["boff"], post=post, pre=pre)
                    else:
                        y = _mm(g["mixed"], w_out2d, k_block=j, n0=0, n=d, tn=1024, tm=tm)
                        g["x"], g["h"] = _norm_call(g["x"], mods5, g["boff"], post=(rows3d(g, y),) + post, pre=pre)
                continue
            k_ffn = layer * 2 + (0 if sub == 0 else 1)
            acts = _swiglu_in(rows2d(main, main["h"]), wi2d, k_ffn, 2 * tm, rider=rows2d(rider, rider["h"]))
            ys = _mm([acts[0]], wo2d, k_block=k_ffn, n0=0, n=d, tn=512, tm=512, rider=[acts[1]])
            if sub == 0:
                pre = (norm_pre[layer, 1], layer, 3, 4)
            elif layer + 1 < depth:
                pre = (norm_pre[layer + 1, 0], layer + 1, 0, 1)
            else:
                pre = None
            for g, y in zip(groups, ys):
                post = (rows3d(g, y), norm_post[layer, sub], layer, 3 * sub + 2, FFN_RESIDUAL)
                outs = _norm_call(g["x"], mods5, g["boff"], post=post, pre=pre)
                g["x"] = outs[0]
                g["h"] = outs[1] if pre is not None else None
    return [(g["x"],) + tuple(jnp.stack(s) for s in g["states"]) for g in groups]


def kernel(x_prompt, x_sample, cache_a_k, cache_a_v, state_b_s, state_b_conv, state_c_s, c_prompt, c_sample,
           ada_w, ada_b, norm_pre, norm_post, ffn_wi, ffn_wo, ab_w_in, ab_w_out, b_conv_w, b_a_log, b_dt_bias,
           b_norm, c_w_in, c_w_out, c_lower_bounds, c_norm):
    p = dict(norm_pre=norm_pre, norm_post=norm_post, ffn_wi=ffn_wi, ffn_wo=ffn_wo, ab_w_in=ab_w_in,
             ab_w_out=ab_w_out, b_conv_w=b_conv_w, b_a_log=b_a_log, b_dt_bias=b_dt_bias, b_norm=b_norm,
             c_w_in=c_w_in, c_w_out=c_w_out, c_lower_bounds=c_lower_bounds, c_norm=c_norm)
    depth, d = norm_pre.shape[0], x_prompt.shape[2]
    main_cols = (3 * A_HEADS + 3 * B_HEADS) * HEAD_DIM
    gb_cols = 2 * B_HEADS
    p["w_zgb"] = [jnp.concatenate([ab_w_in[j, :, main_cols + gb_cols:], ab_w_in[j, :, main_cols:main_cols + gb_cols],
                                   jnp.zeros((d, LANES - gb_cols), F32)], axis=1)
                  for j in range(ab_w_in.shape[0])]
    n_p, t_p = x_prompt.shape[0], x_prompt.shape[1]
    n_s, t_s = x_sample.shape[0], x_sample.shape[1]
    past_len = 16384

    rows = -(-(n_p + n_s) // (2 * SUBLANES)) * (2 * SUBLANES)
    c_all = jnp.concatenate([c_prompt, c_sample, jnp.zeros((rows - n_p - n_s, d), F32)], axis=0)
    mods5 = _ada_mods(c_all, ada_w, ada_b).reshape(depth, rows, N_MOD, 1, d)

    prompt = dict(x=x_prompt, t_valid=t_p, pos=jnp.arange(t_p, dtype=jnp.int32), boff=0, past=None)
    sample = dict(x=jnp.pad(x_sample, ((0, 0), (0, SAMPLE_T_PAD - t_s), (0, 0))), t_valid=t_s,
                  pos=past_len + jnp.arange(SAMPLE_T_PAD, dtype=jnp.int32), boff=n_p,
                  past=(cache_a_k, cache_a_v, state_b_s, state_b_conv, state_c_s))
    (y_p, ak_p, av_p, bs_p, bc_p, cs_p), (y_s, ak_s, av_s, bs_s, bc_s, cs_s) = _trunk((prompt, sample), mods5, p)
    return (y_p, y_s[:, :t_s], ak_p, av_p, bs_p, bc_p, cs_p, ak_s, av_s, bs_s, bc_s, cs_s)
```

```python
import functools
import math

import jax
import jax.numpy as jnp
from jax import lax
from jax.experimental import pallas as pl
from jax.experimental.pallas import tpu as pltpu

F32 = jnp.float32
BF16 = jnp.bfloat16
HIGHEST = lax.Precision.HIGHEST

LANES = 128
SUBLANES = 8
VMEM_LIMIT_BYTES = 56 * 1024 * 1024

_LOG2_E = 1.4426950408889634
NORM_EPS = 1e-6
FFN_RESIDUAL = 0.5
N_MOD = 9
ROPE_THETA = 10000.0
A_HEADS = 8
A_GROUPS = ((128, 1), (512, 4), (2048, 16))
A_BLOCK = 128
B_HEADS = 8
B_CONV = 4
B_CHUNK = 64
C_HEADS = 16
HEAD_DIM = 128
SAMPLE_T_PAD = 16
C_CHUNK = 64
_MM_NORM_MIN_ROWS = 512
_HGRN_HEAD_GROUP = 16


def _cparams(*sem):
    return pltpu.CompilerParams(dimension_semantics=sem, vmem_limit_bytes=VMEM_LIMIT_BYTES)


def _sigmoid(x):
    return 1.0 / (1.0 + jnp.exp(-x))


def _silu(x):
    return x * _sigmoid(x)


def _softplus(x):
    return jnp.maximum(x, 0.0) + jnp.log1p(jnp.exp(-jnp.abs(x)))


def _rms(x, g):
    return x * lax.rsqrt(jnp.mean(x * x, axis=-1, keepdims=True) + NORM_EPS) * g


def _dot(a, b):
    return jnp.dot(a.astype(BF16), b.astype(BF16), preferred_element_type=F32)


def _dot_nt(a, b):
    return lax.dot_general(a.astype(BF16), b.astype(BF16), (((1,), (1,)), ((), ())),
                           preferred_element_type=F32)


def _dot_tn(a, b):
    return lax.dot_general(a.astype(BF16), b.astype(BF16), (((0,), (0,)), ((), ())),
                           preferred_element_type=F32)


def _split2(x):
    hi = x.astype(BF16)
    lo = (x - hi.astype(F32)).astype(BF16)
    return hi, lo


def _dot_x3(a_parts, b_parts):
    (ah, al), (bh, bl) = a_parts, b_parts
    dot = functools.partial(jnp.dot, preferred_element_type=F32)
    return dot(ah, bh) + (dot(ah, bl) + dot(al, bh))


def _dot_ones(ones_bf16, x, ones_on_right):
    hi = x.astype(BF16)
    r1 = x - hi.astype(F32)
    mid = r1.astype(BF16)
    lo = (r1 - mid.astype(F32)).astype(BF16)
    dot = functools.partial(jnp.dot, preferred_element_type=F32)
    if ones_on_right:
        return dot(hi, ones_bf16) + (dot(mid, ones_bf16) + dot(lo, ones_bf16))
    return dot(ones_bf16, hi) + (dot(ones_bf16, mid) + dot(ones_bf16, lo))


def _roll_in_blocks(x, lag):
    rows, cols = x.shape
    x3 = x.reshape(rows // SUBLANES, SUBLANES, cols)
    return pltpu.roll(x3, lag, 1).reshape(rows, cols)


def _iota(shape, dim):
    return lax.broadcasted_iota(jnp.int32, shape, dim)


def _ada_kernel(c_ref, w_ref, b_ref, o_ref):
    a = _silu(c_ref[...])
    o_ref[...] = _dot(a, w_ref[...]) + b_ref[...]


def _ada_mods(c_all, ada_w, ada_b):
    depth, d, n = ada_w.shape
    rows = c_all.shape[0]
    tn = 1024
    return pl.pallas_call(
        _ada_kernel,
        grid=(depth, n // tn),
        in_specs=[pl.BlockSpec((rows, d), lambda l, j: (0, 0)),
                  pl.BlockSpec((None, d, tn), lambda l, j: (l, 0, j)),
                  pl.BlockSpec((None, 1, tn), lambda l, j: (l, 0, j))],
        out_specs=pl.BlockSpec((None, rows, tn), lambda l, j: (l, 0, j)),
        out_shape=jax.ShapeDtypeStruct((depth, rows, n), F32),
        compiler_params=_cparams("parallel", "parallel"),
        name="ada_mods",
    )(c_all, ada_w, ada_b.reshape(depth, 1, n))


def _norm_kernel(*refs, has_post, has_pre, coef):
    it = iter(refs)
    x_ref = next(it)
    if has_post:
        y_ref, gpost_ref, gate_ref = next(it), next(it), next(it)
    if has_pre:
        gpre_ref, shift_ref, scale_ref = next(it), next(it), next(it)
    x = x_ref[...]
    if has_post:
        xo_ref = next(it)
        x = x + (coef * gate_ref[...]) * _rms(y_ref[...], gpost_ref[...])
        xo_ref[...] = x
    if has_pre:
        h_ref = next(it)
        h = _rms(x, gpre_ref[...]) * (1.0 + scale_ref[...]) + shift_ref[...]
        h_ref[...] = h.astype(BF16)


def _norm_call(x, mods5, boff, post=None, pre=None):
    bsz, t, d = x.shape
    tt = min(t, 512)
    row = pl.BlockSpec((None, tt, d), lambda b, i: (b, i, 0))
    vec = pl.BlockSpec((1, d), lambda b, i: (0, 0))

    def mod_spec(layer, k):
        return pl.BlockSpec((None, None, None, 1, d), lambda b, i: (layer, boff + b, k, 0, 0))

    args, in_specs, out_shape, out_specs = [x], [row], [], []
    coef = 1.0
    if post is not None:
        y, g_post, layer, gate_idx, coef = post
        args += [y, g_post, mods5]
        in_specs += [row, vec, mod_spec(layer, gate_idx)]
        out_shape.append(jax.ShapeDtypeStruct(x.shape, F32))
        out_specs.append(row)
    if pre is not None:
        g_pre, layer, shift_idx, scale_idx = pre
        args += [g_pre, mods5, mods5]
        in_specs += [vec, mod_spec(layer, shift_idx), mod_spec(layer, scale_idx)]
        out_shape.append(jax.ShapeDtypeStruct(x.shape, BF16))
        out_specs.append(row)
    outs = pl.pallas_call(
        functools.partial(_norm_kernel, has_post=post is not None, has_pre=pre is not None, coef=coef),
        grid=(bsz, t // tt),
        in_specs=in_specs, out_specs=out_specs, out_shape=out_shape,
        compiler_params=_cparams("parallel", "parallel"),
        name="sandwich_norm",
    )(*args)
    return outs


def _mm_kernel(*refs, k_sizes, has_rider, w_rows_are_outputs):
    n_x = len(k_sizes)
    x_refs = refs[:n_x]
    r_refs = refs[n_x:2 * n_x] if has_rider else ()
    rest = refs[n_x + len(r_refs):]
    w_ref, o_ref = rest[0], rest[1]
    ro_ref = rest[2] if has_rider else None
    wbf_ref = rest[-1]

    def product(parts):
        acc, off = None, 0
        for x_ref, ks in zip(parts, k_sizes):
            if w_rows_are_outputs:
                part = _dot_nt(x_ref[...], wbf_ref[:, off:off + ks])
            else:
                part = jnp.dot(x_ref[...].astype(BF16), wbf_ref[off:off + ks, :], preferred_element_type=F32)
            acc = part if acc is None else acc + part
            off += ks
        return acc

    @pl.when(pl.program_id(1) == 0)
    def _():
        wbf_ref[...] = w_ref[...].astype(BF16)
        if has_rider:
            ro_ref[...] = product(r_refs).astype(ro_ref.dtype)

    o_ref[...] = product(x_refs).astype(o_ref.dtype)


def _mm(xs, w2d, *, k_block, n0, n, tn, tm, out_dtype=F32, rider=None, w_t=None):
    m = xs[0].shape[0]
    k_sizes = tuple(x.shape[1] for x in xs)
    k = sum(k_sizes)
    tm = min(tm, m)
    assert m % tm == 0 and n % tn == 0 and n0 % tn == 0
    assert (w2d.shape[0] % k == 0) if w_t is None else (w_t.shape[2] == k)
    nb0 = n0 // tn
    in_specs = [pl.BlockSpec((tm, ks), lambda j, i: (i, 0)) for ks in k_sizes]
    out_specs = [pl.BlockSpec((tm, tn), lambda j, i: (i, j))]
    out_shape = [jax.ShapeDtypeStruct((m, n), out_dtype)]
    args = list(xs)
    if rider is not None:
        m2 = rider[0].shape[0]
        assert tuple(x.shape[1] for x in rider) == k_sizes
        in_specs += [pl.BlockSpec((m2, ks), lambda j, i: (0, 0)) for ks in k_sizes]
        out_specs.append(pl.BlockSpec((m2, tn), lambda j, i: (0, j)))
        out_shape.append(jax.ShapeDtypeStruct((m2, n), out_dtype))
        args += list(rider)
    if w_t is None:
        in_specs.append(pl.BlockSpec((k, tn), lambda j, i: (k_block, nb0 + j)))
        w_tile = (k, tn)
    else:
        in_specs.append(pl.BlockSpec((None, tn, k), lambda j, i: (k_block, nb0 + j, 0)))
        w_tile = (tn, k)
    outs = pl.pallas_call(
        functools.partial(_mm_kernel, k_sizes=k_sizes, has_rider=rider is not None,
                          w_rows_are_outputs=w_t is not None),
        grid=(n // tn, m // tm),
        in_specs=in_specs,
        out_specs=out_specs,
        out_shape=out_shape,
        scratch_shapes=[pltpu.VMEM(w_tile, BF16)],
        compiler_params=_cparams("parallel", "arbitrary"),
        name="matmul",
    )(*args, w2d if w_t is None else w_t)
    return outs[0] if rider is None else (outs[0], outs[1])


def _mm_norm_kernel(*refs, k_sizes, coef):
    n_x = len(k_sizes)
    x_refs = refs[:n_x]
    (w_ref, xres_ref, gpost_ref, gate_ref, gpre_ref, shift_ref, scale_ref,
     xo_ref, h_ref, wbf_ref) = refs[n_x:]

    @pl.when(pl.program_id(0) == 0)
    def _():
        wbf_ref[...] = w_ref[...].astype(BF16)

    rows = xres_ref.shape[0]
    n_split = 4 if rows % (4 * 2 * SUBLANES) == 0 else 1
    step = rows // n_split
    for r in range(n_split):
        rs = slice(r * step, (r + 1) * step)
        y, off = None, 0
        for x_ref, ks in zip(x_refs, k_sizes):
            part = jnp.dot(x_ref[rs, :].astype(BF16), wbf_ref[off:off + ks, :], preferred_element_type=F32)
            y = part if y is None else y + part
            off += ks
        x = xres_ref[rs, :] + (coef * gate_ref[...]) * _rms(y, gpost_ref[...])
        xo_ref[rs, :] = x
        h_ref[rs, :] = (_rms(x, gpre_ref[...]) * (1.0 + scale_ref[...]) + shift_ref[...]).astype(BF16)


def _mm_norm(xs, w2d, k_block, x_res, mods5, boff, post, pre):
    bsz, t, d = x_res.shape
    m = bsz * t
    k_sizes = tuple(x.shape[1] for x in xs)
    k = sum(k_sizes)
    tm = min(t, 512)
    assert t % tm == 0 and w2d.shape == (w2d.shape[0] // k * k, d)
    g_post, layer_post, gate_idx, coef = post
    g_pre, layer_pre, shift_idx, scale_idx = pre
    row = pl.BlockSpec((tm, d), lambda i: (i, 0))
    vec = pl.BlockSpec((1, d), lambda i: (0, 0))

    def mod_spec(layer, idx):
        return pl.BlockSpec((None, None, None, 1, d), lambda i: (layer, boff + (i * tm) // t, idx, 0, 0))

    in_specs = [pl.BlockSpec((tm, ks), lambda i: (i, 0)) for ks in k_sizes]
    in_specs += [pl.BlockSpec((k, d), lambda i: (k_block, 0), pipeline_mode=pl.Buffered(1)),
                 row, vec, mod_spec(layer_post, gate_idx), vec, mod_spec(layer_pre, shift_idx),
                 mod_spec(layer_pre, scale_idx)]
    x_new, h = pl.pallas_call(
        functools.partial(_mm_norm_kernel, k_sizes=k_sizes, coef=coef),
        grid=(m // tm,),
        in_specs=in_specs,
        out_specs=[row, row],
        out_shape=[jax.ShapeDtypeStruct((m, d), F32), jax.ShapeDtypeStruct((m, d), BF16)],
        scratch_shapes=[pltpu.VMEM((k, d), BF16)],
        compiler_params=_cparams("arbitrary"),
        name="matmul_norm",
    )(*xs, w2d, x_res.reshape(m, d), g_post, mods5, g_pre, mods5, mods5)
    return x_new.reshape(bsz, t, d), h.reshape(bsz, t, d)


_SWIGLU_GROUP = 4


def _swiglu_kernel(*refs, has_rider):
    ng = _SWIGLU_GROUP
    n_x = 2 if has_rider else 1
    x_ref, w_refs = refs[0], refs[n_x:n_x + 2 * ng]
    o_ref, wbf_ref = refs[n_x + 2 * ng], refs[-1]

    def gated(x, dst_ref):
        for g in range(ng):
            r = jnp.dot(x, wbf_ref[:, 2 * g * LANES:(2 * g + 2) * LANES], preferred_element_type=F32)
            u, v = r[:, :LANES], r[:, LANES:]
            dst_ref[:, g * LANES:(g + 1) * LANES] = (_silu(v) * u).astype(dst_ref.dtype)

    @pl.when(pl.program_id(1) == 0)
    def _():
        for g in range(ng):
            wbf_ref[:, (2 * g) * LANES:(2 * g + 1) * LANES] = w_refs[g][...].astype(BF16)
            wbf_ref[:, (2 * g + 1) * LANES:(2 * g + 2) * LANES] = w_refs[ng + g][...].astype(BF16)
        if has_rider:
            gated(refs[1][...], refs[n_x + 2 * ng + 1])

    gated(x_ref[...], o_ref)


def _swiglu_in(h, wi2d, k_block, tm, rider=None):
    m, k = h.shape
    d_ff = wi2d.shape[1] // 2
    assert d_ff % LANES == 0
    nblk = d_ff // LANES
    ng = _SWIGLU_GROUP
    tn = ng * LANES
    tm = min(tm, m)
    last = nblk - 1
    w_specs = [pl.BlockSpec((k, LANES), lambda j, i, g=g, base=base: (k_block, base + jnp.minimum(ng * j + g, last)))
               for base in (0, nblk) for g in range(ng)]
    x_specs = [pl.BlockSpec((tm, k), lambda j, i: (i, 0))]
    out_specs = [pl.BlockSpec((tm, tn), lambda j, i: (i, j))]
    out_shape = [jax.ShapeDtypeStruct((m, d_ff), BF16)]
    args = [h]
    if rider is not None:
        m2 = rider.shape[0]
        x_specs.append(pl.BlockSpec((m2, k), lambda j, i: (0, 0)))
        out_specs.append(pl.BlockSpec((m2, tn), lambda j, i: (0, j)))
        out_shape.append(jax.ShapeDtypeStruct((m2, d_ff), BF16))
        args.append(rider)
    outs = pl.pallas_call(
        functools.partial(_swiglu_kernel, has_rider=rider is not None),
        grid=(pl.cdiv(nblk, ng), m // tm),
        in_specs=x_specs + w_specs,
        out_specs=out_specs,
        out_shape=out_shape,
        scratch_shapes=[pltpu.VMEM((k, 2 * tn), BF16)],
        compiler_params=_cparams("parallel", "arbitrary"),
        name="swiglu_in",
    )(*args, *([wi2d] * (2 * ng)))
    return outs[0] if rider is None else (outs[0], outs[1])


def _rope_kernel(x_ref, cos_ref, sin_ref, q_ref, k_ref):
    cos, sin = cos_ref[...], sin_ref[...]
    for h in range(2 * A_HEADS):
        xh = x_ref[:, h * HEAD_DIM:(h + 1) * HEAD_DIM]
        r = xh * cos + pltpu.roll(xh, HEAD_DIM // 2, 1) * sin
        dst = q_ref if h < A_HEADS else k_ref
        hh = h % A_HEADS
        dst[:, hh * HEAD_DIM:(hh + 1) * HEAD_DIM] = r


def _rope(proj, cos_t, sin_t):
    bsz, t, _ = proj.shape
    a_dim = A_HEADS * HEAD_DIM
    tt = min(t, 256)
    out = jax.ShapeDtypeStruct((bsz, t, a_dim), F32)
    o_spec = pl.BlockSpec((None, tt, a_dim), lambda b, i: (b, i, 0))
    return pl.pallas_call(
        _rope_kernel,
        grid=(bsz, t // tt),
        in_specs=[pl.BlockSpec((None, tt, 2 * a_dim), lambda b, i: (b, i, 0)),
                  pl.BlockSpec((tt, HEAD_DIM), lambda b, i: (i, 0)),
                  pl.BlockSpec((tt, HEAD_DIM), lambda b, i: (i, 0))],
        out_specs=[o_spec, o_spec], out_shape=[out, out],
        compiler_params=_cparams("parallel", "parallel"),
        name="rope",
    )(proj, cos_t, sin_t)


def _rope_tables(pos):
    half = HEAD_DIM // 2
    inv_freq = jnp.power(ROPE_THETA, -jnp.arange(half, dtype=F32) / half)
    ang = pos.astype(F32)[:, None] * inv_freq[None, :]
    cos, sin = jnp.cos(ang), jnp.sin(ang)
    return jnp.concatenate([cos, cos], axis=1), jnp.concatenate([-sin, sin], axis=1)


_A_TOKEN_BLOCK = A_BLOCK * max(dil for _, dil in A_GROUPS)


def _dilated_attn_kernel(q_ref, kp_ref, kc_ref, vp_ref, vc_ref, o_ref, m_ref, l_ref, acc_ref):
    tb = pl.program_id(2)
    blk = A_BLOCK
    tokens = _A_TOKEN_BLOCK
    qi = _iota((blk, 2 * blk), 0)
    ki = _iota((blk, 2 * blk), 1)
    rel = qi + blk - ki
    has_prev = (ki >= blk) | (tb > 0)
    scale = HEAD_DIM ** -0.5

    for g, (window, dil) in enumerate(A_GROUPS):
        span = window // dil
        band = (rel >= 0) & (rel <= span)
        reach = blk * dil

        def rows(ref, base, r, dil=dil):
            if dil == 1:
                return ref[base:base + blk, :]
            return ref[pl.ds(base + r, blk, stride=dil), :]

        def put(ref, base, r, val, dil=dil):
            if dil == 1:
                ref[base:base + blk, :] = val
            else:
                ref[pl.ds(base + r, blk, stride=dil), :] = val

        for r in range(dil):
            k_prev = rows(kp_ref, tokens - reach, r).astype(BF16)
            v_prev = rows(vp_ref, tokens - reach, r).astype(BF16)
            for s in range(tokens // reach):
                base = s * reach
                q = rows(q_ref, base, r)
                k_cur = rows(kc_ref, base, r).astype(BF16)
                v_cur = rows(vc_ref, base, r).astype(BF16)
                kcat = jnp.concatenate([k_prev, k_cur], axis=0)
                vcat = jnp.concatenate([v_prev, v_cur], axis=0)
                k_prev, v_prev = k_cur, v_cur
                valid = band & has_prev if s == 0 else band
                sc = jnp.where(valid, _dot_nt(q, kcat) * scale, -jnp.inf)
                m_loc = jnp.max(sc, axis=-1, keepdims=True)
                if g == 0:
                    p = jnp.exp(sc - m_loc)
                    put(m_ref, base, r, jnp.broadcast_to(m_loc, (blk, HEAD_DIM)))
                    put(l_ref, base, r, jnp.broadcast_to(jnp.sum(p, axis=-1, keepdims=True), (blk, HEAD_DIM)))
                    put(acc_ref, base, r, _dot(p, vcat))
                else:
                    m_old = rows(m_ref, base, r)
                    m_new = jnp.maximum(m_old, m_loc)
                    alpha = jnp.exp(m_old - m_new)
                    p = jnp.exp(sc - m_new[:, 0:1])
                    put(m_ref, base, r, m_new)
                    put(l_ref, base, r, alpha * rows(l_ref, base, r) + jnp.sum(p, axis=-1, keepdims=True))
                    put(acc_ref, base, r, alpha * rows(acc_ref, base, r) + _dot(p, vcat))

    o_ref[...] = (acc_ref[...] / l_ref[...]).astype(o_ref.dtype)


def _dilated_attention(q_r, k_r, proj):
    bsz, t, a_dim = q_r.shape
    tokens = _A_TOKEN_BLOCK
    assert t % tokens == 0 and all(window // dil <= A_BLOCK for window, dil in A_GROUPS)
    blk = (None, tokens, HEAD_DIM)
    v_off = 2 * a_dim // HEAD_DIM
    cur = lambda b, h, i: (b, i, h)
    prev = lambda b, h, i: (b, jnp.maximum(i - 1, 0), h)
    v_cur = lambda b, h, i: (b, i, v_off + h)
    v_prev = lambda b, h, i: (b, jnp.maximum(i - 1, 0), v_off + h)
    return pl.pallas_call(
        _dilated_attn_kernel,
        grid=(bsz, A_HEADS, t // tokens),
        in_specs=[pl.BlockSpec(blk, cur), pl.BlockSpec(blk, prev), pl.BlockSpec(blk, cur),
                  pl.BlockSpec(blk, v_prev), pl.BlockSpec(blk, v_cur)],
        out_specs=pl.BlockSpec(blk, cur),
        out_shape=jax.ShapeDtypeStruct((bsz, t, a_dim), BF16),
        scratch_shapes=[pltpu.VMEM((tokens, HEAD_DIM), F32)] * 3,
        compiler_params=_cparams("parallel", "parallel", "parallel"),
        name="dilated_attention",
    )(q_r, k_r, k_r, proj, proj)


def _group_count(d):
    cnt = jnp.zeros(d.shape, F32)
    for window, dil in A_GROUPS:
        hit = (d >= 0) & (d % dil == 0) & (d <= window)
        cnt = cnt + jnp.where(hit, 1.0, 0.0)
    return cnt


def _cache_attn_kernel(q_ref, kn_ref, vn_ref, kc_ref, vc_ref, o_ref, *, n_buf):
    tq = SUBLANES
    nh = A_HEADS
    scale = HEAD_DIM ** -0.5
    heads_of = lambda ref: jnp.concatenate([ref[0:tq, h * HEAD_DIM:(h + 1) * HEAD_DIM] for h in range(nh)], axis=0)
    q, k_new, v_new = heads_of(q_ref), heads_of(kn_ref), heads_of(vn_ref)
    k_c = kc_ref[...].reshape(n_buf * nh, HEAD_DIM)
    v_c = vc_ref[...].reshape(n_buf * nh, HEAD_DIM)

    r_c = _iota((nh * tq, n_buf * nh), 0)
    c_c = _iota((nh * tq, n_buf * nh), 1)
    same_c = (c_c % nh) == (r_c // tq)
    cnt_c = jnp.where(same_c, _group_count(n_buf + (r_c % tq) - (c_c // nh)), 0.0)
    r_n = _iota((nh * tq, nh * tq), 0)
    c_n = _iota((nh * tq, nh * tq), 1)
    same_n = (c_n // tq) == (r_n // tq)
    cnt_n = jnp.where(same_n, _group_count((r_n % tq) - (c_n % tq)), 0.0)

    s_c = jnp.where(cnt_c > 0, _dot_nt(q, k_c) * scale, -jnp.inf)
    s_n = jnp.where(cnt_n > 0, _dot_nt(q, k_new) * scale, -jnp.inf)
    m = jnp.maximum(jnp.max(s_c, axis=-1, keepdims=True), jnp.max(s_n, axis=-1, keepdims=True))
    p_c = cnt_c * jnp.exp(s_c - m)
    p_n = cnt_n * jnp.exp(s_n - m)
    den = jnp.sum(p_c, axis=-1, keepdims=True) + jnp.sum(p_n, axis=-1, keepdims=True)
    o = (_dot(p_c, v_c) + _dot(p_n, v_new)) / den
    o_rows = jnp.concatenate([o[h * tq:(h + 1) * tq, :] for h in range(nh)], axis=1)
    pad = jnp.zeros((o_ref.shape[0] - tq, nh * HEAD_DIM), F32)
    o_ref[...] = jnp.concatenate([o_rows, pad], axis=0).astype(o_ref.dtype)


def _cache_attention(q_r, k_r, proj, cache_k, cache_v, layer, t_valid):
    bsz, tp, a_dim = q_r.shape
    n_buf = cache_k.shape[2]
    assert t_valid <= SUBLANES <= tp
    new = pl.BlockSpec((None, tp, a_dim), lambda b: (b, 0, 0))
    v_new = pl.BlockSpec((None, tp, a_dim), lambda b: (b, 0, 2))
    cache = pl.BlockSpec((None, None, n_buf, A_HEADS, HEAD_DIM), lambda b: (layer, b, 0, 0, 0))
    return pl.pallas_call(
        functools.partial(_cache_attn_kernel, n_buf=n_buf),
        grid=(bsz,),
        in_specs=[new, new, v_new, cache, cache],
        out_specs=new,
        out_shape=jax.ShapeDtypeStruct((bsz, tp, a_dim), BF16),
        compiler_params=_cparams("parallel"),
        name="cache_attention",
    )(q_r, k_r, proj, cache_k, cache_v)


def _gdn_kernel(raw_ref, prev_ref, buf_ref, cw_ref, gb_ref, gbt_ref, alr_ref, dtr_ref, alc_ref, dtc_ref,
                s0_ref, z_ref, nw_ref, o_ref, sfin_ref, halo_ref, s_ref, *, t_valid):
    c = pl.program_id(1)
    ch = B_CHUNK
    hd = HEAD_DIM
    b_dim = B_HEADS * hd

    halo_ref[0:SUBLANES, :] = jnp.where(c == 0, buf_ref[...], prev_ref[...])
    halo_ref[SUBLANES:SUBLANES + ch, :] = raw_ref[...]
    y = raw_ref[...] * cw_ref[B_CONV - 1:B_CONV, :]
    for j in range(B_CONV - 1):
        lag = B_CONV - 1 - j
        y = y + halo_ref[SUBLANES - lag:SUBLANES - lag + ch, :] * cw_ref[j:j + 1, :]
    act = _silu(y)

    row_ok = (c * ch + _iota((ch, 1), 0)) < t_valid
    col_ok = (c * ch + _iota((1, ch), 1)) < t_valid

    gb = gb_ref[...]
    g_col = jnp.where(row_ok, -jnp.exp(alr_ref[...]) * _softplus(gb + dtr_ref[...]), 0.0)
    beta_col = jnp.where(row_ok, _sigmoid(gb), 0.0)
    gbt = gbt_ref[...]
    g_row = jnp.where(col_ok, -jnp.exp(alc_ref[:, :ch]) * _softplus(gbt + dtc_ref[:, :ch]), 0.0)

    ri = _iota((ch, ch), 0)
    ci = _iota((ch, ch), 1)
    tri = ri >= ci
    strict = ri > ci
    eye = jnp.where(ri == ci, 1.0, 0.0)
    cum_col = _dot_ones(jnp.where(tri, 1.0, 0.0).astype(BF16), g_col, False)
    cum_row = _dot_ones(jnp.where(ri <= ci, 1.0, 0.0).astype(BF16), g_row, True)

    heads = range(B_HEADS)
    qs, ks, vs, ccs, betas, decays, lows = [], [], [], [], [], [], []
    for h in heads:
        q = act[:, h * hd:(h + 1) * hd]
        k = act[:, b_dim + h * hd:b_dim + (h + 1) * hd]
        v = act[:, 2 * b_dim + h * hd:2 * b_dim + (h + 1) * hd]
        q = q * lax.rsqrt(jnp.sum(q * q, axis=-1, keepdims=True) + 1e-6) * (hd ** -0.5)
        k = k * lax.rsqrt(jnp.sum(k * k, axis=-1, keepdims=True) + 1e-6)
        qs.append(jnp.where(row_ok, q, 0.0))
        ks.append(jnp.where(row_ok, k, 0.0))
        vs.append(jnp.where(row_ok, v, 0.0))
        cc = cum_col[:, h:h + 1]
        cr = cum_row[h:h + 1, :]
        ccs.append(cc)
        betas.append(beta_col[:, B_HEADS + h:B_HEADS + h + 1])
        decays.append(jnp.where(tri, jnp.exp(jnp.where(tri, cc - cr, 0.0)), 0.0))
    qk_kk = [_dot_nt(jnp.concatenate([qs[h], ks[h]], axis=0), ks[h]) for h in heads]
    lows = [jnp.where(strict, betas[h] * qk_kk[h][ch:, :] * decays[h], 0.0) for h in heads]
    invs = [eye - lows[h] for h in heads]
    pw_parts = [_split2(lows[h]) for h in heads]
    pws = [_dot_x3(pw_parts[h], pw_parts[h]) for h in heads]
    size = 2
    while size < ch:
        pw_parts = [_split2(pws[h]) for h in heads]
        invs = [invs[h] + _dot_x3(_split2(invs[h]), pw_parts[h]) for h in heads]
        size *= 2
        if size < ch:
            pws = [_dot_x3(pw_parts[h], pw_parts[h]) for h in heads]
    rhs = [jnp.concatenate([vs[h] * betas[h], ks[h] * (betas[h] * jnp.exp(ccs[h]))], axis=1) for h in heads]
    uws = [_dot_x3(_split2(invs[h]), _split2(rhs[h])) for h in heads]

    @pl.when(c == 0)
    def _():
        s_ref[...] = s0_ref[...]

    c_lasts = [ccs[h][ch - 1:ch, :] for h in heads]
    ss = [s_ref[h].astype(BF16) for h in heads]
    ws_qs = [_dot(jnp.concatenate([uws[h][:, hd:], qs[h] * jnp.exp(ccs[h])], axis=0), ss[h]) for h in heads]
    v_news = [uws[h][:, :hd] - ws_qs[h][:ch, :] for h in heads]
    os_ = [ws_qs[h][ch:, :] + _dot(qk_kk[h][:ch, :] * decays[h], v_news[h]) for h in heads]
    upd = [_dot_tn(ks[h] * jnp.exp(c_lasts[h] - ccs[h]), v_news[h]) for h in heads]
    for h in heads:
        sl = slice(h * hd, (h + 1) * hd)
        s_ref[h] = s_ref[h] * jnp.exp(c_lasts[h]) + upd[h]
        o_ref[:, sl] = (_rms(os_[h], nw_ref[...]) * _silu(z_ref[:, sl])).astype(o_ref.dtype)

    @pl.when(c == pl.num_programs(1) - 1)
    def _():
        sfin_ref[...] = s_ref[...]


def _gdn(proj, buf8, conv_w, zgb, gbt, a_log, dt_bias, s0, norm_w, t_valid):
    bsz, t, _ = proj.shape
    c3 = conv_w.shape[1]
    ch = B_CHUNK
    nc = t // ch
    b_dim = B_HEADS * HEAD_DIM
    gb_blk = b_dim // LANES
    pad = jnp.zeros((LANES - B_HEADS,), F32)
    al_row = jnp.concatenate([a_log.astype(F32), pad]).reshape(1, LANES)
    dt_row = jnp.concatenate([dt_bias.astype(F32), pad]).reshape(1, LANES)
    pad_c = jnp.zeros((2 * SUBLANES - B_HEADS,), F32)
    al_col = jnp.broadcast_to(jnp.concatenate([a_log.astype(F32), pad_c])[:, None], (2 * SUBLANES, LANES))
    dt_col = jnp.broadcast_to(jnp.concatenate([dt_bias.astype(F32), pad_c])[:, None], (2 * SUBLANES, LANES))
    full = lambda shape: pl.BlockSpec(shape, lambda b, c: (0,) * len(shape))
    row_spec = pl.BlockSpec((None, ch, b_dim), lambda b, c: (b, c, 0))
    st_spec = pl.BlockSpec((None, B_HEADS, HEAD_DIM, HEAD_DIM), lambda b, c: (b, 0, 0, 0))
    return pl.pallas_call(
        functools.partial(_gdn_kernel, t_valid=t_valid),
        grid=(bsz, nc),
        in_specs=[pl.BlockSpec((None, ch, c3), lambda b, c: (b, c, 1)),
                  pl.BlockSpec((None, SUBLANES, c3), lambda b, c: (b, jnp.maximum(c * (ch // SUBLANES) - 1, 0), 1)),
                  pl.BlockSpec((None, SUBLANES, c3), lambda b, c: (b, 0, 0)),
                  full((B_CONV, c3)),
                  pl.BlockSpec((None, ch, LANES), lambda b, c: (b, c, gb_blk)),
                  pl.BlockSpec((None, None, 2 * SUBLANES, ch), lambda b, c: (b, c, 0, 0)),
                  full((1, LANES)), full((1, LANES)),
                  full((2 * SUBLANES, LANES)), full((2 * SUBLANES, LANES)),
                  st_spec, row_spec, full((1, HEAD_DIM))],
        out_specs=[row_spec, st_spec],
        out_shape=[jax.ShapeDtypeStruct((bsz, t, b_dim), BF16),
                   jax.ShapeDtypeStruct((bsz, B_HEADS, HEAD_DIM, HEAD_DIM), F32)],
        scratch_shapes=[pltpu.VMEM((SUBLANES + ch, c3), F32),
                        pltpu.VMEM((B_HEADS, HEAD_DIM, HEAD_DIM), F32)],
        compiler_params=_cparams("parallel", "arbitrary"),
        name="gdn",
    )(proj, proj, buf8, conv_w, zgb, gbt, al_row, dt_row, al_col, dt_col, s0, zgb, norm_w)


def _hgrn_kernel(q_ref, f_ref, i_ref, g_ref, llb_ref, l1m_ref, oml_ref, s0_ref, nw_ref,
                 o_ref, sfin_ref, st_ref, *, t_valid):
    c = pl.program_id(1)
    ch = C_CHUNK
    hd = HEAD_DIM

    @pl.when(c == 0)
    def _():
        for h in range(C_HEADS):
            st_ref[h] = s0_ref[h].T

    ri = _iota((ch, ch), 0)
    ci = _iota((ch, ch), 1)
    sels, level_masks = [jnp.where(ri >= ci, 1.0, 0.0)], []
    half = SUBLANES
    while half < ch:
        base = (ri // (2 * half)) * (2 * half)
        ref_row = base + half - 1
        later = (ri - base) >= half
        lo_col = jnp.where(later, ref_row, ri)
        hi_col = jnp.where(later, ri, ref_row)
        sels.append(jnp.where(ci > lo_col, jnp.where(ci <= hi_col, 1.0, 0.0), 0.0))
        cbase = (ci // (2 * half)) * (2 * half)
        level_masks.append(later & (cbase == base) & ((ci - cbase) < half))
        half *= 2
    sel_all = jnp.concatenate(sels, axis=0).astype(BF16)
    lag_masks = [(ci == ri - lag) & ((ri % SUBLANES) >= lag) for lag in range(SUBLANES)]
    ones = jnp.ones((hd, hd), BF16)
    row_ok = None if t_valid is None else (c * ch + _iota((ch, 1), 0)) < t_valid

    def gates(h):
        sl = slice(h * hd, (h + 1) * hd)
        f = f_ref[:, sl]
        log_sig = jnp.minimum(f, 0.0) - jnp.log(1.0 + jnp.exp(-jnp.abs(f)))
        a = llb_ref[:, sl]
        b = l1m_ref[:, sl] + log_sig
        log2_f = (jnp.maximum(a, b) + jnp.log(1.0 + jnp.exp(-jnp.abs(a - b)))) * _LOG2_E
        k = oml_ref[:, sl] * _sigmoid(-f)
        q = q_ref[:, sl] * (hd ** -0.5)
        v = i_ref[:, sl]
        if row_ok is not None:
            log2_f = jnp.where(row_ok, log2_f, 0.0)
            k = jnp.where(row_ok, k, 0.0)
            q = jnp.where(row_ok, q, 0.0)
            v = jnp.where(row_ok, v, 0.0)
        return log2_f, q, k, v

    def level_att(sums, q, k):
        att = jnp.zeros((ch, ch), F32)
        for i, mask in enumerate(level_masks):
            e = jnp.exp2(sums[(i + 1) * ch:(i + 2) * ch, :])
            att = att + jnp.where(mask, _dot_nt(q * e, k * e), 0.0)
        return att

    def lag_stack(sums, q, k):
        cum = sums[:ch, :]
        terms = [q * k]
        for lag in range(1, SUBLANES):
            terms.append(q * _roll_in_blocks(k, lag) * jnp.exp2(cum - _roll_in_blocks(cum, lag)))
        return jnp.concatenate(terms, axis=0).astype(BF16)

    def add_lags(att, row_sums):
        for lag in range(SUBLANES):
            att = att + jnp.where(lag_masks[lag], row_sums[lag * ch:(lag + 1) * ch, :ch], 0.0)
        return att

    for h0 in range(0, C_HEADS, _HGRN_HEAD_GROUP):
        heads = range(h0, h0 + _HGRN_HEAD_GROUP)
        gs = [gates(h) for h in heads]
        sums = [_dot_ones(sel_all, g[0], False) for g in gs]
        atts = [level_att(s, g[1], g[2]) for s, g in zip(sums, gs)]
        row_sums = [jnp.dot(lag_stack(s, g[1], g[2]), ones, preferred_element_type=F32) for s, g in zip(sums, gs)]
        atts = [add_lags(a, r) for a, r in zip(atts, row_sums)]
        for h, s, (_, q, k, v), att in zip(heads, sums, gs, atts):
            cum = s[:ch, :]
            c_last = cum[ch - 1:ch, :]
            st = st_ref[h]
            o = _dot_nt(q * jnp.exp2(cum), st) + _dot(att, v)
            st_ref[h] = st * jnp.exp2(c_last) + _dot_tn(v, k * jnp.exp2(c_last - cum))
            sl = slice(h * hd, (h + 1) * hd)
            o_ref[:, sl] = (_rms(o, nw_ref[...]) * _silu(g_ref[:, sl])).astype(o_ref.dtype)

    @pl.when(c == pl.num_programs(1) - 1)
    def _():
        for h in range(C_HEADS):
            sfin_ref[h] = st_ref[h].T


def _hgrn(proj, lb, s0, norm_w, t_valid):
    bsz, t, four_w = proj.shape
    width = four_w // 4
    ch = C_CHUNK
    nc = t // ch
    lb = lb.astype(F32).reshape(1, width)
    col = lambda j: pl.BlockSpec((None, ch, width), lambda b, c: (b, c, j))
    vec = pl.BlockSpec((1, width), lambda b, c: (0, 0))
    st_spec = pl.BlockSpec((None, C_HEADS, HEAD_DIM, HEAD_DIM), lambda b, c: (b, 0, 0, 0))
    return pl.pallas_call(
        functools.partial(_hgrn_kernel, t_valid=None if t_valid == t else t_valid),
        grid=(bsz, nc),
        in_specs=[col(0), col(1), col(2), col(3), vec, vec, vec, st_spec,
                  pl.BlockSpec((1, HEAD_DIM), lambda b, c: (0, 0))],
        out_specs=[pl.BlockSpec((None, ch, width), lambda b, c: (b, c, 0)), st_spec],
        out_shape=[jax.ShapeDtypeStruct((bsz, t, width), BF16),
                   jax.ShapeDtypeStruct((bsz, C_HEADS, HEAD_DIM, HEAD_DIM), F32)],
        scratch_shapes=[pltpu.VMEM((C_HEADS, HEAD_DIM, HEAD_DIM), F32)],
        compiler_params=_cparams("parallel", "arbitrary"),
        name="hgrn2",
    )(proj, proj, proj, proj, jnp.log(lb), jnp.log1p(-lb), 1.0 - lb, s0, norm_w)


def _pad_time(x, multiple):
    t = x.shape[1]
    t_pad = -(-t // multiple) * multiple
    return x if t_pad == t else jnp.pad(x, ((0, 0), (0, t_pad - t), (0, 0)))


def _ab_mixer(proj, zgb, t_valid, rope_tabs, kv_past, s0, conv_buf, j, conv_w, a_log, dt_bias, norm_w):
    bsz, t, _ = proj.shape
    a_dim = A_HEADS * HEAD_DIM
    b_dim = B_HEADS * HEAD_DIM

    q_r, k_r = _rope(proj, *rope_tabs)
    if kv_past is None:
        o_a = _dilated_attention(q_r, k_r, proj)
        keep = min(A_GROUPS[-1][0], t_valid)
        k_rows = k_r[:, t_valid - keep:t_valid]
        v_rows = proj[:, t_valid - keep:t_valid, 2 * a_dim:3 * a_dim]
    else:
        o_a = _cache_attention(q_r, k_r, proj, kv_past[0], kv_past[1], j, t_valid)
        k_rows = k_r[:, :t_valid]
        v_rows = proj[:, :t_valid, 2 * a_dim:3 * a_dim]
    k_rows = k_rows.reshape(bsz, -1, A_HEADS, HEAD_DIM)
    v_rows = v_rows.reshape(bsz, -1, A_HEADS, HEAD_DIM)

    raw_tail = proj[:, max(t_valid - (B_CONV - 1), 0):t_valid, 3 * a_dim:]
    buf_new = jnp.concatenate([conv_buf, raw_tail], axis=1)[:, -(B_CONV - 1):]
    buf8 = jnp.pad(conv_buf, ((0, 0), (SUBLANES - (B_CONV - 1), 0), (0, 0)))
    proj_c, zgb_c = _pad_time(proj, B_CHUNK), _pad_time(zgb, B_CHUNK)
    nc = proj_c.shape[1] // B_CHUNK
    gbt = zgb_c[:, :, b_dim:b_dim + 2 * SUBLANES].reshape(bsz, nc, B_CHUNK, 2 * SUBLANES).swapaxes(2, 3)
    o_b, s_new = _gdn(proj_c, buf8, conv_w, zgb_c, gbt, a_log, dt_bias, s0, norm_w.reshape(1, HEAD_DIM), t_valid)

    mixed = [o_a.reshape(bsz * t, a_dim), o_b[:, :t].reshape(bsz * t, b_dim)]
    return mixed, (k_rows, v_rows, s_new, buf_new)


def _hgrn_mixer(proj, t_valid, s0, lb, norm_w):
    bsz, t, n_proj = proj.shape
    o, s_new = _hgrn(_pad_time(proj, C_CHUNK), lb, s0, norm_w.reshape(1, HEAD_DIM), t_valid)
    return [o[:, :t].reshape(bsz * t, n_proj // 4)], s_new


def _trunk(groups, mods5, p):
    main, rider = groups
    d = main["x"].shape[2]
    depth = p["norm_pre"].shape[0]
    tm = 1024
    a_dim = A_HEADS * HEAD_DIM
    b_dim = B_HEADS * HEAD_DIM
    d_ff = p["ffn_wo"].shape[2]
    wi2d = p["ffn_wi"].reshape(-1, 2 * d_ff)
    wo2d = p["ffn_wo"].reshape(-1, d)
    ab_out2d = p["ab_w_out"].reshape(-1, d)
    c_in2d = p["c_w_in"].reshape(-1, p["c_w_in"].shape[2])
    c_out2d = p["c_w_out"].reshape(-1, d)
    norm_pre = p["norm_pre"].reshape(depth, 3, 1, d)
    norm_post = p["norm_post"].reshape(depth, 3, 1, d)
    lb_all = jnp.cumsum(jax.nn.softmax(p["c_lower_bounds"].astype(F32), axis=0), axis=0)
    lb_all = lb_all - lb_all[0:1]

    def rows2d(g, a):
        return a.reshape(g["x"].shape[0] * g["x"].shape[1], a.shape[-1])

    def rows3d(g, a):
        return a.reshape(g["x"].shape[0], g["x"].shape[1], a.shape[-1])

    def project(w2d, k_block, n, tn, w_t=None):
        outs = _mm([rows2d(main, main["h"])], w2d, k_block=k_block, n0=0, n=n, tn=tn, tm=tm,
                   rider=[rows2d(rider, rider["h"])], w_t=w_t)
        return [rows3d(g, o) for g, o in zip(groups, outs)]

    for g in groups:
        g["rope"] = _rope_tables(g["pos"])
        g["states"] = ([], [], [], [], [])
        (g["h"],) = _norm_call(g["x"], mods5, g["boff"], pre=(norm_pre[0, 0], 0, 0, 1))
    for layer in range(depth):
        j = layer // 2
        for sub in range(3):
            if sub == 1:
                if layer % 2 == 0:
                    w_zgb_t = p["w_zgb_t"][j]
                    projs = project(None, j, 3 * a_dim + 3 * b_dim, 1024, w_t=p["ab_in_t"])
                    zgbs = project(None, 0, w_zgb_t.shape[1], w_zgb_t.shape[1], w_t=w_zgb_t)
                    for g, proj, zgb in zip(groups, projs, zgbs):
                        bsz, past = g["x"].shape[0], g["past"]
                        if past is None:
                            kv_past = None
                            s0 = jnp.zeros((bsz, B_HEADS, HEAD_DIM, HEAD_DIM), F32)
                            buf = jnp.zeros((bsz, B_CONV - 1, 3 * b_dim), F32)
                        else:
                            kv_past, s0, buf = (past[0], past[1]), past[2][j], past[3][j]
                        g["mixed"], new = _ab_mixer(proj, zgb, g["t_valid"], g["rope"], kv_past, s0, buf, j,
                                                    p["b_conv_w"][j], p["b_a_log"][j], p["b_dt_bias"][j],
                                                    p["b_norm"][j])
                        for dst, val in zip(g["states"][:4], new):
                            dst.append(val)
                else:
                    projs = project(c_in2d, j, c_in2d.shape[1], 1024)
                    for g, proj in zip(groups, projs):
                        bsz, past = g["x"].shape[0], g["past"]
                        s0 = jnp.zeros((bsz, C_HEADS, HEAD_DIM, HEAD_DIM), F32) if past is None else past[4][j]
                        g["mixed"], s_new = _hgrn_mixer(proj, g["t_valid"], s0, lb_all[j], p["c_norm"][j])
                        g["states"][4].append(s_new)
                w_out2d = ab_out2d if layer % 2 == 0 else c_out2d
                post = (norm_post[layer, 1], layer, 5, 1.0)
                pre = (norm_pre[layer, 2], layer, 6, 7)
                for g in groups:
                    if g["x"].shape[1] >= _MM_NORM_MIN_ROWS:
                        g["x"], g["h"] = _mm_norm(g["mixed"], w_out2d, j, g["x"], mods5, g["boff"], post=post, pre=pre)
                    else:
                        y = _mm(g["mixed"], w_out2d, k_block=j, n0=0, n=d, tn=1024, tm=tm)
                        g["x"], g["h"] = _norm_call(g["x"], mods5, g["boff"], post=(rows3d(g, y),) + post, pre=pre)
                continue
            k_ffn = layer * 2 + (0 if sub == 0 else 1)
            acts = _swiglu_in(rows2d(main, main["h"]), wi2d, k_ffn, 2 * tm, rider=rows2d(rider, rider["h"]))
            ys = _mm([acts[0]], wo2d, k_block=k_ffn, n0=0, n=d, tn=512, tm=512, rider=[acts[1]])
            if sub == 0:
                pre = (norm_pre[layer, 1], layer, 3, 4)
            elif layer + 1 < depth:
                pre = (norm_pre[layer + 1, 0], layer + 1, 0, 1)
            else:
                pre = None
            for g, y in zip(groups, ys):
                post = (rows3d(g, y), norm_post[layer, sub], layer, 3 * sub + 2, FFN_RESIDUAL)
                outs = _norm_call(g["x"], mods5, g["boff"], post=post, pre=pre)
                g["x"] = outs[0]
                g["h"] = outs[1] if pre is not None else None
    return [(g["x"],) + tuple(jnp.stack(s) for s in g["states"]) for g in groups]


def kernel(x_prompt, x_sample, cache_a_k, cache_a_v, state_b_s, state_b_conv, state_c_s, c_prompt, c_sample,
           ada_w, ada_b, norm_pre, norm_post, ffn_wi, ffn_wo, ab_w_in, ab_w_out, b_conv_w, b_a_log, b_dt_bias,
           b_norm, c_w_in, c_w_out, c_lower_bounds, c_norm):
    p = dict(norm_pre=norm_pre, norm_post=norm_post, ffn_wi=ffn_wi, ffn_wo=ffn_wo, ab_w_in=ab_w_in,
             ab_w_out=ab_w_out, b_conv_w=b_conv_w, b_a_log=b_a_log, b_dt_bias=b_dt_bias, b_norm=b_norm,
             c_w_in=c_w_in, c_w_out=c_w_out, c_lower_bounds=c_lower_bounds, c_norm=c_norm)
    depth, d = norm_pre.shape[0], x_prompt.shape[2]
    main_cols = (3 * A_HEADS + 3 * B_HEADS) * HEAD_DIM
    gb_cols = 2 * B_HEADS
    ab_in_t = jnp.swapaxes(ab_w_in, 1, 2)
    p["ab_in_t"] = ab_in_t
    p["w_zgb_t"] = [jnp.concatenate([ab_in_t[j, main_cols + gb_cols:], ab_in_t[j, main_cols:main_cols + gb_cols],
                                     jnp.zeros((LANES - gb_cols, d), F32)], axis=0)[None]
                    for j in range(ab_w_in.shape[0])]
    n_p, t_p = x_prompt.shape[0], x_prompt.shape[1]
    n_s, t_s = x_sample.shape[0], x_sample.shape[1]
    past_len = 16384

    rows = -(-(n_p + n_s) // (2 * SUBLANES)) * (2 * SUBLANES)
    c_all = jnp.concatenate([c_prompt, c_sample, jnp.zeros((rows - n_p - n_s, d), F32)], axis=0)
    mods5 = _ada_mods(c_all, ada_w, ada_b).reshape(depth, rows, N_MOD, 1, d)

    prompt = dict(x=x_prompt, t_valid=t_p, pos=jnp.arange(t_p, dtype=jnp.int32), boff=0, past=None)
    sample = dict(x=jnp.pad(x_sample, ((0, 0), (0, SAMPLE_T_PAD - t_s), (0, 0))), t_valid=t_s,
                  pos=past_len + jnp.arange(SAMPLE_T_PAD, dtype=jnp.int32), boff=n_p,
                  past=(cache_a_k, cache_a_v, state_b_s, state_b_conv, state_c_s))
    (y_p, ak_p, av_p, bs_p, bc_p, cs_p), (y_s, ak_s, av_s, bs_s, bc_s, cs_s) = _trunk((prompt, sample), mods5, p)
    return (y_p, y_s[:, :t_s], ak_p, av_p, bs_p, bc_p, cs_p, ak_s, av_s, bs_s, bc_s, cs_s)
```

```python
import functools
import math

import jax
import jax.numpy as jnp
from jax import lax
from jax.experimental import pallas as pl
from jax.experimental.pallas import tpu as pltpu

F32 = jnp.float32
BF16 = jnp.bfloat16
HIGHEST = lax.Precision.HIGHEST

LANES = 128
SUBLANES = 8
VMEM_LIMIT_BYTES = 56 * 1024 * 1024

_LOG2_E = 1.4426950408889634
NORM_EPS = 1e-6
FFN_RESIDUAL = 0.5
N_MOD = 9
ROPE_THETA = 10000.0
A_HEADS = 8
A_GROUPS = ((128, 1), (512, 4), (2048, 16))
A_BLOCK = 128
B_HEADS = 8
B_CONV = 4
B_CHUNK = 64
C_HEADS = 16
HEAD_DIM = 128
SAMPLE_T_PAD = 16
C_CHUNK = 64
_MM_NORM_MIN_ROWS = 512
_HGRN_HEAD_GROUP = 16


def _cparams(*sem):
    return pltpu.CompilerParams(dimension_semantics=sem, vmem_limit_bytes=VMEM_LIMIT_BYTES)


def _sigmoid(x):
    return 1.0 / (1.0 + jnp.exp(-x))


def _silu(x):
    return x * _sigmoid(x)


def _softplus(x):
    return jnp.maximum(x, 0.0) + jnp.log1p(jnp.exp(-jnp.abs(x)))


def _rms(x, g):
    return x * lax.rsqrt(jnp.mean(x * x, axis=-1, keepdims=True) + NORM_EPS) * g


def _dot(a, b):
    return jnp.dot(a.astype(BF16), b.astype(BF16), preferred_element_type=F32)


def _dot_nt(a, b):
    return lax.dot_general(a.astype(BF16), b.astype(BF16), (((1,), (1,)), ((), ())),
                           preferred_element_type=F32)


def _dot_tn(a, b):
    return lax.dot_general(a.astype(BF16), b.astype(BF16), (((0,), (0,)), ((), ())),
                           preferred_element_type=F32)


def _split2(x):
    hi = x.astype(BF16)
    lo = (x - hi.astype(F32)).astype(BF16)
    return hi, lo


def _dot_x3(a_parts, b_parts):
    (ah, al), (bh, bl) = a_parts, b_parts
    dot = functools.partial(jnp.dot, preferred_element_type=F32)
    return dot(ah, bh) + (dot(ah, bl) + dot(al, bh))


def _dot_ones(ones_bf16, x, ones_on_right):
    hi = x.astype(BF16)
    r1 = x - hi.astype(F32)
    mid = r1.astype(BF16)
    lo = (r1 - mid.astype(F32)).astype(BF16)
    dot = functools.partial(jnp.dot, preferred_element_type=F32)
    if ones_on_right:
        return dot(hi, ones_bf16) + (dot(mid, ones_bf16) + dot(lo, ones_bf16))
    return dot(ones_bf16, hi) + (dot(ones_bf16, mid) + dot(ones_bf16, lo))


def _roll_in_blocks(x, lag):
    rows, cols = x.shape
    x3 = x.reshape(rows // SUBLANES, SUBLANES, cols)
    return pltpu.roll(x3, lag, 1).reshape(rows, cols)


def _iota(shape, dim):
    return lax.broadcasted_iota(jnp.int32, shape, dim)


def _ada_kernel(c_ref, w_ref, b_ref, o_ref):
    a = _silu(c_ref[...])
    o_ref[...] = _dot(a, w_ref[...]) + b_ref[...]


def _ada_mods(c_all, ada_w, ada_b):
    depth, d, n = ada_w.shape
    rows = c_all.shape[0]
    tn = 1024
    return pl.pallas_call(
        _ada_kernel,
        grid=(depth, n // tn),
        in_specs=[pl.BlockSpec((rows, d), lambda l, j: (0, 0)),
                  pl.BlockSpec((None, d, tn), lambda l, j: (l, 0, j)),
                  pl.BlockSpec((None, 1, tn), lambda l, j: (l, 0, j))],
        out_specs=pl.BlockSpec((None, rows, tn), lambda l, j: (l, 0, j)),
        out_shape=jax.ShapeDtypeStruct((depth, rows, n), F32),
        compiler_params=_cparams("parallel", "parallel"),
        name="ada_mods",
    )(c_all, ada_w, ada_b.reshape(depth, 1, n))


def _norm_kernel(*refs, has_post, has_pre, coef):
    it = iter(refs)
    x_ref = next(it)
    if has_post:
        y_ref, gpost_ref, gate_ref = next(it), next(it), next(it)
    if has_pre:
        gpre_ref, shift_ref, scale_ref = next(it), next(it), next(it)
    x = x_ref[...]
    if has_post:
        xo_ref = next(it)
        x = x + (coef * gate_ref[...]) * _rms(y_ref[...], gpost_ref[...])
        xo_ref[...] = x
    if has_pre:
        h_ref = next(it)
        h = _rms(x, gpre_ref[...]) * (1.0 + scale_ref[...]) + shift_ref[...]
        h_ref[...] = h.astype(BF16)


def _norm_call(x, mods5, boff, post=None, pre=None):
    bsz, t, d = x.shape
    tt = min(t, 512)
    row = pl.BlockSpec((None, tt, d), lambda b, i: (b, i, 0))
    vec = pl.BlockSpec((1, d), lambda b, i: (0, 0))

    def mod_spec(layer, k):
        return pl.BlockSpec((None, None, None, 1, d), lambda b, i: (layer, boff + b, k, 0, 0))

    args, in_specs, out_shape, out_specs = [x], [row], [], []
    coef = 1.0
    if post is not None:
        y, g_post, layer, gate_idx, coef = post
        args += [y, g_post, mods5]
        in_specs += [row, vec, mod_spec(layer, gate_idx)]
        out_shape.append(jax.ShapeDtypeStruct(x.shape, F32))
        out_specs.append(row)
    if pre is not None:
        g_pre, layer, shift_idx, scale_idx = pre
        args += [g_pre, mods5, mods5]
        in_specs += [vec, mod_spec(layer, shift_idx), mod_spec(layer, scale_idx)]
        out_shape.append(jax.ShapeDtypeStruct(x.shape, BF16))
        out_specs.append(row)
    outs = pl.pallas_call(
        functools.partial(_norm_kernel, has_post=post is not None, has_pre=pre is not None, coef=coef),
        grid=(bsz, t // tt),
        in_specs=in_specs, out_specs=out_specs, out_shape=out_shape,
        compiler_params=_cparams("parallel", "parallel"),
        name="sandwich_norm",
    )(*args)
    return outs


def _mm_kernel(*refs, k_sizes, has_rider, w_rows_are_outputs):
    n_x = len(k_sizes)
    x_refs = refs[:n_x]
    r_refs = refs[n_x:2 * n_x] if has_rider else ()
    rest = refs[n_x + len(r_refs):]
    w_ref, o_ref = rest[0], rest[1]
    ro_ref = rest[2] if has_rider else None
    wbf_ref = rest[-1]

    def product(parts):
        acc, off = None, 0
        for x_ref, ks in zip(parts, k_sizes):
            if w_rows_are_outputs:
                part = _dot_nt(x_ref[...], wbf_ref[:, off:off + ks])
            else:
                part = jnp.dot(x_ref[...].astype(BF16), wbf_ref[off:off + ks, :], preferred_element_type=F32)
            acc = part if acc is None else acc + part
            off += ks
        return acc

    @pl.when(pl.program_id(1) == 0)
    def _():
        wbf_ref[...] = w_ref[...].astype(BF16)
        if has_rider:
            ro_ref[...] = product(r_refs).astype(ro_ref.dtype)

    o_ref[...] = product(x_refs).astype(o_ref.dtype)


def _mm(xs, w2d, *, k_block, n0, n, tn, tm, out_dtype=F32, rider=None, w_t=None):
    m = xs[0].shape[0]
    k_sizes = tuple(x.shape[1] for x in xs)
    k = sum(k_sizes)
    tm = min(tm, m)
    assert m % tm == 0 and n % tn == 0 and n0 % tn == 0
    assert (w2d.shape[0] % k == 0) if w_t is None else (w_t.shape[2] == k)
    nb0 = n0 // tn
    in_specs = [pl.BlockSpec((tm, ks), lambda j, i: (i, 0)) for ks in k_sizes]
    out_specs = [pl.BlockSpec((tm, tn), lambda j, i: (i, j))]
    out_shape = [jax.ShapeDtypeStruct((m, n), out_dtype)]
    args = list(xs)
    if rider is not None:
        m2 = rider[0].shape[0]
        assert tuple(x.shape[1] for x in rider) == k_sizes
        in_specs += [pl.BlockSpec((m2, ks), lambda j, i: (0, 0)) for ks in k_sizes]
        out_specs.append(pl.BlockSpec((m2, tn), lambda j, i: (0, j)))
        out_shape.append(jax.ShapeDtypeStruct((m2, n), out_dtype))
        args += list(rider)
    if w_t is None:
        in_specs.append(pl.BlockSpec((k, tn), lambda j, i: (k_block, nb0 + j)))
        w_tile = (k, tn)
    else:
        in_specs.append(pl.BlockSpec((None, tn, k), lambda j, i: (k_block, nb0 + j, 0)))
        w_tile = (tn, k)
    outs = pl.pallas_call(
        functools.partial(_mm_kernel, k_sizes=k_sizes, has_rider=rider is not None,
                          w_rows_are_outputs=w_t is not None),
        grid=(n // tn, m // tm),
        in_specs=in_specs,
        out_specs=out_specs,
        out_shape=out_shape,
        scratch_shapes=[pltpu.VMEM(w_tile, BF16)],
        compiler_params=_cparams("parallel", "arbitrary"),
        name="matmul",
    )(*args, w2d if w_t is None else w_t)
    return outs[0] if rider is None else (outs[0], outs[1])


def _mm_norm_kernel(*refs, k_sizes, coef):
    n_x = len(k_sizes)
    x_refs = refs[:n_x]
    (w_ref, xres_ref, gpost_ref, gate_ref, gpre_ref, shift_ref, scale_ref,
     xo_ref, h_ref, wbf_ref) = refs[n_x:]

    @pl.when(pl.program_id(0) == 0)
    def _():
        wbf_ref[...] = w_ref[...].astype(BF16)

    rows = xres_ref.shape[0]
    n_split = 4 if rows % (4 * 2 * SUBLANES) == 0 else 1
    step = rows // n_split
    for r in range(n_split):
        rs = slice(r * step, (r + 1) * step)
        y, off = None, 0
        for x_ref, ks in zip(x_refs, k_sizes):
            part = jnp.dot(x_ref[rs, :].astype(BF16), wbf_ref[off:off + ks, :], preferred_element_type=F32)
            y = part if y is None else y + part
            off += ks
        x = xres_ref[rs, :] + (coef * gate_ref[...]) * _rms(y, gpost_ref[...])
        xo_ref[rs, :] = x
        h_ref[rs, :] = (_rms(x, gpre_ref[...]) * (1.0 + scale_ref[...]) + shift_ref[...]).astype(BF16)


def _mm_norm(xs, w2d, k_block, x_res, mods5, boff, post, pre):
    bsz, t, d = x_res.shape
    m = bsz * t
    k_sizes = tuple(x.shape[1] for x in xs)
    k = sum(k_sizes)
    tm = min(t, 512)
    assert t % tm == 0 and w2d.shape == (w2d.shape[0] // k * k, d)
    g_post, layer_post, gate_idx, coef = post
    g_pre, layer_pre, shift_idx, scale_idx = pre
    row = pl.BlockSpec((tm, d), lambda i: (i, 0))
    vec = pl.BlockSpec((1, d), lambda i: (0, 0))

    def mod_spec(layer, idx):
        return pl.BlockSpec((None, None, None, 1, d), lambda i: (layer, boff + (i * tm) // t, idx, 0, 0))

    in_specs = [pl.BlockSpec((tm, ks), lambda i: (i, 0)) for ks in k_sizes]
    in_specs += [pl.BlockSpec((k, d), lambda i: (k_block, 0), pipeline_mode=pl.Buffered(1)),
                 row, vec, mod_spec(layer_post, gate_idx), vec, mod_spec(layer_pre, shift_idx),
                 mod_spec(layer_pre, scale_idx)]
    x_new, h = pl.pallas_call(
        functools.partial(_mm_norm_kernel, k_sizes=k_sizes, coef=coef),
        grid=(m // tm,),
        in_specs=in_specs,
        out_specs=[row, row],
        out_shape=[jax.ShapeDtypeStruct((m, d), F32), jax.ShapeDtypeStruct((m, d), BF16)],
        scratch_shapes=[pltpu.VMEM((k, d), BF16)],
        compiler_params=_cparams("arbitrary"),
        name="matmul_norm",
    )(*xs, w2d, x_res.reshape(m, d), g_post, mods5, g_pre, mods5, mods5)
    return x_new.reshape(bsz, t, d), h.reshape(bsz, t, d)


_SWIGLU_GROUP = 4


def _swiglu_kernel(*refs, has_rider):
    ng = _SWIGLU_GROUP
    n_x = 2 if has_rider else 1
    x_ref, w_refs = refs[0], refs[n_x:n_x + 2 * ng]
    o_ref, wbf_ref = refs[n_x + 2 * ng], refs[-1]

    def gated(x, dst_ref):
        for g in range(ng):
            r = jnp.dot(x, wbf_ref[:, 2 * g * LANES:(2 * g + 2) * LANES], preferred_element_type=F32)
            u, v = r[:, :LANES], r[:, LANES:]
            dst_ref[:, g * LANES:(g + 1) * LANES] = (_silu(v) * u).astype(dst_ref.dtype)

    @pl.when(pl.program_id(1) == 0)
    def _():
        for g in range(ng):
            wbf_ref[:, (2 * g) * LANES:(2 * g + 1) * LANES] = w_refs[g][...].astype(BF16)
            wbf_ref[:, (2 * g + 1) * LANES:(2 * g + 2) * LANES] = w_refs[ng + g][...].astype(BF16)
        if has_rider:
            gated(refs[1][...], refs[n_x + 2 * ng + 1])

    gated(x_ref[...], o_ref)


def _swiglu_in(h, wi2d, k_block, tm, rider=None):
    m, k = h.shape
    d_ff = wi2d.shape[1] // 2
    assert d_ff % LANES == 0
    nblk = d_ff // LANES
    ng = _SWIGLU_GROUP
    tn = ng * LANES
    tm = min(tm, m)
    last = nblk - 1
    w_specs = [pl.BlockSpec((k, LANES), lambda j, i, g=g, base=base: (k_block, base + jnp.minimum(ng * j + g, last)))
               for base in (0, nblk) for g in range(ng)]
    x_specs = [pl.BlockSpec((tm, k), lambda j, i: (i, 0))]
    out_specs = [pl.BlockSpec((tm, tn), lambda j, i: (i, j))]
    out_shape = [jax.ShapeDtypeStruct((m, d_ff), BF16)]
    args = [h]
    if rider is not None:
        m2 = rider.shape[0]
        x_specs.append(pl.BlockSpec((m2, k), lambda j, i: (0, 0)))
        out_specs.append(pl.BlockSpec((m2, tn), lambda j, i: (0, j)))
        out_shape.append(jax.ShapeDtypeStruct((m2, d_ff), BF16))
        args.append(rider)
    outs = pl.pallas_call(
        functools.partial(_swiglu_kernel, has_rider=rider is not None),
        grid=(pl.cdiv(nblk, ng), m // tm),
        in_specs=x_specs + w_specs,
        out_specs=out_specs,
        out_shape=out_shape,
        scratch_shapes=[pltpu.VMEM((k, 2 * tn), BF16)],
        compiler_params=_cparams("parallel", "arbitrary"),
        name="swiglu_in",
    )(*args, *([wi2d] * (2 * ng)))
    return outs[0] if rider is None else (outs[0], outs[1])


def _rope_kernel(x_ref, cos_ref, sin_ref, q_ref, k_ref):
    cos, sin = cos_ref[...], sin_ref[...]
    for h in range(2 * A_HEADS):
        xh = x_ref[:, h * HEAD_DIM:(h + 1) * HEAD_DIM]
        r = xh * cos + pltpu.roll(xh, HEAD_DIM // 2, 1) * sin
        dst = q_ref if h < A_HEADS else k_ref
        hh = h % A_HEADS
        dst[:, hh * HEAD_DIM:(hh + 1) * HEAD_DIM] = r


def _rope(proj, cos_t, sin_t):
    bsz, t, _ = proj.shape
    a_dim = A_HEADS * HEAD_DIM
    tt = min(t, 256)
    out = jax.ShapeDtypeStruct((bsz, t, a_dim), F32)
    o_spec = pl.BlockSpec((None, tt, a_dim), lambda b, i: (b, i, 0))
    return pl.pallas_call(
        _rope_kernel,
        grid=(bsz, t // tt),
        in_specs=[pl.BlockSpec((None, tt, 2 * a_dim), lambda b, i: (b, i, 0)),
                  pl.BlockSpec((tt, HEAD_DIM), lambda b, i: (i, 0)),
                  pl.BlockSpec((tt, HEAD_DIM), lambda b, i: (i, 0))],
        out_specs=[o_spec, o_spec], out_shape=[out, out],
        compiler_params=_cparams("parallel", "parallel"),
        name="rope",
    )(proj, cos_t, sin_t)


def _rope_tables(pos):
    half = HEAD_DIM // 2
    inv_freq = jnp.power(ROPE_THETA, -jnp.arange(half, dtype=F32) / half)
    ang = pos.astype(F32)[:, None] * inv_freq[None, :]
    cos, sin = jnp.cos(ang), jnp.sin(ang)
    return jnp.concatenate([cos, cos], axis=1), jnp.concatenate([-sin, sin], axis=1)


_A_TOKEN_BLOCK = A_BLOCK * max(dil for _, dil in A_GROUPS)
_A_PRE_STRIDE = 4
_A_STAGED_OPERANDS = 8


def _dilated_attn_kernel(q_ref, kp_ref, kc_ref, vp_ref, vc_ref, o_ref, m_ref, l_ref, acc_ref, stage_ref):
    tb = pl.program_id(2)
    blk = A_BLOCK
    tokens = _A_TOKEN_BLOCK
    qi = _iota((blk, 2 * blk), 0)
    ki = _iota((blk, 2 * blk), 1)
    rel = qi + blk - ki
    has_prev = (ki >= blk) | (tb > 0)
    scale = HEAD_DIM ** -0.5
    last = len(A_GROUPS) - 1

    for g, (window, dil) in enumerate(A_GROUPS):
        span = window // dil
        band = (rel >= 0) & (rel <= span)
        reach = blk * dil
        staged = {}
        if dil % SUBLANES == 0:
            assert reach == tokens and dil % _A_PRE_STRIDE == 0
            part = tokens // _A_PRE_STRIDE
            for slab, ref in enumerate((q_ref, kp_ref, kc_ref, vp_ref, vc_ref, m_ref, l_ref, acc_ref)):
                staged[id(ref)] = slab
                for r0 in range(_A_PRE_STRIDE):
                    stage_ref[slab, r0 * part:(r0 + 1) * part, :] = ref[pl.ds(r0, part, stride=_A_PRE_STRIDE), :]

        def rows(ref, base, r, dil=dil, staged=staged):
            if dil == 1:
                return ref[base:base + blk, :]
            if id(ref) in staged:
                start = (r % _A_PRE_STRIDE) * (tokens // _A_PRE_STRIDE) + r // _A_PRE_STRIDE
                return stage_ref[staged[id(ref)], pl.ds(start, blk, stride=dil // _A_PRE_STRIDE), :]
            return ref[pl.ds(base + r, blk, stride=dil), :]

        def put(ref, base, r, val, dil=dil):
            if dil == 1:
                ref[base:base + blk, :] = val
            else:
                ref[pl.ds(base + r, blk, stride=dil), :] = val

        for r in range(dil):
            k_prev = rows(kp_ref, tokens - reach, r).astype(BF16)
            v_prev = rows(vp_ref, tokens - reach, r).astype(BF16)
            for s in range(tokens // reach):
                base = s * reach
                q = rows(q_ref, base, r)
                k_cur = rows(kc_ref, base, r).astype(BF16)
                v_cur = rows(vc_ref, base, r).astype(BF16)
                kcat = jnp.concatenate([k_prev, k_cur], axis=0)
                vcat = jnp.concatenate([v_prev, v_cur], axis=0)
                k_prev, v_prev = k_cur, v_cur
                valid = band & has_prev if s == 0 else band
                sc = jnp.where(valid, _dot_nt(q, kcat) * scale, -jnp.inf)
                m_loc = jnp.max(sc, axis=-1, keepdims=True)
                if g == 0:
                    p = jnp.exp(sc - m_loc)
                    put(m_ref, base, r, jnp.broadcast_to(m_loc, (blk, HEAD_DIM)))
                    put(l_ref, base, r, jnp.broadcast_to(jnp.sum(p, axis=-1, keepdims=True), (blk, HEAD_DIM)))
                    put(acc_ref, base, r, _dot(p, vcat))
                else:
                    m_old = rows(m_ref, base, r)
                    m_new = jnp.maximum(m_old, m_loc)
                    alpha = jnp.exp(m_old - m_new)
                    p = jnp.exp(sc - m_new[:, 0:1])
                    l_new = alpha * rows(l_ref, base, r) + jnp.sum(p, axis=-1, keepdims=True)
                    acc_new = alpha * rows(acc_ref, base, r) + _dot(p, vcat)
                    if g < last:
                        put(m_ref, base, r, m_new)
                        put(l_ref, base, r, l_new)
                        put(acc_ref, base, r, acc_new)
                    else:
                        put(acc_ref, base, r, acc_new / l_new)

    o_ref[...] = acc_ref[...].astype(o_ref.dtype)


def _dilated_attention(q_r, k_r, proj):
    bsz, t, a_dim = q_r.shape
    tokens = _A_TOKEN_BLOCK
    assert t % tokens == 0 and all(window // dil <= A_BLOCK for window, dil in A_GROUPS)
    blk = (None, tokens, HEAD_DIM)
    v_off = 2 * a_dim // HEAD_DIM
    cur = lambda b, h, i: (b, i, h)
    prev = lambda b, h, i: (b, jnp.maximum(i - 1, 0), h)
    v_cur = lambda b, h, i: (b, i, v_off + h)
    v_prev = lambda b, h, i: (b, jnp.maximum(i - 1, 0), v_off + h)
    return pl.pallas_call(
        _dilated_attn_kernel,
        grid=(bsz, A_HEADS, t // tokens),
        in_specs=[pl.BlockSpec(blk, cur), pl.BlockSpec(blk, prev), pl.BlockSpec(blk, cur),
                  pl.BlockSpec(blk, v_prev), pl.BlockSpec(blk, v_cur)],
        out_specs=pl.BlockSpec(blk, cur),
        out_shape=jax.ShapeDtypeStruct((bsz, t, a_dim), BF16),
        scratch_shapes=[pltpu.VMEM((tokens, HEAD_DIM), F32)] * 3
                       + [pltpu.VMEM((_A_STAGED_OPERANDS, tokens, HEAD_DIM), F32)],
        compiler_params=_cparams("parallel", "parallel", "parallel"),
        name="dilated_attention",
    )(q_r, k_r, k_r, proj, proj)


def _group_count(d):
    cnt = jnp.zeros(d.shape, F32)
    for window, dil in A_GROUPS:
        hit = (d >= 0) & (d % dil == 0) & (d <= window)
        cnt = cnt + jnp.where(hit, 1.0, 0.0)
    return cnt


def _cache_attn_kernel(q_ref, kn_ref, vn_ref, kc_ref, vc_ref, o_ref, *, n_buf):
    tq = SUBLANES
    nh = A_HEADS
    scale = HEAD_DIM ** -0.5
    heads_of = lambda ref: jnp.concatenate([ref[0:tq, h * HEAD_DIM:(h + 1) * HEAD_DIM] for h in range(nh)], axis=0)
    q, k_new, v_new = heads_of(q_ref), heads_of(kn_ref), heads_of(vn_ref)
    k_c = kc_ref[...].reshape(n_buf * nh, HEAD_DIM)
    v_c = vc_ref[...].reshape(n_buf * nh, HEAD_DIM)

    r_c = _iota((nh * tq, n_buf * nh), 0)
    c_c = _iota((nh * tq, n_buf * nh), 1)
    same_c = (c_c % nh) == (r_c // tq)
    cnt_c = jnp.where(same_c, _group_count(n_buf + (r_c % tq) - (c_c // nh)), 0.0)
    r_n = _iota((nh * tq, nh * tq), 0)
    c_n = _iota((nh * tq, nh * tq), 1)
    same_n = (c_n // tq) == (r_n // tq)
    cnt_n = jnp.where(same_n, _group_count((r_n % tq) - (c_n % tq)), 0.0)

    s_c = jnp.where(cnt_c > 0, _dot_nt(q, k_c) * scale, -jnp.inf)
    s_n = jnp.where(cnt_n > 0, _dot_nt(q, k_new) * scale, -jnp.inf)
    m = jnp.maximum(jnp.max(s_c, axis=-1, keepdims=True), jnp.max(s_n, axis=-1, keepdims=True))
    p_c = cnt_c * jnp.exp(s_c - m)
    p_n = cnt_n * jnp.exp(s_n - m)
    den = jnp.sum(p_c, axis=-1, keepdims=True) + jnp.sum(p_n, axis=-1, keepdims=True)
    o = (_dot(p_c, v_c) + _dot(p_n, v_new)) / den
    o_rows = jnp.concatenate([o[h * tq:(h + 1) * tq, :] for h in range(nh)], axis=1)
    pad = jnp.zeros((o_ref.shape[0] - tq, nh * HEAD_DIM), F32)
    o_ref[...] = jnp.concatenate([o_rows, pad], axis=0).astype(o_ref.dtype)


def _cache_attention(q_r, k_r, proj, cache_k, cache_v, layer, t_valid):
    bsz, tp, a_dim = q_r.shape
    n_buf = cache_k.shape[2]
    assert t_valid <= SUBLANES <= tp
    new = pl.BlockSpec((None, tp, a_dim), lambda b: (b, 0, 0))
    v_new = pl.BlockSpec((None, tp, a_dim), lambda b: (b, 0, 2))
    cache = pl.BlockSpec((None, None, n_buf, A_HEADS, HEAD_DIM), lambda b: (layer, b, 0, 0, 0))
    return pl.pallas_call(
        functools.partial(_cache_attn_kernel, n_buf=n_buf),
        grid=(bsz,),
        in_specs=[new, new, v_new, cache, cache],
        out_specs=new,
        out_shape=jax.ShapeDtypeStruct((bsz, tp, a_dim), BF16),
        compiler_params=_cparams("parallel"),
        name="cache_attention",
    )(q_r, k_r, proj, cache_k, cache_v)


def _gdn_kernel(raw_ref, prev_ref, buf_ref, cw_ref, gb_ref, gbt_ref, alr_ref, dtr_ref, alc_ref, dtc_ref,
                s0_ref, z_ref, nw_ref, o_ref, sfin_ref, halo_ref, s_ref, *, t_valid):
    c = pl.program_id(1)
    ch = B_CHUNK
    hd = HEAD_DIM
    b_dim = B_HEADS * hd

    halo_ref[0:SUBLANES, :] = jnp.where(c == 0, buf_ref[...], prev_ref[...])
    halo_ref[SUBLANES:SUBLANES + ch, :] = raw_ref[...]
    y = raw_ref[...] * cw_ref[B_CONV - 1:B_CONV, :]
    for j in range(B_CONV - 1):
        lag = B_CONV - 1 - j
        y = y + halo_ref[SUBLANES - lag:SUBLANES - lag + ch, :] * cw_ref[j:j + 1, :]
    act = _silu(y)

    row_ok = (c * ch + _iota((ch, 1), 0)) < t_valid
    col_ok = (c * ch + _iota((1, ch), 1)) < t_valid

    gb = gb_ref[...]
    g_col = jnp.where(row_ok, -jnp.exp(alr_ref[...]) * _softplus(gb + dtr_ref[...]), 0.0)
    beta_col = jnp.where(row_ok, _sigmoid(gb), 0.0)
    gbt = gbt_ref[...]
    g_row = jnp.where(col_ok, -jnp.exp(alc_ref[:, :ch]) * _softplus(gbt + dtc_ref[:, :ch]), 0.0)

    ri = _iota((ch, ch), 0)
    ci = _iota((ch, ch), 1)
    tri = ri >= ci
    strict = ri > ci
    eye = jnp.where(ri == ci, 1.0, 0.0)
    cum_col = _dot_ones(jnp.where(tri, 1.0, 0.0).astype(BF16), g_col, False)
    cum_row = _dot_ones(jnp.where(ri <= ci, 1.0, 0.0).astype(BF16), g_row, True)

    heads = range(B_HEADS)
    qs, ks, vs, ccs, betas, decays, lows = [], [], [], [], [], [], []
    for h in heads:
        q = act[:, h * hd:(h + 1) * hd]
        k = act[:, b_dim + h * hd:b_dim + (h + 1) * hd]
        v = act[:, 2 * b_dim + h * hd:2 * b_dim + (h + 1) * hd]
        q = q * lax.rsqrt(jnp.sum(q * q, axis=-1, keepdims=True) + 1e-6) * (hd ** -0.5)
        k = k * lax.rsqrt(jnp.sum(k * k, axis=-1, keepdims=True) + 1e-6)
        qs.append(jnp.where(row_ok, q, 0.0))
        ks.append(jnp.where(row_ok, k, 0.0))
        vs.append(jnp.where(row_ok, v, 0.0))
        cc = cum_col[:, h:h + 1]
        cr = cum_row[h:h + 1, :]
        ccs.append(cc)
        betas.append(beta_col[:, B_HEADS + h:B_HEADS + h + 1])
        decays.append(jnp.where(tri, jnp.exp(jnp.where(tri, cc - cr, 0.0)), 0.0))
    qk_kk = [_dot_nt(jnp.concatenate([qs[h], ks[h]], axis=0), ks[h]) for h in heads]
    lows = [jnp.where(strict, betas[h] * qk_kk[h][ch:, :] * decays[h], 0.0) for h in heads]
    invs = [eye - lows[h] for h in heads]
    pw_parts = [_split2(lows[h]) for h in heads]
    pws = [_dot_x3(pw_parts[h], pw_parts[h]) for h in heads]
    size = 2
    while size < ch:
        pw_parts = [_split2(pws[h]) for h in heads]
        invs = [invs[h] + _dot_x3(_split2(invs[h]), pw_parts[h]) for h in heads]
        size *= 2
        if size < ch:
            pws = [_dot_x3(pw_parts[h], pw_parts[h]) for h in heads]
    rhs = [jnp.concatenate([vs[h] * betas[h], ks[h] * (betas[h] * jnp.exp(ccs[h]))], axis=1) for h in heads]
    uws = [_dot_x3(_split2(invs[h]), _split2(rhs[h])) for h in heads]

    @pl.when(c == 0)
    def _():
        s_ref[...] = s0_ref[...]

    c_lasts = [ccs[h][ch - 1:ch, :] for h in heads]
    ss = [s_ref[h].astype(BF16) for h in heads]
    ws_qs = [_dot(jnp.concatenate([uws[h][:, hd:], qs[h] * jnp.exp(ccs[h])], axis=0), ss[h]) for h in heads]
    v_news = [uws[h][:, :hd] - ws_qs[h][:ch, :] for h in heads]
    os_ = [ws_qs[h][ch:, :] + _dot(qk_kk[h][:ch, :] * decays[h], v_news[h]) for h in heads]
    upd = [_dot_tn(ks[h] * jnp.exp(c_lasts[h] - ccs[h]), v_news[h]) for h in heads]
    for h in heads:
        sl = slice(h * hd, (h + 1) * hd)
        s_ref[h] = s_ref[h] * jnp.exp(c_lasts[h]) + upd[h]
        o_ref[:, sl] = (_rms(os_[h], nw_ref[...]) * _silu(z_ref[:, sl])).astype(o_ref.dtype)

    @pl.when(c == pl.num_programs(1) - 1)
    def _():
        sfin_ref[...] = s_ref[...]


def _gdn(proj, buf8, conv_w, zgb, gbt, a_log, dt_bias, s0, norm_w, t_valid):
    bsz, t, _ = proj.shape
    c3 = conv_w.shape[1]
    ch = B_CHUNK
    nc = t // ch
    b_dim = B_HEADS * HEAD_DIM
    gb_blk = b_dim // LANES
    pad = jnp.zeros((LANES - B_HEADS,), F32)
    al_row = jnp.concatenate([a_log.astype(F32), pad]).reshape(1, LANES)
    dt_row = jnp.concatenate([dt_bias.astype(F32), pad]).reshape(1, LANES)
    pad_c = jnp.zeros((2 * SUBLANES - B_HEADS,), F32)
    al_col = jnp.broadcast_to(jnp.concatenate([a_log.astype(F32), pad_c])[:, None], (2 * SUBLANES, LANES))
    dt_col = jnp.broadcast_to(jnp.concatenate([dt_bias.astype(F32), pad_c])[:, None], (2 * SUBLANES, LANES))
    full = lambda shape: pl.BlockSpec(shape, lambda b, c: (0,) * len(shape))
    row_spec = pl.BlockSpec((None, ch, b_dim), lambda b, c: (b, c, 0))
    st_spec = pl.BlockSpec((None, B_HEADS, HEAD_DIM, HEAD_DIM), lambda b, c: (b, 0, 0, 0))
    return pl.pallas_call(
        functools.partial(_gdn_kernel, t_valid=t_valid),
        grid=(bsz, nc),
        in_specs=[pl.BlockSpec((None, ch, c3), lambda b, c: (b, c, 1)),
                  pl.BlockSpec((None, SUBLANES, c3), lambda b, c: (b, jnp.maximum(c * (ch // SUBLANES) - 1, 0), 1)),
                  pl.BlockSpec((None, SUBLANES, c3), lambda b, c: (b, 0, 0)),
                  full((B_CONV, c3)),
                  pl.BlockSpec((None, ch, LANES), lambda b, c: (b, c, gb_blk)),
                  pl.BlockSpec((None, None, 2 * SUBLANES, ch), lambda b, c: (b, c, 0, 0)),
                  full((1, LANES)), full((1, LANES)),
                  full((2 * SUBLANES, LANES)), full((2 * SUBLANES, LANES)),
                  st_spec, row_spec, full((1, HEAD_DIM))],
        out_specs=[row_spec, st_spec],
        out_shape=[jax.ShapeDtypeStruct((bsz, t, b_dim), BF16),
                   jax.ShapeDtypeStruct((bsz, B_HEADS, HEAD_DIM, HEAD_DIM), F32)],
        scratch_shapes=[pltpu.VMEM((SUBLANES + ch, c3), F32),
                        pltpu.VMEM((B_HEADS, HEAD_DIM, HEAD_DIM), F32)],
        compiler_params=_cparams("parallel", "arbitrary"),
        name="gdn",
    )(proj, proj, buf8, conv_w, zgb, gbt, al_row, dt_row, al_col, dt_col, s0, zgb, norm_w)


def _hgrn_kernel(q_ref, f_ref, i_ref, g_ref, llb_ref, l1m_ref, oml_ref, s0_ref, nw_ref,
                 o_ref, sfin_ref, st_ref, *, t_valid):
    c = pl.program_id(1)
    ch = C_CHUNK
    hd = HEAD_DIM

    @pl.when(c == 0)
    def _():
        for h in range(C_HEADS):
            st_ref[h] = s0_ref[h].T

    ri = _iota((ch, ch), 0)
    ci = _iota((ch, ch), 1)
    sels, level_masks = [jnp.where(ri >= ci, 1.0, 0.0)], []
    half = SUBLANES
    while half < ch:
        base = (ri // (2 * half)) * (2 * half)
        ref_row = base + half - 1
        later = (ri - base) >= half
        lo_col = jnp.where(later, ref_row, ri)
        hi_col = jnp.where(later, ri, ref_row)
        sels.append(jnp.where(ci > lo_col, jnp.where(ci <= hi_col, 1.0, 0.0), 0.0))
        cbase = (ci // (2 * half)) * (2 * half)
        level_masks.append(later & (cbase == base) & ((ci - cbase) < half))
        half *= 2
    sel_all = jnp.concatenate(sels, axis=0).astype(BF16)
    lag_masks = [(ci == ri - lag) & ((ri % SUBLANES) >= lag) for lag in range(SUBLANES)]
    ones = jnp.ones((hd, hd), BF16)
    row_ok = None if t_valid is None else (c * ch + _iota((ch, 1), 0)) < t_valid

    def gates(h):
        sl = slice(h * hd, (h + 1) * hd)
        f = f_ref[:, sl]
        log_sig = jnp.minimum(f, 0.0) - jnp.log(1.0 + jnp.exp(-jnp.abs(f)))
        a = llb_ref[:, sl]
        b = l1m_ref[:, sl] + log_sig
        log2_f = (jnp.maximum(a, b) + jnp.log(1.0 + jnp.exp(-jnp.abs(a - b)))) * _LOG2_E
        k = oml_ref[:, sl] * _sigmoid(-f)
        q = q_ref[:, sl] * (hd ** -0.5)
        v = i_ref[:, sl]
        if row_ok is not None:
            log2_f = jnp.where(row_ok, log2_f, 0.0)
            k = jnp.where(row_ok, k, 0.0)
            q = jnp.where(row_ok, q, 0.0)
            v = jnp.where(row_ok, v, 0.0)
        return log2_f, q, k, v

    def level_att(sums, q, k):
        att = jnp.zeros((ch, ch), F32)
        for i, mask in enumerate(level_masks):
            e = jnp.exp2(sums[(i + 1) * ch:(i + 2) * ch, :])
            att = att + jnp.where(mask, _dot_nt(q * e, k * e), 0.0)
        return att

    def lag_stack(sums, q, k):
        cum = sums[:ch, :]
        terms = [q * k]
        for lag in range(1, SUBLANES):
            terms.append(q * _roll_in_blocks(k, lag) * jnp.exp2(cum - _roll_in_blocks(cum, lag)))
        return jnp.concatenate(terms, axis=0).astype(BF16)

    def add_lags(att, row_sums):
        for lag in range(SUBLANES):
            att = att + jnp.where(lag_masks[lag], row_sums[lag * ch:(lag + 1) * ch, :ch], 0.0)
        return att

    for h0 in range(0, C_HEADS, _HGRN_HEAD_GROUP):
        heads = range(h0, h0 + _HGRN_HEAD_GROUP)
        gs = [gates(h) for h in heads]
        sums = [_dot_ones(sel_all, g[0], False) for g in gs]
        atts = [level_att(s, g[1], g[2]) for s, g in zip(sums, gs)]
        row_sums = [jnp.dot(lag_stack(s, g[1], g[2]), ones, preferred_element_type=F32) for s, g in zip(sums, gs)]
        atts = [add_lags(a, r) for a, r in zip(atts, row_sums)]
        for h, s, (_, q, k, v), att in zip(heads, sums, gs, atts):
            cum = s[:ch, :]
            c_last = cum[ch - 1:ch, :]
            st = st_ref[h]
            o = _dot_nt(q * jnp.exp2(cum), st) + _dot(att, v)
            st_ref[h] = st * jnp.exp2(c_last) + _dot_tn(v, k * jnp.exp2(c_last - cum))
            sl = slice(h * hd, (h + 1) * hd)
            o_ref[:, sl] = (_rms(o, nw_ref[...]) * _silu(g_ref[:, sl])).astype(o_ref.dtype)

    @pl.when(c == pl.num_programs(1) - 1)
    def _():
        for h in range(C_HEADS):
            sfin_ref[h] = st_ref[h].T


def _hgrn(proj, lb, s0, norm_w, t_valid):
    bsz, t, four_w = proj.shape
    width = four_w // 4
    ch = C_CHUNK
    nc = t // ch
    lb = lb.astype(F32).reshape(1, width)
    col = lambda j: pl.BlockSpec((None, ch, width), lambda b, c: (b, c, j))
    vec = pl.BlockSpec((1, width), lambda b, c: (0, 0))
    st_spec = pl.BlockSpec((None, C_HEADS, HEAD_DIM, HEAD_DIM), lambda b, c: (b, 0, 0, 0))
    return pl.pallas_call(
        functools.partial(_hgrn_kernel, t_valid=None if t_valid == t else t_valid),
        grid=(bsz, nc),
        in_specs=[col(0), col(1), col(2), col(3), vec, vec, vec, st_spec,
                  pl.BlockSpec((1, HEAD_DIM), lambda b, c: (0, 0))],
        out_specs=[pl.BlockSpec((None, ch, width), lambda b, c: (b, c, 0)), st_spec],
        out_shape=[jax.ShapeDtypeStruct((bsz, t, width), BF16),
                   jax.ShapeDtypeStruct((bsz, C_HEADS, HEAD_DIM, HEAD_DIM), F32)],
        scratch_shapes=[pltpu.VMEM((C_HEADS, HEAD_DIM, HEAD_DIM), F32)],
        compiler_params=_cparams("parallel", "arbitrary"),
        name="hgrn2",
    )(proj, proj, proj, proj, jnp.log(lb), jnp.log1p(-lb), 1.0 - lb, s0, norm_w)


def _pad_time(x, multiple):
    t = x.shape[1]
    t_pad = -(-t // multiple) * multiple
    return x if t_pad == t else jnp.pad(x, ((0, 0), (0, t_pad - t), (0, 0)))


def _ab_mixer(proj, zgb, t_valid, rope_tabs, kv_past, s0, conv_buf, j, conv_w, a_log, dt_bias, norm_w):
    bsz, t, _ = proj.shape
    a_dim = A_HEADS * HEAD_DIM
    b_dim = B_HEADS * HEAD_DIM

    q_r, k_r = _rope(proj, *rope_tabs)
    if kv_past is None:
        o_a = _dilated_attention(q_r, k_r, proj)
        keep = min(A_GROUPS[-1][0], t_valid)
        k_rows = k_r[:, t_valid - keep:t_valid]
        v_rows = proj[:, t_valid - keep:t_valid, 2 * a_dim:3 * a_dim]
    else:
        o_a = _cache_attention(q_r, k_r, proj, kv_past[0], kv_past[1], j, t_valid)
        k_rows = k_r[:, :t_valid]
        v_rows = proj[:, :t_valid, 2 * a_dim:3 * a_dim]
    k_rows = k_rows.reshape(bsz, -1, A_HEADS, HEAD_DIM)
    v_rows = v_rows.reshape(bsz, -1, A_HEADS, HEAD_DIM)

    raw_tail = proj[:, max(t_valid - (B_CONV - 1), 0):t_valid, 3 * a_dim:]
    buf_new = jnp.concatenate([conv_buf, raw_tail], axis=1)[:, -(B_CONV - 1):]
    buf8 = jnp.pad(conv_buf, ((0, 0), (SUBLANES - (B_CONV - 1), 0), (0, 0)))
    proj_c, zgb_c = _pad_time(proj, B_CHUNK), _pad_time(zgb, B_CHUNK)
    nc = proj_c.shape[1] // B_CHUNK
    gbt = zgb_c[:, :, b_dim:b_dim + 2 * SUBLANES].reshape(bsz, nc, B_CHUNK, 2 * SUBLANES).swapaxes(2, 3)
    o_b, s_new = _gdn(proj_c, buf8, conv_w, zgb_c, gbt, a_log, dt_bias, s0, norm_w.reshape(1, HEAD_DIM), t_valid)

    mixed = [o_a.reshape(bsz * t, a_dim), o_b[:, :t].reshape(bsz * t, b_dim)]
    return mixed, (k_rows, v_rows, s_new, buf_new)


def _hgrn_mixer(proj, t_valid, s0, lb, norm_w):
    bsz, t, n_proj = proj.shape
    o, s_new = _hgrn(_pad_time(proj, C_CHUNK), lb, s0, norm_w.reshape(1, HEAD_DIM), t_valid)
    return [o[:, :t].reshape(bsz * t, n_proj // 4)], s_new


def _trunk(groups, mods5, p):
    main, rider = groups
    d = main["x"].shape[2]
    depth = p["norm_pre"].shape[0]
    tm = 1024
    a_dim = A_HEADS * HEAD_DIM
    b_dim = B_HEADS * HEAD_DIM
    d_ff = p["ffn_wo"].shape[2]
    wi2d = p["ffn_wi"].reshape(-1, 2 * d_ff)
    wo2d = p["ffn_wo"].reshape(-1, d)
    ab_out2d = p["ab_w_out"].reshape(-1, d)
    c_in2d = p["c_w_in"].reshape(-1, p["c_w_in"].shape[2])
    c_out2d = p["c_w_out"].reshape(-1, d)
    norm_pre = p["norm_pre"].reshape(depth, 3, 1, d)
    norm_post = p["norm_post"].reshape(depth, 3, 1, d)
    lb_all = jnp.cumsum(jax.nn.softmax(p["c_lower_bounds"].astype(F32), axis=0), axis=0)
    lb_all = lb_all - lb_all[0:1]

    def rows2d(g, a):
        return a.reshape(g["x"].shape[0] * g["x"].shape[1], a.shape[-1])

    def rows3d(g, a):
        return a.reshape(g["x"].shape[0], g["x"].shape[1], a.shape[-1])

    def project(w2d, k_block, n, tn, w_t=None):
        outs = _mm([rows2d(main, main["h"])], w2d, k_block=k_block, n0=0, n=n, tn=tn, tm=tm,
                   rider=[rows2d(rider, rider["h"])], w_t=w_t)
        return [rows3d(g, o) for g, o in zip(groups, outs)]

    for g in groups:
        g["rope"] = _rope_tables(g["pos"])
        g["states"] = ([], [], [], [], [])
        (g["h"],) = _norm_call(g["x"], mods5, g["boff"], pre=(norm_pre[0, 0], 0, 0, 1))
    for layer in range(depth):
        j = layer // 2
        for sub in range(3):
            if sub == 1:
                if layer % 2 == 0:
                    w_zgb_t = p["w_zgb_t"][j]
                    projs = project(None, j, 3 * a_dim + 3 * b_dim, 1024, w_t=p["ab_in_t"])
                    zgbs = project(None, 0, w_zgb_t.shape[1], w_zgb_t.shape[1], w_t=w_zgb_t)
                    for g, proj, zgb in zip(groups, projs, zgbs):
                        bsz, past = g["x"].shape[0], g["past"]
                        if past is None:
                            kv_past = None
                            s0 = jnp.zeros((bsz, B_HEADS, HEAD_DIM, HEAD_DIM), F32)
                            buf = jnp.zeros((bsz, B_CONV - 1, 3 * b_dim), F32)
                        else:
                            kv_past, s0, buf = (past[0], past[1]), past[2][j], past[3][j]
                        g["mixed"], new = _ab_mixer(proj, zgb, g["t_valid"], g["rope"], kv_past, s0, buf, j,
                                                    p["b_conv_w"][j], p["b_a_log"][j], p["b_dt_bias"][j],
                                                    p["b_norm"][j])
                        for dst, val in zip(g["states"][:4], new):
                            dst.append(val)
                else:
                    projs = project(c_in2d, j, c_in2d.shape[1], 1024)
                    for g, proj in zip(groups, projs):
                        bsz, past = g["x"].shape[0], g["past"]
                        s0 = jnp.zeros((bsz, C_HEADS, HEAD_DIM, HEAD_DIM), F32) if past is None else past[4][j]
                        g["mixed"], s_new = _hgrn_mixer(proj, g["t_valid"], s0, lb_all[j], p["c_norm"][j])
                        g["states"][4].append(s_new)
                w_out2d = ab_out2d if layer % 2 == 0 else c_out2d
                post = (norm_post[layer, 1], layer, 5, 1.0)
                pre = (norm_pre[layer, 2], layer, 6, 7)
                for g in groups:
                    if g["x"].shape[1] >= _MM_NORM_MIN_ROWS:
                        g["x"], g["h"] = _mm_norm(g["mixed"], w_out2d, j, g["x"], mods5, g["boff"], post=post, pre=pre)
                    else:
                        y = _mm(g["mixed"], w_out2d, k_block=j, n0=0, n=d, tn=1024, tm=tm)
                        g["x"], g["h"] = _norm_call(g["x"], mods5, g["boff"], post=(rows3d(g, y),) + post, pre=pre)
                continue
            k_ffn = layer * 2 + (0 if sub == 0 else 1)
            acts = _swiglu_in(rows2d(main, main["h"]), wi2d, k_ffn, 2 * tm, rider=rows2d(rider, rider["h"]))
            ys = _mm([acts[0]], wo2d, k_block=k_ffn, n0=0, n=d, tn=512, tm=512, rider=[acts[1]])
            if sub == 0:
                pre = (norm_pre[layer, 1], layer, 3, 4)
            elif layer + 1 < depth:
                pre = (norm_pre[layer + 1, 0], layer + 1, 0, 1)
            else:
                pre = None
            for g, y in zip(groups, ys):
                post = (rows3d(g, y), norm_post[layer, sub], layer, 3 * sub + 2, FFN_RESIDUAL)
                outs = _norm_call(g["x"], mods5, g["boff"], post=post, pre=pre)
                g["x"] = outs[0]
                g["h"] = outs[1] if pre is not None else None
    return [(g["x"],) + tuple(jnp.stack(s) for s in g["states"]) for g in groups]


def kernel(x_prompt, x_sample, cache_a_k, cache_a_v, state_b_s, state_b_conv, state_c_s, c_prompt, c_sample,
           ada_w, ada_b, norm_pre, norm_post, ffn_wi, ffn_wo, ab_w_in, ab_w_out, b_conv_w, b_a_log, b_dt_bias,
           b_norm, c_w_in, c_w_out, c_lower_bounds, c_norm):
    p = dict(norm_pre=norm_pre, norm_post=norm_post, ffn_wi=ffn_wi, ffn_wo=ffn_wo, ab_w_in=ab_w_in,
             ab_w_out=ab_w_out, b_conv_w=b_conv_w, b_a_log=b_a_log, b_dt_bias=b_dt_bias, b_norm=b_norm,
             c_w_in=c_w_in, c_w_out=c_w_out, c_lower_bounds=c_lower_bounds, c_norm=c_norm)
    depth, d = norm_pre.shape[0], x_prompt.shape[2]
    main_cols = (3 * A_HEADS + 3 * B_HEADS) * HEAD_DIM
    gb_cols = 2 * B_HEADS
    ab_in_t = jnp.swapaxes(ab_w_in, 1, 2)
    p["ab_in_t"] = ab_in_t
    p["w_zgb_t"] = [jnp.concatenate([ab_in_t[j, main_cols + gb_cols:], ab_in_t[j, main_cols:main_cols + gb_cols],
                                     jnp.zeros((LANES - gb_cols, d), F32)], axis=0)[None]
                    for j in range(ab_w_in.shape[0])]
    n_p, t_p = x_prompt.shape[0], x_prompt.shape[1]
    n_s, t_s = x_sample.shape[0], x_sample.shape[1]
    past_len = 16384

    rows = -(-(n_p + n_s) // (2 * SUBLANES)) * (2 * SUBLANES)
    c_all = jnp.concatenate([c_prompt, c_sample, jnp.zeros((rows - n_p - n_s, d), F32)], axis=0)
    mods5 = _ada_mods(c_all, ada_w, ada_b).reshape(depth, rows, N_MOD, 1, d)

    prompt = dict(x=x_prompt, t_valid=t_p, pos=jnp.arange(t_p, dtype=jnp.int32), boff=0, past=None)
    sample = dict(x=jnp.pad(x_sample, ((0, 0), (0, SAMPLE_T_PAD - t_s), (0, 0))), t_valid=t_s,
                  pos=past_len + jnp.arange(SAMPLE_T_PAD, dtype=jnp.int32), boff=n_p,
                  past=(cache_a_k, cache_a_v, state_b_s, state_b_conv, state_c_s))
    (y_p, ak_p, av_p, bs_p, bc_p, cs_p), (y_s, ak_s, av_s, bs_s, bc_s, cs_s) = _trunk((prompt, sample), mods5, p)
    return (y_p, y_s[:, :t_s], ak_p, av_p, bs_p, bc_p, cs_p, ak_s, av_s, bs_s, bc_s, cs_s)
```

```python
import functools
import math

import jax
import jax.numpy as jnp
from jax import lax
from jax.experimental import pallas as pl
from jax.experimental.pallas import tpu as pltpu

F32 = jnp.float32
BF16 = jnp.bfloat16
HIGHEST = lax.Precision.HIGHEST

LANES = 128
SUBLANES = 8
VMEM_LIMIT_BYTES = 56 * 1024 * 1024

_LOG2_E = 1.4426950408889634
NORM_EPS = 1e-6
FFN_RESIDUAL = 0.5
N_MOD = 9
ROPE_THETA = 10000.0
A_HEADS = 8
A_GROUPS = ((128, 1), (512, 4), (2048, 16))
A_BLOCK = 128
B_HEADS = 8
B_CONV = 4
B_CHUNK = 64
C_HEADS = 16
HEAD_DIM = 128
SAMPLE_T_PAD = 16
C_CHUNK = 64
_MM_NORM_MIN_ROWS = 512
_HGRN_HEAD_GROUP = 16
_HGRN_CHUNKS_PER_STEP = 4


def _cparams(*sem):
    return pltpu.CompilerParams(dimension_semantics=sem, vmem_limit_bytes=VMEM_LIMIT_BYTES)


def _sigmoid(x):
    return 1.0 / (1.0 + jnp.exp(-x))


def _silu(x):
    return x * _sigmoid(x)


def _softplus(x):
    return jnp.maximum(x, 0.0) + jnp.log1p(jnp.exp(-jnp.abs(x)))


def _rms(x, g):
    return x * lax.rsqrt(jnp.mean(x * x, axis=-1, keepdims=True) + NORM_EPS) * g


def _dot(a, b):
    return jnp.dot(a.astype(BF16), b.astype(BF16), preferred_element_type=F32)


def _dot_nt(a, b):
    return lax.dot_general(a.astype(BF16), b.astype(BF16), (((1,), (1,)), ((), ())),
                           preferred_element_type=F32)


def _dot_tn(a, b):
    return lax.dot_general(a.astype(BF16), b.astype(BF16), (((0,), (0,)), ((), ())),
                           preferred_element_type=F32)


def _split2(x):
    hi = x.astype(BF16)
    lo = (x - hi.astype(F32)).astype(BF16)
    return hi, lo


def _dot_x3(a_parts, b_parts):
    (ah, al), (bh, bl) = a_parts, b_parts
    dot = functools.partial(jnp.dot, preferred_element_type=F32)
    return dot(ah, bh) + (dot(ah, bl) + dot(al, bh))


def _dot_ones(ones_bf16, x, ones_on_right):
    hi = x.astype(BF16)
    r1 = x - hi.astype(F32)
    mid = r1.astype(BF16)
    lo = (r1 - mid.astype(F32)).astype(BF16)
    dot = functools.partial(jnp.dot, preferred_element_type=F32)
    if ones_on_right:
        return dot(hi, ones_bf16) + (dot(mid, ones_bf16) + dot(lo, ones_bf16))
    return dot(ones_bf16, hi) + (dot(ones_bf16, mid) + dot(ones_bf16, lo))


def _roll_in_blocks(x, lag):
    rows, cols = x.shape
    x3 = x.reshape(rows // SUBLANES, SUBLANES, cols)
    return pltpu.roll(x3, lag, 1).reshape(rows, cols)


def _iota(shape, dim):
    return lax.broadcasted_iota(jnp.int32, shape, dim)


def _ada_kernel(c_ref, w_ref, b_ref, o_ref):
    a = _silu(c_ref[...])
    o_ref[...] = _dot(a, w_ref[...]) + b_ref[...]


def _ada_mods(c_all, ada_w, ada_b):
    depth, d, n = ada_w.shape
    rows = c_all.shape[0]
    tn = 1024
    return pl.pallas_call(
        _ada_kernel,
        grid=(depth, n // tn),
        in_specs=[pl.BlockSpec((rows, d), lambda l, j: (0, 0)),
                  pl.BlockSpec((None, d, tn), lambda l, j: (l, 0, j)),
                  pl.BlockSpec((None, 1, tn), lambda l, j: (l, 0, j))],
        out_specs=pl.BlockSpec((None, rows, tn), lambda l, j: (l, 0, j)),
        out_shape=jax.ShapeDtypeStruct((depth, rows, n), F32),
        compiler_params=_cparams("parallel", "parallel"),
        name="ada_mods",
    )(c_all, ada_w, ada_b.reshape(depth, 1, n))


def _norm_kernel(*refs, has_post, has_pre, coef):
    it = iter(refs)
    x_ref = next(it)
    if has_post:
        y_ref, gpost_ref, gate_ref = next(it), next(it), next(it)
    if has_pre:
        gpre_ref, shift_ref, scale_ref = next(it), next(it), next(it)
    x = x_ref[...]
    if has_post:
        xo_ref = next(it)
        x = x + (coef * gate_ref[...]) * _rms(y_ref[...], gpost_ref[...])
        xo_ref[...] = x
    if has_pre:
        h_ref = next(it)
        h = _rms(x, gpre_ref[...]) * (1.0 + scale_ref[...]) + shift_ref[...]
        h_ref[...] = h.astype(BF16)


def _norm_call(x, mods5, boff, post=None, pre=None):
    bsz, t, d = x.shape
    tt = min(t, 512)
    row = pl.BlockSpec((None, tt, d), lambda b, i: (b, i, 0))
    vec = pl.BlockSpec((1, d), lambda b, i: (0, 0))

    def mod_spec(layer, k):
        return pl.BlockSpec((None, None, None, 1, d), lambda b, i: (layer, boff + b, k, 0, 0))

    args, in_specs, out_shape, out_specs = [x], [row], [], []
    coef = 1.0
    if post is not None:
        y, g_post, layer, gate_idx, coef = post
        args += [y, g_post, mods5]
        in_specs += [row, vec, mod_spec(layer, gate_idx)]
        out_shape.append(jax.ShapeDtypeStruct(x.shape, F32))
        out_specs.append(row)
    if pre is not None:
        g_pre, layer, shift_idx, scale_idx = pre
        args += [g_pre, mods5, mods5]
        in_specs += [vec, mod_spec(layer, shift_idx), mod_spec(layer, scale_idx)]
        out_shape.append(jax.ShapeDtypeStruct(x.shape, BF16))
        out_specs.append(row)
    outs = pl.pallas_call(
        functools.partial(_norm_kernel, has_post=post is not None, has_pre=pre is not None, coef=coef),
        grid=(bsz, t // tt),
        in_specs=in_specs, out_specs=out_specs, out_shape=out_shape,
        compiler_params=_cparams("parallel", "parallel"),
        name="sandwich_norm",
    )(*args)
    return outs


def _mm_kernel(*refs, k_sizes, has_rider, w_rows_are_outputs):
    n_x = len(k_sizes)
    x_refs = refs[:n_x]
    r_refs = refs[n_x:2 * n_x] if has_rider else ()
    rest = refs[n_x + len(r_refs):]
    w_ref, o_ref = rest[0], rest[1]
    ro_ref = rest[2] if has_rider else None
    wbf_ref = rest[-1]

    def product(parts):
        acc, off = None, 0
        for x_ref, ks in zip(parts, k_sizes):
            if w_rows_are_outputs:
                part = _dot_nt(x_ref[...], wbf_ref[:, off:off + ks])
            else:
                part = jnp.dot(x_ref[...].astype(BF16), wbf_ref[off:off + ks, :], preferred_element_type=F32)
            acc = part if acc is None else acc + part
            off += ks
        return acc

    @pl.when(pl.program_id(1) == 0)
    def _():
        wbf_ref[...] = w_ref[...].astype(BF16)
        if has_rider:
            ro_ref[...] = product(r_refs).astype(ro_ref.dtype)

    o_ref[...] = product(x_refs).astype(o_ref.dtype)


def _mm(xs, w2d, *, k_block, n0, n, tn, tm, out_dtype=F32, rider=None, w_t=None):
    m = xs[0].shape[0]
    k_sizes = tuple(x.shape[1] for x in xs)
    k = sum(k_sizes)
    tm = min(tm, m)
    assert m % tm == 0 and n % tn == 0 and n0 % tn == 0
    assert (w2d.shape[0] % k == 0) if w_t is None else (w_t.shape[2] == k)
    nb0 = n0 // tn
    in_specs = [pl.BlockSpec((tm, ks), lambda j, i: (i, 0)) for ks in k_sizes]
    out_specs = [pl.BlockSpec((tm, tn), lambda j, i: (i, j))]
    out_shape = [jax.ShapeDtypeStruct((m, n), out_dtype)]
    args = list(xs)
    if rider is not None:
        m2 = rider[0].shape[0]
        assert tuple(x.shape[1] for x in rider) == k_sizes
        in_specs += [pl.BlockSpec((m2, ks), lambda j, i: (0, 0)) for ks in k_sizes]
        out_specs.append(pl.BlockSpec((m2, tn), lambda j, i: (0, j)))
        out_shape.append(jax.ShapeDtypeStruct((m2, n), out_dtype))
        args += list(rider)
    if w_t is None:
        in_specs.append(pl.BlockSpec((k, tn), lambda j, i: (k_block, nb0 + j)))
        w_tile = (k, tn)
    else:
        in_specs.append(pl.BlockSpec((None, tn, k), lambda j, i: (k_block, nb0 + j, 0)))
        w_tile = (tn, k)
    outs = pl.pallas_call(
        functools.partial(_mm_kernel, k_sizes=k_sizes, has_rider=rider is not None,
                          w_rows_are_outputs=w_t is not None),
        grid=(n // tn, m // tm),
        in_specs=in_specs,
        out_specs=out_specs,
        out_shape=out_shape,
        scratch_shapes=[pltpu.VMEM(w_tile, BF16)],
        compiler_params=_cparams("parallel", "arbitrary"),
        name="matmul",
    )(*args, w2d if w_t is None else w_t)
    return outs[0] if rider is None else (outs[0], outs[1])


def _mm_norm_kernel(*refs, k_sizes, coef):
    n_x = len(k_sizes)
    x_refs = refs[:n_x]
    (w_ref, xres_ref, gpost_ref, gate_ref, gpre_ref, shift_ref, scale_ref,
     xo_ref, h_ref, wbf_ref) = refs[n_x:]

    @pl.when(pl.program_id(0) == 0)
    def _():
        wbf_ref[...] = w_ref[...].astype(BF16)

    rows = xres_ref.shape[0]
    n_split = 4 if rows % (4 * 2 * SUBLANES) == 0 else 1
    step = rows // n_split
    for r in range(n_split):
        rs = slice(r * step, (r + 1) * step)
        y, off = None, 0
        for x_ref, ks in zip(x_refs, k_sizes):
            part = jnp.dot(x_ref[rs, :].astype(BF16), wbf_ref[off:off + ks, :], preferred_element_type=F32)
            y = part if y is None else y + part
            off += ks
        x = xres_ref[rs, :] + (coef * gate_ref[...]) * _rms(y, gpost_ref[...])
        xo_ref[rs, :] = x
        h_ref[rs, :] = (_rms(x, gpre_ref[...]) * (1.0 + scale_ref[...]) + shift_ref[...]).astype(BF16)


def _mm_norm(xs, w2d, k_block, x_res, mods5, boff, post, pre):
    bsz, t, d = x_res.shape
    m = bsz * t
    k_sizes = tuple(x.shape[1] for x in xs)
    k = sum(k_sizes)
    tm = min(t, 512)
    assert t % tm == 0 and w2d.shape == (w2d.shape[0] // k * k, d)
    g_post, layer_post, gate_idx, coef = post
    g_pre, layer_pre, shift_idx, scale_idx = pre
    row = pl.BlockSpec((tm, d), lambda i: (i, 0))
    vec = pl.BlockSpec((1, d), lambda i: (0, 0))

    def mod_spec(layer, idx):
        return pl.BlockSpec((None, None, None, 1, d), lambda i: (layer, boff + (i * tm) // t, idx, 0, 0))

    in_specs = [pl.BlockSpec((tm, ks), lambda i: (i, 0)) for ks in k_sizes]
    in_specs += [pl.BlockSpec((k, d), lambda i: (k_block, 0), pipeline_mode=pl.Buffered(1)),
                 row, vec, mod_spec(layer_post, gate_idx), vec, mod_spec(layer_pre, shift_idx),
                 mod_spec(layer_pre, scale_idx)]
    x_new, h = pl.pallas_call(
        functools.partial(_mm_norm_kernel, k_sizes=k_sizes, coef=coef),
        grid=(m // tm,),
        in_specs=in_specs,
        out_specs=[row, row],
        out_shape=[jax.ShapeDtypeStruct((m, d), F32), jax.ShapeDtypeStruct((m, d), BF16)],
        scratch_shapes=[pltpu.VMEM((k, d), BF16)],
        compiler_params=_cparams("arbitrary"),
        name="matmul_norm",
    )(*xs, w2d, x_res.reshape(m, d), g_post, mods5, g_pre, mods5, mods5)
    return x_new.reshape(bsz, t, d), h.reshape(bsz, t, d)


_SWIGLU_GROUP = 4


def _swiglu_kernel(*refs, has_rider):
    ng = _SWIGLU_GROUP
    n_x = 2 if has_rider else 1
    x_ref, w_refs = refs[0], refs[n_x:n_x + 2 * ng]
    o_ref, wbf_ref = refs[n_x + 2 * ng], refs[-1]

    def gated(x, dst_ref):
        for g in range(ng):
            r = jnp.dot(x, wbf_ref[:, 2 * g * LANES:(2 * g + 2) * LANES], preferred_element_type=F32)
            u, v = r[:, :LANES], r[:, LANES:]
            dst_ref[:, g * LANES:(g + 1) * LANES] = (_silu(v) * u).astype(dst_ref.dtype)

    @pl.when(pl.program_id(1) == 0)
    def _():
        for g in range(ng):
            wbf_ref[:, (2 * g) * LANES:(2 * g + 1) * LANES] = w_refs[g][...].astype(BF16)
            wbf_ref[:, (2 * g + 1) * LANES:(2 * g + 2) * LANES] = w_refs[ng + g][...].astype(BF16)
        if has_rider:
            gated(refs[1][...], refs[n_x + 2 * ng + 1])

    gated(x_ref[...], o_ref)


def _swiglu_in(h, wi2d, k_block, tm, rider=None):
    m, k = h.shape
    d_ff = wi2d.shape[1] // 2
    assert d_ff % LANES == 0
    nblk = d_ff // LANES
    ng = _SWIGLU_GROUP
    tn = ng * LANES
    tm = min(tm, m)
    last = nblk - 1
    w_specs = [pl.BlockSpec((k, LANES), lambda j, i, g=g, base=base: (k_block, base + jnp.minimum(ng * j + g, last)))
               for base in (0, nblk) for g in range(ng)]
    x_specs = [pl.BlockSpec((tm, k), lambda j, i: (i, 0))]
    out_specs = [pl.BlockSpec((tm, tn), lambda j, i: (i, j))]
    out_shape = [jax.ShapeDtypeStruct((m, d_ff), BF16)]
    args = [h]
    if rider is not None:
        m2 = rider.shape[0]
        x_specs.append(pl.BlockSpec((m2, k), lambda j, i: (0, 0)))
        out_specs.append(pl.BlockSpec((m2, tn), lambda j, i: (0, j)))
        out_shape.append(jax.ShapeDtypeStruct((m2, d_ff), BF16))
        args.append(rider)
    outs = pl.pallas_call(
        functools.partial(_swiglu_kernel, has_rider=rider is not None),
        grid=(pl.cdiv(nblk, ng), m // tm),
        in_specs=x_specs + w_specs,
        out_specs=out_specs,
        out_shape=out_shape,
        scratch_shapes=[pltpu.VMEM((k, 2 * tn), BF16)],
        compiler_params=_cparams("parallel", "arbitrary"),
        name="swiglu_in",
    )(*args, *([wi2d] * (2 * ng)))
    return outs[0] if rider is None else (outs[0], outs[1])


def _rope_kernel(x_ref, cos_ref, sin_ref, q_ref, k_ref):
    cos, sin = cos_ref[...], sin_ref[...]
    for h in range(2 * A_HEADS):
        xh = x_ref[:, h * HEAD_DIM:(h + 1) * HEAD_DIM]
        r = xh * cos + pltpu.roll(xh, HEAD_DIM // 2, 1) * sin
        dst = q_ref if h < A_HEADS else k_ref
        hh = h % A_HEADS
        dst[:, hh * HEAD_DIM:(hh + 1) * HEAD_DIM] = r


def _rope(proj, cos_t, sin_t):
    bsz, t, _ = proj.shape
    a_dim = A_HEADS * HEAD_DIM
    tt = min(t, 256)
    out = jax.ShapeDtypeStruct((bsz, t, a_dim), F32)
    o_spec = pl.BlockSpec((None, tt, a_dim), lambda b, i: (b, i, 0))
    return pl.pallas_call(
        _rope_kernel,
        grid=(bsz, t // tt),
        in_specs=[pl.BlockSpec((None, tt, 2 * a_dim), lambda b, i: (b, i, 0)),
                  pl.BlockSpec((tt, HEAD_DIM), lambda b, i: (i, 0)),
                  pl.BlockSpec((tt, HEAD_DIM), lambda b, i: (i, 0))],
        out_specs=[o_spec, o_spec], out_shape=[out, out],
        compiler_params=_cparams("parallel", "parallel"),
        name="rope",
    )(proj, cos_t, sin_t)


def _rope_tables(pos):
    half = HEAD_DIM // 2
    inv_freq = jnp.power(ROPE_THETA, -jnp.arange(half, dtype=F32) / half)
    ang = pos.astype(F32)[:, None] * inv_freq[None, :]
    cos, sin = jnp.cos(ang), jnp.sin(ang)
    return jnp.concatenate([cos, cos], axis=1), jnp.concatenate([-sin, sin], axis=1)


_A_TOKEN_BLOCK = A_BLOCK * max(dil for _, dil in A_GROUPS)
_A_PRE_STRIDE = 4
_A_STAGED_OPERANDS = 8


def _dilated_attn_kernel(q_ref, kp_ref, kc_ref, vp_ref, vc_ref, o_ref, m_ref, l_ref, acc_ref, stage_ref):
    tb = pl.program_id(2)
    blk = A_BLOCK
    tokens = _A_TOKEN_BLOCK
    qi = _iota((blk, 2 * blk), 0)
    ki = _iota((blk, 2 * blk), 1)
    rel = qi + blk - ki
    has_prev = (ki >= blk) | (tb > 0)
    scale = HEAD_DIM ** -0.5
    last = len(A_GROUPS) - 1

    for g, (window, dil) in enumerate(A_GROUPS):
        span = window // dil
        band = (rel >= 0) & (rel <= span)
        reach = blk * dil
        staged = {}
        if dil % SUBLANES == 0:
            assert reach == tokens and dil % _A_PRE_STRIDE == 0
            part = tokens // _A_PRE_STRIDE
            for slab, ref in enumerate((q_ref, kp_ref, kc_ref, vp_ref, vc_ref, m_ref, l_ref, acc_ref)):
                staged[id(ref)] = slab
                for r0 in range(_A_PRE_STRIDE):
                    stage_ref[slab, r0 * part:(r0 + 1) * part, :] = ref[pl.ds(r0, part, stride=_A_PRE_STRIDE), :]

        def rows(ref, base, r, dil=dil, staged=staged):
            if dil == 1:
                return ref[base:base + blk, :]
            if id(ref) in staged:
                start = (r % _A_PRE_STRIDE) * (tokens // _A_PRE_STRIDE) + r // _A_PRE_STRIDE
                return stage_ref[staged[id(ref)], pl.ds(start, blk, stride=dil // _A_PRE_STRIDE), :]
            return ref[pl.ds(base + r, blk, stride=dil), :]

        def put(ref, base, r, val, dil=dil):
            if dil == 1:
                ref[base:base + blk, :] = val
            else:
                ref[pl.ds(base + r, blk, stride=dil), :] = val

        for r in range(dil):
            k_prev = rows(kp_ref, tokens - reach, r).astype(BF16)
            v_prev = rows(vp_ref, tokens - reach, r).astype(BF16)
            for s in range(tokens // reach):
                base = s * reach
                q = rows(q_ref, base, r)
                k_cur = rows(kc_ref, base, r).astype(BF16)
                v_cur = rows(vc_ref, base, r).astype(BF16)
                kcat = jnp.concatenate([k_prev, k_cur], axis=0)
                vcat = jnp.concatenate([v_prev, v_cur], axis=0)
                k_prev, v_prev = k_cur, v_cur
                valid = band & has_prev if s == 0 else band
                sc = jnp.where(valid, _dot_nt(q, kcat) * scale, -jnp.inf)
                m_loc = jnp.max(sc, axis=-1, keepdims=True)
                if g == 0:
                    p = jnp.exp(sc - m_loc)
                    put(m_ref, base, r, jnp.broadcast_to(m_loc, (blk, HEAD_DIM)))
                    put(l_ref, base, r, jnp.broadcast_to(jnp.sum(p, axis=-1, keepdims=True), (blk, HEAD_DIM)))
                    put(acc_ref, base, r, _dot(p, vcat))
                else:
                    m_old = rows(m_ref, base, r)
                    m_new = jnp.maximum(m_old, m_loc)
                    alpha = jnp.exp(m_old - m_new)
                    p = jnp.exp(sc - m_new[:, 0:1])
                    l_new = alpha * rows(l_ref, base, r) + jnp.sum(p, axis=-1, keepdims=True)
                    acc_new = alpha * rows(acc_ref, base, r) + _dot(p, vcat)
                    if g < last:
                        put(m_ref, base, r, m_new)
                        put(l_ref, base, r, l_new)
                        put(acc_ref, base, r, acc_new)
                    else:
                        put(acc_ref, base, r, acc_new / l_new)

    o_ref[...] = acc_ref[...].astype(o_ref.dtype)


def _dilated_attention(q_r, k_r, proj):
    bsz, t, a_dim = q_r.shape
    tokens = _A_TOKEN_BLOCK
    assert t % tokens == 0 and all(window // dil <= A_BLOCK for window, dil in A_GROUPS)
    blk = (None, tokens, HEAD_DIM)
    v_off = 2 * a_dim // HEAD_DIM
    cur = lambda b, h, i: (b, i, h)
    prev = lambda b, h, i: (b, jnp.maximum(i - 1, 0), h)
    v_cur = lambda b, h, i: (b, i, v_off + h)
    v_prev = lambda b, h, i: (b, jnp.maximum(i - 1, 0), v_off + h)
    return pl.pallas_call(
        _dilated_attn_kernel,
        grid=(bsz, A_HEADS, t // tokens),
        in_specs=[pl.BlockSpec(blk, cur), pl.BlockSpec(blk, prev), pl.BlockSpec(blk, cur),
                  pl.BlockSpec(blk, v_prev), pl.BlockSpec(blk, v_cur)],
        out_specs=pl.BlockSpec(blk, cur),
        out_shape=jax.ShapeDtypeStruct((bsz, t, a_dim), BF16),
        scratch_shapes=[pltpu.VMEM((tokens, HEAD_DIM), F32)] * 3
                       + [pltpu.VMEM((_A_STAGED_OPERANDS, tokens, HEAD_DIM), F32)],
        compiler_params=_cparams("parallel", "parallel", "parallel"),
        name="dilated_attention",
    )(q_r, k_r, k_r, proj, proj)


def _group_count(d):
    cnt = jnp.zeros(d.shape, F32)
    for window, dil in A_GROUPS:
        hit = (d >= 0) & (d % dil == 0) & (d <= window)
        cnt = cnt + jnp.where(hit, 1.0, 0.0)
    return cnt


def _cache_attn_kernel(q_ref, kn_ref, vn_ref, kc_ref, vc_ref, o_ref, *, n_buf):
    tq = SUBLANES
    nh = A_HEADS
    scale = HEAD_DIM ** -0.5
    heads_of = lambda ref: jnp.concatenate([ref[0:tq, h * HEAD_DIM:(h + 1) * HEAD_DIM] for h in range(nh)], axis=0)
    q, k_new, v_new = heads_of(q_ref), heads_of(kn_ref), heads_of(vn_ref)
    k_c = kc_ref[...].reshape(n_buf * nh, HEAD_DIM)
    v_c = vc_ref[...].reshape(n_buf * nh, HEAD_DIM)

    r_c = _iota((nh * tq, n_buf * nh), 0)
    c_c = _iota((nh * tq, n_buf * nh), 1)
    same_c = (c_c % nh) == (r_c // tq)
    cnt_c = jnp.where(same_c, _group_count(n_buf + (r_c % tq) - (c_c // nh)), 0.0)
    r_n = _iota((nh * tq, nh * tq), 0)
    c_n = _iota((nh * tq, nh * tq), 1)
    same_n = (c_n // tq) == (r_n // tq)
    cnt_n = jnp.where(same_n, _group_count((r_n % tq) - (c_n % tq)), 0.0)

    s_c = jnp.where(cnt_c > 0, _dot_nt(q, k_c) * scale, -jnp.inf)
    s_n = jnp.where(cnt_n > 0, _dot_nt(q, k_new) * scale, -jnp.inf)
    m = jnp.maximum(jnp.max(s_c, axis=-1, keepdims=True), jnp.max(s_n, axis=-1, keepdims=True))
    p_c = cnt_c * jnp.exp(s_c - m)
    p_n = cnt_n * jnp.exp(s_n - m)
    den = jnp.sum(p_c, axis=-1, keepdims=True) + jnp.sum(p_n, axis=-1, keepdims=True)
    o = (_dot(p_c, v_c) + _dot(p_n, v_new)) / den
    o_rows = jnp.concatenate([o[h * tq:(h + 1) * tq, :] for h in range(nh)], axis=1)
    pad = jnp.zeros((o_ref.shape[0] - tq, nh * HEAD_DIM), F32)
    o_ref[...] = jnp.concatenate([o_rows, pad], axis=0).astype(o_ref.dtype)


def _cache_attention(q_r, k_r, proj, cache_k, cache_v, layer, t_valid):
    bsz, tp, a_dim = q_r.shape
    n_buf = cache_k.shape[2]
    assert t_valid <= SUBLANES <= tp
    new = pl.BlockSpec((None, tp, a_dim), lambda b: (b, 0, 0))
    v_new = pl.BlockSpec((None, tp, a_dim), lambda b: (b, 0, 2))
    cache = pl.BlockSpec((None, None, n_buf, A_HEADS, HEAD_DIM), lambda b: (layer, b, 0, 0, 0))
    return pl.pallas_call(
        functools.partial(_cache_attn_kernel, n_buf=n_buf),
        grid=(bsz,),
        in_specs=[new, new, v_new, cache, cache],
        out_specs=new,
        out_shape=jax.ShapeDtypeStruct((bsz, tp, a_dim), BF16),
        compiler_params=_cparams("parallel"),
        name="cache_attention",
    )(q_r, k_r, proj, cache_k, cache_v)


def _gdn_kernel(raw_ref, prev_ref, buf_ref, cw_ref, gb_ref, gbt_ref, alr_ref, dtr_ref, alc_ref, dtc_ref,
                s0_ref, z_ref, nw_ref, o_ref, sfin_ref, halo_ref, s_ref, *, t_valid):
    c = pl.program_id(1)
    ch = B_CHUNK
    hd = HEAD_DIM
    b_dim = B_HEADS * hd

    halo_ref[0:SUBLANES, :] = jnp.where(c == 0, buf_ref[...], prev_ref[...])
    halo_ref[SUBLANES:SUBLANES + ch, :] = raw_ref[...]
    y = raw_ref[...] * cw_ref[B_CONV - 1:B_CONV, :]
    for j in range(B_CONV - 1):
        lag = B_CONV - 1 - j
        y = y + halo_ref[SUBLANES - lag:SUBLANES - lag + ch, :] * cw_ref[j:j + 1, :]
    act = _silu(y)

    row_ok = (c * ch + _iota((ch, 1), 0)) < t_valid
    col_ok = (c * ch + _iota((1, ch), 1)) < t_valid

    gb = gb_ref[...]
    g_col = jnp.where(row_ok, -jnp.exp(alr_ref[...]) * _softplus(gb + dtr_ref[...]), 0.0)
    beta_col = jnp.where(row_ok, _sigmoid(gb), 0.0)
    gbt = gbt_ref[...]
    g_row = jnp.where(col_ok, -jnp.exp(alc_ref[:, :ch]) * _softplus(gbt + dtc_ref[:, :ch]), 0.0)

    ri = _iota((ch, ch), 0)
    ci = _iota((ch, ch), 1)
    tri = ri >= ci
    strict = ri > ci
    eye = jnp.where(ri == ci, 1.0, 0.0)
    cum_col = _dot_ones(jnp.where(tri, 1.0, 0.0).astype(BF16), g_col, False)
    cum_row = _dot_ones(jnp.where(ri <= ci, 1.0, 0.0).astype(BF16), g_row, True)

    heads = range(B_HEADS)
    qs, ks, vs, ccs, betas, decays, lows = [], [], [], [], [], [], []
    for h in heads:
        q = act[:, h * hd:(h + 1) * hd]
        k = act[:, b_dim + h * hd:b_dim + (h + 1) * hd]
        v = act[:, 2 * b_dim + h * hd:2 * b_dim + (h + 1) * hd]
        q = q * lax.rsqrt(jnp.sum(q * q, axis=-1, keepdims=True) + 1e-6) * (hd ** -0.5)
        k = k * lax.rsqrt(jnp.sum(k * k, axis=-1, keepdims=True) + 1e-6)
        qs.append(jnp.where(row_ok, q, 0.0))
        ks.append(jnp.where(row_ok, k, 0.0))
        vs.append(jnp.where(row_ok, v, 0.0))
        cc = cum_col[:, h:h + 1]
        cr = cum_row[h:h + 1, :]
        ccs.append(cc)
        betas.append(beta_col[:, B_HEADS + h:B_HEADS + h + 1])
        decays.append(jnp.where(tri, jnp.exp(jnp.where(tri, cc - cr, 0.0)), 0.0))
    qk_kk = [_dot_nt(jnp.concatenate([qs[h], ks[h]], axis=0), ks[h]) for h in heads]
    lows = [jnp.where(strict, betas[h] * qk_kk[h][ch:, :] * decays[h], 0.0) for h in heads]
    invs = [eye - lows[h] for h in heads]
    pw_parts = [_split2(lows[h]) for h in heads]
    pws = [_dot_x3(pw_parts[h], pw_parts[h]) for h in heads]
    size = 2
    while size < ch:
        pw_parts = [_split2(pws[h]) for h in heads]
        invs = [invs[h] + _dot_x3(_split2(invs[h]), pw_parts[h]) for h in heads]
        size *= 2
        if size < ch:
            pws = [_dot_x3(pw_parts[h], pw_parts[h]) for h in heads]
    rhs = [jnp.concatenate([vs[h] * betas[h], ks[h] * (betas[h] * jnp.exp(ccs[h]))], axis=1) for h in heads]
    uws = [_dot_x3(_split2(invs[h]), _split2(rhs[h])) for h in heads]

    @pl.when(c == 0)
    def _():
        s_ref[...] = s0_ref[...]

    c_lasts = [ccs[h][ch - 1:ch, :] for h in heads]
    ss = [s_ref[h].astype(BF16) for h in heads]
    ws_qs = [_dot(jnp.concatenate([uws[h][:, hd:], qs[h] * jnp.exp(ccs[h])], axis=0), ss[h]) for h in heads]
    v_news = [uws[h][:, :hd] - ws_qs[h][:ch, :] for h in heads]
    os_ = [ws_qs[h][ch:, :] + _dot(qk_kk[h][:ch, :] * decays[h], v_news[h]) for h in heads]
    upd = [_dot_tn(ks[h] * jnp.exp(c_lasts[h] - ccs[h]), v_news[h]) for h in heads]
    for h in heads:
        sl = slice(h * hd, (h + 1) * hd)
        s_ref[h] = s_ref[h] * jnp.exp(c_lasts[h]) + upd[h]
        o_ref[:, sl] = (_rms(os_[h], nw_ref[...]) * _silu(z_ref[:, sl])).astype(o_ref.dtype)

    @pl.when(c == pl.num_programs(1) - 1)
    def _():
        sfin_ref[...] = s_ref[...]


def _gdn(proj, buf8, conv_w, zgb, gbt, a_log, dt_bias, s0, norm_w, t_valid):
    bsz, t, _ = proj.shape
    c3 = conv_w.shape[1]
    ch = B_CHUNK
    nc = t // ch
    b_dim = B_HEADS * HEAD_DIM
    gb_blk = b_dim // LANES
    pad = jnp.zeros((LANES - B_HEADS,), F32)
    al_row = jnp.concatenate([a_log.astype(F32), pad]).reshape(1, LANES)
    dt_row = jnp.concatenate([dt_bias.astype(F32), pad]).reshape(1, LANES)
    pad_c = jnp.zeros((2 * SUBLANES - B_HEADS,), F32)
    al_col = jnp.broadcast_to(jnp.concatenate([a_log.astype(F32), pad_c])[:, None], (2 * SUBLANES, LANES))
    dt_col = jnp.broadcast_to(jnp.concatenate([dt_bias.astype(F32), pad_c])[:, None], (2 * SUBLANES, LANES))
    full = lambda shape: pl.BlockSpec(shape, lambda b, c: (0,) * len(shape))
    row_spec = pl.BlockSpec((None, ch, b_dim), lambda b, c: (b, c, 0))
    st_spec = pl.BlockSpec((None, B_HEADS, HEAD_DIM, HEAD_DIM), lambda b, c: (b, 0, 0, 0))
    return pl.pallas_call(
        functools.partial(_gdn_kernel, t_valid=t_valid),
        grid=(bsz, nc),
        in_specs=[pl.BlockSpec((None, ch, c3), lambda b, c: (b, c, 1)),
                  pl.BlockSpec((None, SUBLANES, c3), lambda b, c: (b, jnp.maximum(c * (ch // SUBLANES) - 1, 0), 1)),
                  pl.BlockSpec((None, SUBLANES, c3), lambda b, c: (b, 0, 0)),
                  full((B_CONV, c3)),
                  pl.BlockSpec((None, ch, LANES), lambda b, c: (b, c, gb_blk)),
                  pl.BlockSpec((None, None, 2 * SUBLANES, ch), lambda b, c: (b, c, 0, 0)),
                  full((1, LANES)), full((1, LANES)),
                  full((2 * SUBLANES, LANES)), full((2 * SUBLANES, LANES)),
                  st_spec, row_spec, full((1, HEAD_DIM))],
        out_specs=[row_spec, st_spec],
        out_shape=[jax.ShapeDtypeStruct((bsz, t, b_dim), BF16),
                   jax.ShapeDtypeStruct((bsz, B_HEADS, HEAD_DIM, HEAD_DIM), F32)],
        scratch_shapes=[pltpu.VMEM((SUBLANES + ch, c3), F32),
                        pltpu.VMEM((B_HEADS, HEAD_DIM, HEAD_DIM), F32)],
        compiler_params=_cparams("parallel", "arbitrary"),
        name="gdn",
    )(proj, proj, buf8, conv_w, zgb, gbt, al_row, dt_row, al_col, dt_col, s0, zgb, norm_w)


def _hgrn_kernel(q_ref, f_ref, i_ref, g_ref, llb_ref, l1m_ref, oml_ref, s0_ref, nw_ref,
                 o_ref, sfin_ref, st_ref, *, t_valid, n_sub):
    c = pl.program_id(1)
    ch = C_CHUNK
    hd = HEAD_DIM

    @pl.when(c == 0)
    def _():
        for h in range(C_HEADS):
            st_ref[h] = s0_ref[h].T

    ri = _iota((ch, ch), 0)
    ci = _iota((ch, ch), 1)
    sels, level_masks = [jnp.where(ri >= ci, 1.0, 0.0)], []
    half = SUBLANES
    while half < ch:
        base = (ri // (2 * half)) * (2 * half)
        ref_row = base + half - 1
        later = (ri - base) >= half
        lo_col = jnp.where(later, ref_row, ri)
        hi_col = jnp.where(later, ri, ref_row)
        sels.append(jnp.where(ci > lo_col, jnp.where(ci <= hi_col, 1.0, 0.0), 0.0))
        cbase = (ci // (2 * half)) * (2 * half)
        level_masks.append(later & (cbase == base) & ((ci - cbase) < half))
        half *= 2
    sel_all = jnp.concatenate(sels, axis=0).astype(BF16)
    lag_masks = [(ci == ri - lag) & ((ri % SUBLANES) >= lag) for lag in range(SUBLANES)]
    ones = jnp.ones((hd, hd), BF16)
    def gates(h, sub):
        sl = slice(h * hd, (h + 1) * hd)
        rs = slice(sub * ch, (sub + 1) * ch)
        f = f_ref[rs, sl]
        log_sig = jnp.minimum(f, 0.0) - jnp.log(1.0 + jnp.exp(-jnp.abs(f)))
        a = llb_ref[:, sl]
        b = l1m_ref[:, sl] + log_sig
        log2_f = (jnp.maximum(a, b) + jnp.log(1.0 + jnp.exp(-jnp.abs(a - b)))) * _LOG2_E
        k = oml_ref[:, sl] * _sigmoid(-f)
        q = q_ref[rs, sl] * (hd ** -0.5)
        v = i_ref[rs, sl]
        if t_valid is not None:
            row_ok = ((c * n_sub + sub) * ch + _iota((ch, 1), 0)) < t_valid
            log2_f = jnp.where(row_ok, log2_f, 0.0)
            k = jnp.where(row_ok, k, 0.0)
            q = jnp.where(row_ok, q, 0.0)
            v = jnp.where(row_ok, v, 0.0)
        return log2_f, q, k, v

    def level_att(sums, q, k):
        att = jnp.zeros((ch, ch), F32)
        for i, mask in enumerate(level_masks):
            e = jnp.exp2(sums[(i + 1) * ch:(i + 2) * ch, :])
            att = att + jnp.where(mask, _dot_nt(q * e, k * e), 0.0)
        return att

    def lag_stack(sums, q, k):
        cum = sums[:ch, :]
        terms = [q * k]
        for lag in range(1, SUBLANES):
            terms.append(q * _roll_in_blocks(k, lag) * jnp.exp2(cum - _roll_in_blocks(cum, lag)))
        return jnp.concatenate(terms, axis=0).astype(BF16)

    def add_lags(att, row_sums):
        for lag in range(SUBLANES):
            att = att + jnp.where(lag_masks[lag], row_sums[lag * ch:(lag + 1) * ch, :ch], 0.0)
        return att

    for sub, h0 in ((sub, h0) for sub in range(n_sub) for h0 in range(0, C_HEADS, _HGRN_HEAD_GROUP)):
        heads = range(h0, h0 + _HGRN_HEAD_GROUP)
        rs = slice(sub * ch, (sub + 1) * ch)
        gs = [gates(h, sub) for h in heads]
        sums = [_dot_ones(sel_all, g[0], False) for g in gs]
        atts = [level_att(s, g[1], g[2]) for s, g in zip(sums, gs)]
        row_sums = [jnp.dot(lag_stack(s, g[1], g[2]), ones, preferred_element_type=F32) for s, g in zip(sums, gs)]
        atts = [add_lags(a, r) for a, r in zip(atts, row_sums)]
        for h, s, (_, q, k, v), att in zip(heads, sums, gs, atts):
            cum = s[:ch, :]
            c_last = cum[ch - 1:ch, :]
            st = st_ref[h]
            o = _dot_nt(q * jnp.exp2(cum), st) + _dot(att, v)
            st_ref[h] = st * jnp.exp2(c_last) + _dot_tn(v, k * jnp.exp2(c_last - cum))
            sl = slice(h * hd, (h + 1) * hd)
            o_ref[rs, sl] = (_rms(o, nw_ref[...]) * _silu(g_ref[rs, sl])).astype(o_ref.dtype)

    @pl.when(c == pl.num_programs(1) - 1)
    def _():
        for h in range(C_HEADS):
            sfin_ref[h] = st_ref[h].T


def _hgrn(proj, lb, s0, norm_w, t_valid):
    bsz, t, four_w = proj.shape
    width = four_w // 4
    n_sub = _HGRN_CHUNKS_PER_STEP if t % (_HGRN_CHUNKS_PER_STEP * C_CHUNK) == 0 else 1
    rows = n_sub * C_CHUNK
    lb = lb.astype(F32).reshape(1, width)
    col = lambda j: pl.BlockSpec((None, rows, width), lambda b, c: (b, c, j))
    vec = pl.BlockSpec((1, width), lambda b, c: (0, 0))
    st_spec = pl.BlockSpec((None, C_HEADS, HEAD_DIM, HEAD_DIM), lambda b, c: (b, 0, 0, 0))
    return pl.pallas_call(
        functools.partial(_hgrn_kernel, t_valid=None if t_valid == t else t_valid, n_sub=n_sub),
        grid=(bsz, t // rows),
        in_specs=[col(0), col(1), col(2), col(3), vec, vec, vec, st_spec,
                  pl.BlockSpec((1, HEAD_DIM), lambda b, c: (0, 0))],
        out_specs=[pl.BlockSpec((None, rows, width), lambda b, c: (b, c, 0)), st_spec],
        out_shape=[jax.ShapeDtypeStruct((bsz, t, width), BF16),
                   jax.ShapeDtypeStruct((bsz, C_HEADS, HEAD_DIM, HEAD_DIM), F32)],
        scratch_shapes=[pltpu.VMEM((C_HEADS, HEAD_DIM, HEAD_DIM), F32)],
        compiler_params=_cparams("parallel", "arbitrary"),
        name="hgrn2",
    )(proj, proj, proj, proj, jnp.log(lb), jnp.log1p(-lb), 1.0 - lb, s0, norm_w)


def _pad_time(x, multiple):
    t = x.shape[1]
    t_pad = -(-t // multiple) * multiple
    return x if t_pad == t else jnp.pad(x, ((0, 0), (0, t_pad - t), (0, 0)))


def _ab_mixer(proj, zgb, t_valid, rope_tabs, kv_past, s0, conv_buf, j, conv_w, a_log, dt_bias, norm_w):
    bsz, t, _ = proj.shape
    a_dim = A_HEADS * HEAD_DIM
    b_dim = B_HEADS * HEAD_DIM

    q_r, k_r = _rope(proj, *rope_tabs)
    if kv_past is None:
        o_a = _dilated_attention(q_r, k_r, proj)
        keep = min(A_GROUPS[-1][0], t_valid)
        k_rows = k_r[:, t_valid - keep:t_valid]
        v_rows = proj[:, t_valid - keep:t_valid, 2 * a_dim:3 * a_dim]
    else:
        o_a = _cache_attention(q_r, k_r, proj, kv_past[0], kv_past[1], j, t_valid)
        k_rows = k_r[:, :t_valid]
        v_rows = proj[:, :t_valid, 2 * a_dim:3 * a_dim]
    k_rows = k_rows.reshape(bsz, -1, A_HEADS, HEAD_DIM)
    v_rows = v_rows.reshape(bsz, -1, A_HEADS, HEAD_DIM)

    raw_tail = proj[:, max(t_valid - (B_CONV - 1), 0):t_valid, 3 * a_dim:]
    buf_new = jnp.concatenate([conv_buf, raw_tail], axis=1)[:, -(B_CONV - 1):]
    buf8 = jnp.pad(conv_buf, ((0, 0), (SUBLANES - (B_CONV - 1), 0), (0, 0)))
    proj_c, zgb_c = _pad_time(proj, B_CHUNK), _pad_time(zgb, B_CHUNK)
    nc = proj_c.shape[1] // B_CHUNK
    gbt = zgb_c[:, :, b_dim:b_dim + 2 * SUBLANES].reshape(bsz, nc, B_CHUNK, 2 * SUBLANES).swapaxes(2, 3)
    o_b, s_new = _gdn(proj_c, buf8, conv_w, zgb_c, gbt, a_log, dt_bias, s0, norm_w.reshape(1, HEAD_DIM), t_valid)

    mixed = [o_a.reshape(bsz * t, a_dim), o_b[:, :t].reshape(bsz * t, b_dim)]
    return mixed, (k_rows, v_rows, s_new, buf_new)


def _hgrn_mixer(proj, t_valid, s0, lb, norm_w):
    bsz, t, n_proj = proj.shape
    o, s_new = _hgrn(_pad_time(proj, C_CHUNK), lb, s0, norm_w.reshape(1, HEAD_DIM), t_valid)
    return [o[:, :t].reshape(bsz * t, n_proj // 4)], s_new


def _trunk(groups, mods5, p):
    main, rider = groups
    d = main["x"].shape[2]
    depth = p["norm_pre"].shape[0]
    tm = 1024
    a_dim = A_HEADS * HEAD_DIM
    b_dim = B_HEADS * HEAD_DIM
    d_ff = p["ffn_wo"].shape[2]
    wi2d = p["ffn_wi"].reshape(-1, 2 * d_ff)
    wo2d = p["ffn_wo"].reshape(-1, d)
    ab_out2d = p["ab_w_out"].reshape(-1, d)
    c_in2d = p["c_w_in"].reshape(-1, p["c_w_in"].shape[2])
    c_out2d = p["c_w_out"].reshape(-1, d)
    norm_pre = p["norm_pre"].reshape(depth, 3, 1, d)
    norm_post = p["norm_post"].reshape(depth, 3, 1, d)
    lb_all = jnp.cumsum(jax.nn.softmax(p["c_lower_bounds"].astype(F32), axis=0), axis=0)
    lb_all = lb_all - lb_all[0:1]

    def rows2d(g, a):
        return a.reshape(g["x"].shape[0] * g["x"].shape[1], a.shape[-1])

    def rows3d(g, a):
        return a.reshape(g["x"].shape[0], g["x"].shape[1], a.shape[-1])

    def project(w2d, k_block, n, tn, w_t=None):
        outs = _mm([rows2d(main, main["h"])], w2d, k_block=k_block, n0=0, n=n, tn=tn, tm=tm,
                   rider=[rows2d(rider, rider["h"])], w_t=w_t)
        return [rows3d(g, o) for g, o in zip(groups, outs)]

    for g in groups:
        g["rope"] = _rope_tables(g["pos"])
        g["states"] = ([], [], [], [], [])
        (g["h"],) = _norm_call(g["x"], mods5, g["boff"], pre=(norm_pre[0, 0], 0, 0, 1))
    for layer in range(depth):
        j = layer // 2
        for sub in range(3):
            if sub == 1:
                if layer % 2 == 0:
                    w_zgb_t = p["w_zgb_t"][j]
                    projs = project(None, j, 3 * a_dim + 3 * b_dim, 1024, w_t=p["ab_in_t"])
                    zgbs = project(None, 0, w_zgb_t.shape[1], w_zgb_t.shape[1], w_t=w_zgb_t)
                    for g, proj, zgb in zip(groups, projs, zgbs):
                        bsz, past = g["x"].shape[0], g["past"]
                        if past is None:
                            kv_past = None
                            s0 = jnp.zeros((bsz, B_HEADS, HEAD_DIM, HEAD_DIM), F32)
                            buf = jnp.zeros((bsz, B_CONV - 1, 3 * b_dim), F32)
                        else:
                            kv_past, s0, buf = (past[0], past[1]), past[2][j], past[3][j]
                        g["mixed"], new = _ab_mixer(proj, zgb, g["t_valid"], g["rope"], kv_past, s0, buf, j,
                                                    p["b_conv_w"][j], p["b_a_log"][j], p["b_dt_bias"][j],
                                                    p["b_norm"][j])
                        for dst, val in zip(g["states"][:4], new):
                            dst.append(val)
                else:
                    projs = project(c_in2d, j, c_in2d.shape[1], 1024)
                    for g, proj in zip(groups, projs):
                        bsz, past = g["x"].shape[0], g["past"]
                        s0 = jnp.zeros((bsz, C_HEADS, HEAD_DIM, HEAD_DIM), F32) if past is None else past[4][j]
                        g["mixed"], s_new = _hgrn_mixer(proj, g["t_valid"], s0, lb_all[j], p["c_norm"][j])
                        g["states"][4].append(s_new)
                w_out2d = ab_out2d if layer % 2 == 0 else c_out2d
                post = (norm_post[layer, 1], layer, 5, 1.0)
                pre = (norm_pre[layer, 2], layer, 6, 7)
                for g in groups:
                    if g["x"].shape[1] >= _MM_NORM_MIN_ROWS:
                        g["x"], g["h"] = _mm_norm(g["mixed"], w_out2d, j, g["x"], mods5, g["boff"], post=post, pre=pre)
                    else:
                        y = _mm(g["mixed"], w_out2d, k_block=j, n0=0, n=d, tn=1024, tm=tm)
                        g["x"], g["h"] = _norm_call(g["x"], mods5, g["boff"], post=(rows3d(g, y),) + post, pre=pre)
                continue
            k_ffn = layer * 2 + (0 if sub == 0 else 1)
            acts = _swiglu_in(rows2d(main, main["h"]), wi2d, k_ffn, 2 * tm, rider=rows2d(rider, rider["h"]))
            ys = _mm([acts[0]], wo2d, k_block=k_ffn, n0=0, n=d, tn=512, tm=512, rider=[acts[1]])
            if sub == 0:
                pre = (norm_pre[layer, 1], layer, 3, 4)
            elif layer + 1 < depth:
                pre = (norm_pre[layer + 1, 0], layer + 1, 0, 1)
            else:
                pre = None
            for g, y in zip(groups, ys):
                post = (rows3d(g, y), norm_post[layer, sub], layer, 3 * sub + 2, FFN_RESIDUAL)
                outs = _norm_call(g["x"], mods5, g["boff"], post=post, pre=pre)
                g["x"] = outs[0]
                g["h"] = outs[1] if pre is not None else None
    return [(g["x"],) + tuple(jnp.stack(s) for s in g["states"]) for g in groups]


def kernel(x_prompt, x_sample, cache_a_k, cache_a_v, state_b_s, state_b_conv, state_c_s, c_prompt, c_sample,
           ada_w, ada_b, norm_pre, norm_post, ffn_wi, ffn_wo, ab_w_in, ab_w_out, b_conv_w, b_a_log, b_dt_bias,
           b_norm, c_w_in, c_w_out, c_lower_bounds, c_norm):
    p = dict(norm_pre=norm_pre, norm_post=norm_post, ffn_wi=ffn_wi, ffn_wo=ffn_wo, ab_w_in=ab_w_in,
             ab_w_out=ab_w_out, b_conv_w=b_conv_w, b_a_log=b_a_log, b_dt_bias=b_dt_bias, b_norm=b_norm,
             c_w_in=c_w_in, c_w_out=c_w_out, c_lower_bounds=c_lower_bounds, c_norm=c_norm)
    depth, d = norm_pre.shape[0], x_prompt.shape[2]
    main_cols = (3 * A_HEADS + 3 * B_HEADS) * HEAD_DIM
    gb_cols = 2 * B_HEADS
    ab_in_t = jnp.swapaxes(ab_w_in, 1, 2)
    p["ab_in_t"] = ab_in_t
    p["w_zgb_t"] = [jnp.concatenate([ab_in_t[j, main_cols + gb_cols:], ab_in_t[j, main_cols:main_cols + gb_cols],
                                     jnp.zeros((LANES - gb_cols, d), F32)], axis=0)[None]
                    for j in range(ab_w_in.shape[0])]
    n_p, t_p = x_prompt.shape[0], x_prompt.shape[1]
    n_s, t_s = x_sample.shape[0], x_sample.shape[1]
    past_len = 16384

    rows = -(-(n_p + n_s) // (2 * SUBLANES)) * (2 * SUBLANES)
    c_all = jnp.concatenate([c_prompt, c_sample, jnp.zeros((rows - n_p - n_s, d), F32)], axis=0)
    mods5 = _ada_mods(c_all, ada_w, ada_b).reshape(depth, rows, N_MOD, 1, d)

    prompt = dict(x=x_prompt, t_valid=t_p, pos=jnp.arange(t_p, dtype=jnp.int32), boff=0, past=None)
    sample = dict(x=jnp.pad(x_sample, ((0, 0), (0, SAMPLE_T_PAD - t_s), (0, 0))), t_valid=t_s,
                  pos=past_len + jnp.arange(SAMPLE_T_PAD, dtype=jnp.int32), boff=n_p,
                  past=(cache_a_k, cache_a_v, state_b_s, state_b_conv, state_c_s))
    (y_p, ak_p, av_p, bs_p, bc_p, cs_p), (y_s, ak_s, av_s, bs_s, bc_s, cs_s) = _trunk((prompt, sample), mods5, p)
    return (y_p, y_s[:, :t_s], ak_p, av_p, bs_p, bc_p, cs_p, ak_s, av_s, bs_s, bc_s, cs_s)
```

```python
import functools
import math

import jax
import jax.numpy as jnp
from jax import lax
from jax.experimental import pallas as pl
from jax.experimental.pallas import tpu as pltpu

F32 = jnp.float32
BF16 = jnp.bfloat16
HIGHEST = lax.Precision.HIGHEST

LANES = 128
SUBLANES = 8
VMEM_LIMIT_BYTES = 56 * 1024 * 1024

_LOG2_E = 1.4426950408889634
NORM_EPS = 1e-6
FFN_RESIDUAL = 0.5
N_MOD = 9
ROPE_THETA = 10000.0
A_HEADS = 8
A_GROUPS = ((128, 1), (512, 4), (2048, 16))
A_BLOCK = 128
B_HEADS = 8
B_CONV = 4
B_CHUNK = 64
C_HEADS = 16
HEAD_DIM = 128
SAMPLE_T_PAD = 16
C_CHUNK = 64
_MM_NORM_MIN_ROWS = 512
_HGRN_HEAD_GROUP = 16
_HGRN_CHUNKS_PER_STEP = 4


def _cparams(*sem):
    return pltpu.CompilerParams(dimension_semantics=sem, vmem_limit_bytes=VMEM_LIMIT_BYTES)


def _sigmoid(x):
    return 1.0 / (1.0 + jnp.exp(-x))


def _silu(x):
    return x * _sigmoid(x)


def _softplus(x):
    return jnp.maximum(x, 0.0) + jnp.log1p(jnp.exp(-jnp.abs(x)))


def _rms(x, g):
    return x * lax.rsqrt(jnp.mean(x * x, axis=-1, keepdims=True) + NORM_EPS) * g


def _dot(a, b):
    return jnp.dot(a.astype(BF16), b.astype(BF16), preferred_element_type=F32)


def _dot_nt(a, b):
    return lax.dot_general(a.astype(BF16), b.astype(BF16), (((1,), (1,)), ((), ())),
                           preferred_element_type=F32)


def _dot_tn(a, b):
    return lax.dot_general(a.astype(BF16), b.astype(BF16), (((0,), (0,)), ((), ())),
                           preferred_element_type=F32)


def _split2(x):
    hi = x.astype(BF16)
    lo = (x - hi.astype(F32)).astype(BF16)
    return hi, lo


def _dot_x3(a_parts, b_parts):
    (ah, al), (bh, bl) = a_parts, b_parts
    dot = functools.partial(jnp.dot, preferred_element_type=F32)
    return dot(ah, bh) + (dot(ah, bl) + dot(al, bh))


def _dot_ones(ones_bf16, x, ones_on_right):
    hi = x.astype(BF16)
    r1 = x - hi.astype(F32)
    mid = r1.astype(BF16)
    lo = (r1 - mid.astype(F32)).astype(BF16)
    dot = functools.partial(jnp.dot, preferred_element_type=F32)
    if ones_on_right:
        return dot(hi, ones_bf16) + (dot(mid, ones_bf16) + dot(lo, ones_bf16))
    return dot(ones_bf16, hi) + (dot(ones_bf16, mid) + dot(ones_bf16, lo))


def _roll_in_blocks(x, lag):
    rows, cols = x.shape
    x3 = x.reshape(rows // SUBLANES, SUBLANES, cols)
    return pltpu.roll(x3, lag, 1).reshape(rows, cols)


def _iota(shape, dim):
    return lax.broadcasted_iota(jnp.int32, shape, dim)


def _ada_kernel(c_ref, w_ref, b_ref, o_ref):
    a = _silu(c_ref[...])
    o_ref[...] = _dot(a, w_ref[...]) + b_ref[...]


def _ada_mods(c_all, ada_w, ada_b):
    depth, d, n = ada_w.shape
    rows = c_all.shape[0]
    tn = 1024
    return pl.pallas_call(
        _ada_kernel,
        grid=(depth, n // tn),
        in_specs=[pl.BlockSpec((rows, d), lambda l, j: (0, 0)),
                  pl.BlockSpec((None, d, tn), lambda l, j: (l, 0, j)),
                  pl.BlockSpec((None, 1, tn), lambda l, j: (l, 0, j))],
        out_specs=pl.BlockSpec((None, rows, tn), lambda l, j: (l, 0, j)),
        out_shape=jax.ShapeDtypeStruct((depth, rows, n), F32),
        compiler_params=_cparams("parallel", "parallel"),
        name="ada_mods",
    )(c_all, ada_w, ada_b.reshape(depth, 1, n))


def _norm_kernel(*refs, has_post, has_pre, coef):
    it = iter(refs)
    x_ref = next(it)
    if has_post:
        y_ref, gpost_ref, gate_ref = next(it), next(it), next(it)
    if has_pre:
        gpre_ref, shift_ref, scale_ref = next(it), next(it), next(it)
    x = x_ref[...]
    if has_post:
        xo_ref = next(it)
        x = x + (coef * gate_ref[...]) * _rms(y_ref[...], gpost_ref[...])
        xo_ref[...] = x
    if has_pre:
        h_ref = next(it)
        h = _rms(x, gpre_ref[...]) * (1.0 + scale_ref[...]) + shift_ref[...]
        h_ref[...] = h.astype(BF16)


def _norm_call(x, mods5, boff, post=None, pre=None):
    bsz, t, d = x.shape
    tt = min(t, 512)
    row = pl.BlockSpec((None, tt, d), lambda b, i: (b, i, 0))
    vec = pl.BlockSpec((1, d), lambda b, i: (0, 0))

    def mod_spec(layer, k):
        return pl.BlockSpec((None, None, None, 1, d), lambda b, i: (layer, boff + b, k, 0, 0))

    args, in_specs, out_shape, out_specs = [x], [row], [], []
    coef = 1.0
    if post is not None:
        y, g_post, layer, gate_idx, coef = post
        args += [y, g_post, mods5]
        in_specs += [row, vec, mod_spec(layer, gate_idx)]
        out_shape.append(jax.ShapeDtypeStruct(x.shape, F32))
        out_specs.append(row)
    if pre is not None:
        g_pre, layer, shift_idx, scale_idx = pre
        args += [g_pre, mods5, mods5]
        in_specs += [vec, mod_spec(layer, shift_idx), mod_spec(layer, scale_idx)]
        out_shape.append(jax.ShapeDtypeStruct(x.shape, BF16))
        out_specs.append(row)
    outs = pl.pallas_call(
        functools.partial(_norm_kernel, has_post=post is not None, has_pre=pre is not None, coef=coef),
        grid=(bsz, t // tt),
        in_specs=in_specs, out_specs=out_specs, out_shape=out_shape,
        compiler_params=_cparams("parallel", "parallel"),
        name="sandwich_norm",
    )(*args)
    return outs


def _mm_kernel(*refs, k_sizes, has_rider, w_rows_are_outputs):
    n_x = len(k_sizes)
    x_refs = refs[:n_x]
    r_refs = refs[n_x:2 * n_x] if has_rider else ()
    rest = refs[n_x + len(r_refs):]
    w_ref, o_ref = rest[0], rest[1]
    ro_ref = rest[2] if has_rider else None
    wbf_ref = rest[-1]

    def product(parts):
        acc, off = None, 0
        for x_ref, ks in zip(parts, k_sizes):
            if w_rows_are_outputs:
                part = _dot_nt(x_ref[...], wbf_ref[:, off:off + ks])
            else:
                part = jnp.dot(x_ref[...].astype(BF16), wbf_ref[off:off + ks, :], preferred_element_type=F32)
            acc = part if acc is None else acc + part
            off += ks
        return acc

    @pl.when(pl.program_id(1) == 0)
    def _():
        wbf_ref[...] = w_ref[...].astype(BF16)
        if has_rider:
            ro_ref[...] = product(r_refs).astype(ro_ref.dtype)

    o_ref[...] = product(x_refs).astype(o_ref.dtype)


def _mm(xs, w2d, *, k_block, n0, n, tn, tm, out_dtype=F32, rider=None, w_t=None):
    m = xs[0].shape[0]
    k_sizes = tuple(x.shape[1] for x in xs)
    k = sum(k_sizes)
    tm = min(tm, m)
    assert m % tm == 0 and n % tn == 0 and n0 % tn == 0
    assert (w2d.shape[0] % k == 0) if w_t is None else (w_t.shape[2] == k)
    nb0 = n0 // tn
    in_specs = [pl.BlockSpec((tm, ks), lambda j, i: (i, 0)) for ks in k_sizes]
    out_specs = [pl.BlockSpec((tm, tn), lambda j, i: (i, j))]
    out_shape = [jax.ShapeDtypeStruct((m, n), out_dtype)]
    args = list(xs)
    if rider is not None:
        m2 = rider[0].shape[0]
        assert tuple(x.shape[1] for x in rider) == k_sizes
        in_specs += [pl.BlockSpec((m2, ks), lambda j, i: (0, 0)) for ks in k_sizes]
        out_specs.append(pl.BlockSpec((m2, tn), lambda j, i: (0, j)))
        out_shape.append(jax.ShapeDtypeStruct((m2, n), out_dtype))
        args += list(rider)
    if w_t is None:
        in_specs.append(pl.BlockSpec((k, tn), lambda j, i: (k_block, nb0 + j)))
        w_tile = (k, tn)
    else:
        in_specs.append(pl.BlockSpec((None, tn, k), lambda j, i: (k_block, nb0 + j, 0)))
        w_tile = (tn, k)
    outs = pl.pallas_call(
        functools.partial(_mm_kernel, k_sizes=k_sizes, has_rider=rider is not None,
                          w_rows_are_outputs=w_t is not None),
        grid=(n // tn, m // tm),
        in_specs=in_specs,
        out_specs=out_specs,
        out_shape=out_shape,
        scratch_shapes=[pltpu.VMEM(w_tile, BF16)],
        compiler_params=_cparams("parallel", "arbitrary"),
        name="matmul",
    )(*args, w2d if w_t is None else w_t)
    return outs[0] if rider is None else (outs[0], outs[1])


def _mm_norm_kernel(*refs, k_sizes, coef):
    n_x = len(k_sizes)
    x_refs = refs[:n_x]
    (w_ref, xres_ref, gpost_ref, gate_ref, gpre_ref, shift_ref, scale_ref,
     xo_ref, h_ref, wbf_ref) = refs[n_x:]

    @pl.when(pl.program_id(0) == 0)
    def _():
        wbf_ref[...] = w_ref[...].astype(BF16)

    rows = xres_ref.shape[0]
    n_split = 4 if rows % (4 * 2 * SUBLANES) == 0 else 1
    step = rows // n_split
    for r in range(n_split):
        rs = slice(r * step, (r + 1) * step)
        y, off = None, 0
        for x_ref, ks in zip(x_refs, k_sizes):
            part = jnp.dot(x_ref[rs, :].astype(BF16), wbf_ref[off:off + ks, :], preferred_element_type=F32)
            y = part if y is None else y + part
            off += ks
        x = xres_ref[rs, :] + (coef * gate_ref[...]) * _rms(y, gpost_ref[...])
        xo_ref[rs, :] = x
        h_ref[rs, :] = (_rms(x, gpre_ref[...]) * (1.0 + scale_ref[...]) + shift_ref[...]).astype(BF16)


def _mm_norm(xs, w2d, k_block, x_res, mods5, boff, post, pre):
    bsz, t, d = x_res.shape
    m = bsz * t
    k_sizes = tuple(x.shape[1] for x in xs)
    k = sum(k_sizes)
    tm = min(t, 512)
    assert t % tm == 0 and w2d.shape == (w2d.shape[0] // k * k, d)
    g_post, layer_post, gate_idx, coef = post
    g_pre, layer_pre, shift_idx, scale_idx = pre
    row = pl.BlockSpec((tm, d), lambda i: (i, 0))
    vec = pl.BlockSpec((1, d), lambda i: (0, 0))

    def mod_spec(layer, idx):
        return pl.BlockSpec((None, None, None, 1, d), lambda i: (layer, boff + (i * tm) // t, idx, 0, 0))

    in_specs = [pl.BlockSpec((tm, ks), lambda i: (i, 0)) for ks in k_sizes]
    in_specs += [pl.BlockSpec((k, d), lambda i: (k_block, 0), pipeline_mode=pl.Buffered(1)),
                 row, vec, mod_spec(layer_post, gate_idx), vec, mod_spec(layer_pre, shift_idx),
                 mod_spec(layer_pre, scale_idx)]
    x_new, h = pl.pallas_call(
        functools.partial(_mm_norm_kernel, k_sizes=k_sizes, coef=coef),
        grid=(m // tm,),
        in_specs=in_specs,
        out_specs=[row, row],
        out_shape=[jax.ShapeDtypeStruct((m, d), F32), jax.ShapeDtypeStruct((m, d), BF16)],
        scratch_shapes=[pltpu.VMEM((k, d), BF16)],
        compiler_params=_cparams("arbitrary"),
        name="matmul_norm",
    )(*xs, w2d, x_res.reshape(m, d), g_post, mods5, g_pre, mods5, mods5)
    return x_new.reshape(bsz, t, d), h.reshape(bsz, t, d)


_SWIGLU_GROUP = 4


def _swiglu_kernel(*refs, has_rider):
    ng = _SWIGLU_GROUP
    n_x = 2 if has_rider else 1
    x_ref, w_refs = refs[0], refs[n_x:n_x + 2 * ng]
    o_ref, wbf_ref = refs[n_x + 2 * ng], refs[-1]

    def gated(x, dst_ref):
        for g in range(ng):
            r = jnp.dot(x[...], wbf_ref[:, 2 * g * LANES:(2 * g + 2) * LANES], preferred_element_type=F32)
            u, v = r[:, :LANES], r[:, LANES:]
            dst_ref[:, g * LANES:(g + 1) * LANES] = (_silu(v) * u).astype(dst_ref.dtype)

    @pl.when(pl.program_id(1) == 0)
    def _():
        for g in range(ng):
            wbf_ref[:, (2 * g) * LANES:(2 * g + 1) * LANES] = w_refs[g][...].astype(BF16)
            wbf_ref[:, (2 * g + 1) * LANES:(2 * g + 2) * LANES] = w_refs[ng + g][...].astype(BF16)
        if has_rider:
            gated(refs[1], refs[n_x + 2 * ng + 1])

    gated(x_ref, o_ref)


def _swiglu_in(h, wi2d, k_block, tm, rider=None):
    m, k = h.shape
    d_ff = wi2d.shape[1] // 2
    assert d_ff % LANES == 0
    nblk = d_ff // LANES
    ng = _SWIGLU_GROUP
    tn = ng * LANES
    tm = min(tm, m)
    last = nblk - 1
    w_specs = [pl.BlockSpec((k, LANES), lambda j, i, g=g, base=base: (k_block, base + jnp.minimum(ng * j + g, last)))
               for base in (0, nblk) for g in range(ng)]
    x_specs = [pl.BlockSpec((tm, k), lambda j, i: (i, 0))]
    out_specs = [pl.BlockSpec((tm, tn), lambda j, i: (i, j))]
    out_shape = [jax.ShapeDtypeStruct((m, d_ff), BF16)]
    args = [h]
    if rider is not None:
        m2 = rider.shape[0]
        x_specs.append(pl.BlockSpec((m2, k), lambda j, i: (0, 0)))
        out_specs.append(pl.BlockSpec((m2, tn), lambda j, i: (0, j)))
        out_shape.append(jax.ShapeDtypeStruct((m2, d_ff), BF16))
        args.append(rider)
    outs = pl.pallas_call(
        functools.partial(_swiglu_kernel, has_rider=rider is not None),
        grid=(pl.cdiv(nblk, ng), m // tm),
        in_specs=x_specs + w_specs,
        out_specs=out_specs,
        out_shape=out_shape,
        scratch_shapes=[pltpu.VMEM((k, 2 * tn), BF16)],
        compiler_params=_cparams("parallel", "arbitrary"),
        name="swiglu_in",
    )(*args, *([wi2d] * (2 * ng)))
    return outs[0] if rider is None else (outs[0], outs[1])


def _rope_kernel(x_ref, cos_ref, sin_ref, q_ref, k_ref):
    cos, sin = cos_ref[...], sin_ref[...]
    for h in range(2 * A_HEADS):
        xh = x_ref[:, h * HEAD_DIM:(h + 1) * HEAD_DIM]
        r = xh * cos + pltpu.roll(xh, HEAD_DIM // 2, 1) * sin
        dst = q_ref if h < A_HEADS else k_ref
        hh = h % A_HEADS
        dst[:, hh * HEAD_DIM:(hh + 1) * HEAD_DIM] = r


def _rope(proj, cos_t, sin_t):
    bsz, t, _ = proj.shape
    a_dim = A_HEADS * HEAD_DIM
    tt = min(t, 256)
    out = jax.ShapeDtypeStruct((bsz, t, a_dim), F32)
    o_spec = pl.BlockSpec((None, tt, a_dim), lambda b, i: (b, i, 0))
    return pl.pallas_call(
        _rope_kernel,
        grid=(bsz, t // tt),
        in_specs=[pl.BlockSpec((None, tt, 2 * a_dim), lambda b, i: (b, i, 0)),
                  pl.BlockSpec((tt, HEAD_DIM), lambda b, i: (i, 0)),
                  pl.BlockSpec((tt, HEAD_DIM), lambda b, i: (i, 0))],
        out_specs=[o_spec, o_spec], out_shape=[out, out],
        compiler_params=_cparams("parallel", "parallel"),
        name="rope",
    )(proj, cos_t, sin_t)


def _rope_tables(pos):
    half = HEAD_DIM // 2
    inv_freq = jnp.power(ROPE_THETA, -jnp.arange(half, dtype=F32) / half)
    ang = pos.astype(F32)[:, None] * inv_freq[None, :]
    cos, sin = jnp.cos(ang), jnp.sin(ang)
    return jnp.concatenate([cos, cos], axis=1), jnp.concatenate([-sin, sin], axis=1)


_A_TOKEN_BLOCK = A_BLOCK * max(dil for _, dil in A_GROUPS)
_A_PRE_STRIDE = 4
_A_STAGED_OPERANDS = 8


def _dilated_attn_kernel(q_ref, kp_ref, kc_ref, vp_ref, vc_ref, o_ref, m_ref, l_ref, acc_ref, stage_ref):
    tb = pl.program_id(2)
    blk = A_BLOCK
    tokens = _A_TOKEN_BLOCK
    qi = _iota((blk, 2 * blk), 0)
    ki = _iota((blk, 2 * blk), 1)
    rel = qi + blk - ki
    has_prev = (ki >= blk) | (tb > 0)
    scale = HEAD_DIM ** -0.5
    last = len(A_GROUPS) - 1

    for g, (window, dil) in enumerate(A_GROUPS):
        span = window // dil
        band = (rel >= 0) & (rel <= span)
        reach = blk * dil
        staged = {}
        if dil % SUBLANES == 0:
            assert reach == tokens and dil % _A_PRE_STRIDE == 0
            part = tokens // _A_PRE_STRIDE
            for slab, ref in enumerate((q_ref, kp_ref, kc_ref, vp_ref, vc_ref, m_ref, l_ref, acc_ref)):
                staged[id(ref)] = slab
                for r0 in range(_A_PRE_STRIDE):
                    stage_ref[slab, r0 * part:(r0 + 1) * part, :] = ref[pl.ds(r0, part, stride=_A_PRE_STRIDE), :]

        def rows(ref, base, r, dil=dil, staged=staged):
            if dil == 1:
                return ref[base:base + blk, :]
            if id(ref) in staged:
                start = (r % _A_PRE_STRIDE) * (tokens // _A_PRE_STRIDE) + r // _A_PRE_STRIDE
                return stage_ref[staged[id(ref)], pl.ds(start, blk, stride=dil // _A_PRE_STRIDE), :]
            return ref[pl.ds(base + r, blk, stride=dil), :]

        def put(ref, base, r, val, dil=dil):
            if dil == 1:
                ref[base:base + blk, :] = val
            else:
                ref[pl.ds(base + r, blk, stride=dil), :] = val

        for r in range(dil):
            k_prev = rows(kp_ref, tokens - reach, r).astype(BF16)
            v_prev = rows(vp_ref, tokens - reach, r).astype(BF16)
            for s in range(tokens // reach):
                base = s * reach
                q = rows(q_ref, base, r)
                k_cur = rows(kc_ref, base, r).astype(BF16)
                v_cur = rows(vc_ref, base, r).astype(BF16)
                kcat = jnp.concatenate([k_prev, k_cur], axis=0)
                vcat = jnp.concatenate([v_prev, v_cur], axis=0)
                k_prev, v_prev = k_cur, v_cur
                valid = band & has_prev if s == 0 else band
                sc = jnp.where(valid, _dot_nt(q, kcat) * scale, -jnp.inf)
                m_loc = jnp.max(sc, axis=-1, keepdims=True)
                if g == 0:
                    p = jnp.exp(sc - m_loc)
                    put(m_ref, base, r, jnp.broadcast_to(m_loc, (blk, HEAD_DIM)))
                    put(l_ref, base, r, jnp.broadcast_to(jnp.sum(p, axis=-1, keepdims=True), (blk, HEAD_DIM)))
                    put(acc_ref, base, r, _dot(p, vcat))
                else:
                    m_old = rows(m_ref, base, r)
                    m_new = jnp.maximum(m_old, m_loc)
                    alpha = jnp.exp(m_old - m_new)
                    p = jnp.exp(sc - m_new[:, 0:1])
                    l_new = alpha * rows(l_ref, base, r) + jnp.sum(p, axis=-1, keepdims=True)
                    acc_new = alpha * rows(acc_ref, base, r) + _dot(p, vcat)
                    if g < last:
                        put(m_ref, base, r, m_new)
                        put(l_ref, base, r, l_new)
                        put(acc_ref, base, r, acc_new)
                    else:
                        put(acc_ref, base, r, acc_new / l_new)

    o_ref[...] = acc_ref[...].astype(o_ref.dtype)


def _dilated_attention(q_r, k_r, proj):
    bsz, t, a_dim = q_r.shape
    tokens = _A_TOKEN_BLOCK
    assert t % tokens == 0 and all(window // dil <= A_BLOCK for window, dil in A_GROUPS)
    blk = (None, tokens, HEAD_DIM)
    v_off = 2 * a_dim // HEAD_DIM
    cur = lambda b, h, i: (b, i, h)
    prev = lambda b, h, i: (b, jnp.maximum(i - 1, 0), h)
    v_cur = lambda b, h, i: (b, i, v_off + h)
    v_prev = lambda b, h, i: (b, jnp.maximum(i - 1, 0), v_off + h)
    return pl.pallas_call(
        _dilated_attn_kernel,
        grid=(bsz, A_HEADS, t // tokens),
        in_specs=[pl.BlockSpec(blk, cur), pl.BlockSpec(blk, prev), pl.BlockSpec(blk, cur),
                  pl.BlockSpec(blk, v_prev), pl.BlockSpec(blk, v_cur)],
        out_specs=pl.BlockSpec(blk, cur),
        out_shape=jax.ShapeDtypeStruct((bsz, t, a_dim), BF16),
        scratch_shapes=[pltpu.VMEM((tokens, HEAD_DIM), F32)] * 3
                       + [pltpu.VMEM((_A_STAGED_OPERANDS, tokens, HEAD_DIM), F32)],
        compiler_params=_cparams("parallel", "parallel", "parallel"),
        name="dilated_attention",
    )(q_r, k_r, k_r, proj, proj)


def _group_count(d):
    cnt = jnp.zeros(d.shape, F32)
    for window, dil in A_GROUPS:
        hit = (d >= 0) & (d % dil == 0) & (d <= window)
        cnt = cnt + jnp.where(hit, 1.0, 0.0)
    return cnt


def _cache_attn_kernel(q_ref, kn_ref, vn_ref, kc_ref, vc_ref, o_ref, *, n_buf):
    tq = SUBLANES
    nh = A_HEADS
    scale = HEAD_DIM ** -0.5
    heads_of = lambda ref: jnp.concatenate([ref[0:tq, h * HEAD_DIM:(h + 1) * HEAD_DIM] for h in range(nh)], axis=0)
    q, k_new, v_new = heads_of(q_ref), heads_of(kn_ref), heads_of(vn_ref)
    k_c = kc_ref[...].reshape(n_buf * nh, HEAD_DIM)
    v_c = vc_ref[...].reshape(n_buf * nh, HEAD_DIM)

    r_c = _iota((nh * tq, n_buf * nh), 0)
    c_c = _iota((nh * tq, n_buf * nh), 1)
    same_c = (c_c % nh) == (r_c // tq)
    cnt_c = jnp.where(same_c, _group_count(n_buf + (r_c % tq) - (c_c // nh)), 0.0)
    r_n = _iota((nh * tq, nh * tq), 0)
    c_n = _iota((nh * tq, nh * tq), 1)
    same_n = (c_n // tq) == (r_n // tq)
    cnt_n = jnp.where(same_n, _group_count((r_n % tq) - (c_n % tq)), 0.0)

    s_c = jnp.where(cnt_c > 0, _dot_nt(q, k_c) * scale, -jnp.inf)
    s_n = jnp.where(cnt_n > 0, _dot_nt(q, k_new) * scale, -jnp.inf)
    m = jnp.maximum(jnp.max(s_c, axis=-1, keepdims=True), jnp.max(s_n, axis=-1, keepdims=True))
    p_c = cnt_c * jnp.exp(s_c - m)
    p_n = cnt_n * jnp.exp(s_n - m)
    den = jnp.sum(p_c, axis=-1, keepdims=True) + jnp.sum(p_n, axis=-1, keepdims=True)
    o = (_dot(p_c, v_c) + _dot(p_n, v_new)) / den
    o_rows = jnp.concatenate([o[h * tq:(h + 1) * tq, :] for h in range(nh)], axis=1)
    pad = jnp.zeros((o_ref.shape[0] - tq, nh * HEAD_DIM), F32)
    o_ref[...] = jnp.concatenate([o_rows, pad], axis=0).astype(o_ref.dtype)


def _cache_attention(q_r, k_r, proj, cache_k, cache_v, layer, t_valid):
    bsz, tp, a_dim = q_r.shape
    n_buf = cache_k.shape[2]
    assert t_valid <= SUBLANES <= tp
    new = pl.BlockSpec((None, tp, a_dim), lambda b: (b, 0, 0))
    v_new = pl.BlockSpec((None, tp, a_dim), lambda b: (b, 0, 2))
    cache = pl.BlockSpec((None, None, n_buf, A_HEADS, HEAD_DIM), lambda b: (layer, b, 0, 0, 0))
    return pl.pallas_call(
        functools.partial(_cache_attn_kernel, n_buf=n_buf),
        grid=(bsz,),
        in_specs=[new, new, v_new, cache, cache],
        out_specs=new,
        out_shape=jax.ShapeDtypeStruct((bsz, tp, a_dim), BF16),
        compiler_params=_cparams("parallel"),
        name="cache_attention",
    )(q_r, k_r, proj, cache_k, cache_v)


def _gdn_kernel(raw_ref, prev_ref, buf_ref, cw_ref, gb_ref, gbt_ref, alr_ref, dtr_ref, alc_ref, dtc_ref,
                s0_ref, z_ref, nw_ref, o_ref, sfin_ref, halo_ref, s_ref, *, t_valid):
    c = pl.program_id(1)
    ch = B_CHUNK
    hd = HEAD_DIM
    b_dim = B_HEADS * hd

    halo_ref[0:SUBLANES, :] = jnp.where(c == 0, buf_ref[...], prev_ref[...])
    halo_ref[SUBLANES:SUBLANES + ch, :] = raw_ref[...]
    y = raw_ref[...] * cw_ref[B_CONV - 1:B_CONV, :]
    for j in range(B_CONV - 1):
        lag = B_CONV - 1 - j
        y = y + halo_ref[SUBLANES - lag:SUBLANES - lag + ch, :] * cw_ref[j:j + 1, :]
    act = _silu(y)

    row_ok = (c * ch + _iota((ch, 1), 0)) < t_valid
    col_ok = (c * ch + _iota((1, ch), 1)) < t_valid

    gb = gb_ref[...]
    g_col = jnp.where(row_ok, -jnp.exp(alr_ref[...]) * _softplus(gb + dtr_ref[...]), 0.0)
    beta_col = jnp.where(row_ok, _sigmoid(gb), 0.0)
    gbt = gbt_ref[...]
    g_row = jnp.where(col_ok, -jnp.exp(alc_ref[:, :ch]) * _softplus(gbt + dtc_ref[:, :ch]), 0.0)

    ri = _iota((ch, ch), 0)
    ci = _iota((ch, ch), 1)
    tri = ri >= ci
    strict = ri > ci
    eye = jnp.where(ri == ci, 1.0, 0.0)
    cum_col = _dot_ones(jnp.where(tri, 1.0, 0.0).astype(BF16), g_col, False)
    cum_row = _dot_ones(jnp.where(ri <= ci, 1.0, 0.0).astype(BF16), g_row, True)

    heads = range(B_HEADS)
    qs, ks, vs, ccs, betas, decays, lows = [], [], [], [], [], [], []
    for h in heads:
        q = act[:, h * hd:(h + 1) * hd]
        k = act[:, b_dim + h * hd:b_dim + (h + 1) * hd]
        v = act[:, 2 * b_dim + h * hd:2 * b_dim + (h + 1) * hd]
        q = q * lax.rsqrt(jnp.sum(q * q, axis=-1, keepdims=True) + 1e-6) * (hd ** -0.5)
        k = k * lax.rsqrt(jnp.sum(k * k, axis=-1, keepdims=True) + 1e-6)
        qs.append(jnp.where(row_ok, q, 0.0))
        ks.append(jnp.where(row_ok, k, 0.0))
        vs.append(jnp.where(row_ok, v, 0.0))
        cc = cum_col[:, h:h + 1]
        cr = cum_row[h:h + 1, :]
        ccs.append(cc)
        betas.append(beta_col[:, B_HEADS + h:B_HEADS + h + 1])
        decays.append(jnp.where(tri, jnp.exp(jnp.where(tri, cc - cr, 0.0)), 0.0))
    qk_kk = [_dot_nt(jnp.concatenate([qs[h], ks[h]], axis=0), ks[h]) for h in heads]
    lows = [jnp.where(strict, betas[h] * qk_kk[h][ch:, :] * decays[h], 0.0) for h in heads]
    invs = [eye - lows[h] for h in heads]
    pw_parts = [_split2(lows[h]) for h in heads]
    pws = [_dot_x3(pw_parts[h], pw_parts[h]) for h in heads]
    size = 2
    while size < ch:
        pw_parts = [_split2(pws[h]) for h in heads]
        invs = [invs[h] + _dot_x3(_split2(invs[h]), pw_parts[h]) for h in heads]
        size *= 2
        if size < ch:
            pws = [_dot_x3(pw_parts[h], pw_parts[h]) for h in heads]
    rhs = [jnp.concatenate([vs[h] * betas[h], ks[h] * (betas[h] * jnp.exp(ccs[h]))], axis=1) for h in heads]
    uws = [_dot_x3(_split2(invs[h]), _split2(rhs[h])) for h in heads]

    @pl.when(c == 0)
    def _():
        s_ref[...] = s0_ref[...]

    c_lasts = [ccs[h][ch - 1:ch, :] for h in heads]
    ss = [s_ref[h].astype(BF16) for h in heads]
    ws_qs = [_dot(jnp.concatenate([uws[h][:, hd:], qs[h] * jnp.exp(ccs[h])], axis=0), ss[h]) for h in heads]
    v_news = [uws[h][:, :hd] - ws_qs[h][:ch, :] for h in heads]
    os_ = [ws_qs[h][ch:, :] + _dot(qk_kk[h][:ch, :] * decays[h], v_news[h]) for h in heads]
    upd = [_dot_tn(ks[h] * jnp.exp(c_lasts[h] - ccs[h]), v_news[h]) for h in heads]
    for h in heads:
        sl = slice(h * hd, (h + 1) * hd)
        s_ref[h] = s_ref[h] * jnp.exp(c_lasts[h]) + upd[h]
        o_ref[:, sl] = (_rms(os_[h], nw_ref[...]) * _silu(z_ref[:, sl])).astype(o_ref.dtype)

    @pl.when(c == pl.num_programs(1) - 1)
    def _():
        sfin_ref[...] = s_ref[...]


def _gdn(proj, buf8, conv_w, zgb, gbt, a_log, dt_bias, s0, norm_w, t_valid):
    bsz, t, _ = proj.shape
    c3 = conv_w.shape[1]
    ch = B_CHUNK
    nc = t // ch
    b_dim = B_HEADS * HEAD_DIM
    gb_blk = b_dim // LANES
    pad = jnp.zeros((LANES - B_HEADS,), F32)
    al_row = jnp.concatenate([a_log.astype(F32), pad]).reshape(1, LANES)
    dt_row = jnp.concatenate([dt_bias.astype(F32), pad]).reshape(1, LANES)
    pad_c = jnp.zeros((2 * SUBLANES - B_HEADS,), F32)
    al_col = jnp.broadcast_to(jnp.concatenate([a_log.astype(F32), pad_c])[:, None], (2 * SUBLANES, LANES))
    dt_col = jnp.broadcast_to(jnp.concatenate([dt_bias.astype(F32), pad_c])[:, None], (2 * SUBLANES, LANES))
    full = lambda shape: pl.BlockSpec(shape, lambda b, c: (0,) * len(shape))
    row_spec = pl.BlockSpec((None, ch, b_dim), lambda b, c: (b, c, 0))
    st_spec = pl.BlockSpec((None, B_HEADS, HEAD_DIM, HEAD_DIM), lambda b, c: (b, 0, 0, 0))
    return pl.pallas_call(
        functools.partial(_gdn_kernel, t_valid=t_valid),
        grid=(bsz, nc),
        in_specs=[pl.BlockSpec((None, ch, c3), lambda b, c: (b, c, 1)),
                  pl.BlockSpec((None, SUBLANES, c3), lambda b, c: (b, jnp.maximum(c * (ch // SUBLANES) - 1, 0), 1)),
                  pl.BlockSpec((None, SUBLANES, c3), lambda b, c: (b, 0, 0)),
                  full((B_CONV, c3)),
                  pl.BlockSpec((None, ch, LANES), lambda b, c: (b, c, gb_blk)),
                  pl.BlockSpec((None, None, 2 * SUBLANES, ch), lambda b, c: (b, c, 0, 0)),
                  full((1, LANES)), full((1, LANES)),
                  full((2 * SUBLANES, LANES)), full((2 * SUBLANES, LANES)),
                  st_spec, row_spec, full((1, HEAD_DIM))],
        out_specs=[row_spec, st_spec],
        out_shape=[jax.ShapeDtypeStruct((bsz, t, b_dim), BF16),
                   jax.ShapeDtypeStruct((bsz, B_HEADS, HEAD_DIM, HEAD_DIM), F32)],
        scratch_shapes=[pltpu.VMEM((SUBLANES + ch, c3), F32),
                        pltpu.VMEM((B_HEADS, HEAD_DIM, HEAD_DIM), F32)],
        compiler_params=_cparams("parallel", "arbitrary"),
        name="gdn",
    )(proj, proj, buf8, conv_w, zgb, gbt, al_row, dt_row, al_col, dt_col, s0, zgb, norm_w)


def _hgrn_kernel(q_ref, f_ref, i_ref, g_ref, llb_ref, l1m_ref, oml_ref, s0_ref, nw_ref,
                 o_ref, sfin_ref, st_ref, *, t_valid, n_sub):
    c = pl.program_id(1)
    ch = C_CHUNK
    hd = HEAD_DIM

    @pl.when(c == 0)
    def _():
        for h in range(C_HEADS):
            st_ref[h] = s0_ref[h].T

    ri = _iota((ch, ch), 0)
    ci = _iota((ch, ch), 1)
    sels, level_masks = [jnp.where(ri >= ci, 1.0, 0.0)], []
    half = SUBLANES
    while half < ch:
        base = (ri // (2 * half)) * (2 * half)
        ref_row = base + half - 1
        later = (ri - base) >= half
        lo_col = jnp.where(later, ref_row, ri)
        hi_col = jnp.where(later, ri, ref_row)
        sels.append(jnp.where(ci > lo_col, jnp.where(ci <= hi_col, 1.0, 0.0), 0.0))
        cbase = (ci // (2 * half)) * (2 * half)
        level_masks.append(later & (cbase == base) & ((ci - cbase) < half))
        half *= 2
    sel_all = jnp.concatenate(sels, axis=0).astype(BF16)
    lag_masks = [(ci == ri - lag) & ((ri % SUBLANES) >= lag) for lag in range(SUBLANES)]
    ones = jnp.ones((hd, hd), BF16)
    def gates(h, sub):
        sl = slice(h * hd, (h + 1) * hd)
        rs = slice(sub * ch, (sub + 1) * ch)
        f = f_ref[rs, sl]
        log_sig = jnp.minimum(f, 0.0) - jnp.log(1.0 + jnp.exp(-jnp.abs(f)))
        a = llb_ref[:, sl]
        b = l1m_ref[:, sl] + log_sig
        log2_f = (jnp.maximum(a, b) + jnp.log(1.0 + jnp.exp(-jnp.abs(a - b)))) * _LOG2_E
        k = oml_ref[:, sl] * _sigmoid(-f)
        q = q_ref[rs, sl] * (hd ** -0.5)
        v = i_ref[rs, sl]
        if t_valid is not None:
            row_ok = ((c * n_sub + sub) * ch + _iota((ch, 1), 0)) < t_valid
            log2_f = jnp.where(row_ok, log2_f, 0.0)
            k = jnp.where(row_ok, k, 0.0)
            q = jnp.where(row_ok, q, 0.0)
            v = jnp.where(row_ok, v, 0.0)
        return log2_f, q, k, v

    def level_att(sums, q, k):
        att = jnp.zeros((ch, ch), F32)
        for i, mask in enumerate(level_masks):
            e = jnp.exp2(sums[(i + 1) * ch:(i + 2) * ch, :])
            att = att + jnp.where(mask, _dot_nt(q * e, k * e), 0.0)
        return att

    def lag_stack(sums, q, k):
        cum = sums[:ch, :]
        terms = [q * k]
        for lag in range(1, SUBLANES):
            terms.append(q * _roll_in_blocks(k, lag) * jnp.exp2(cum - _roll_in_blocks(cum, lag)))
        return jnp.concatenate(terms, axis=0).astype(BF16)

    def add_lags(att, row_sums):
        for lag in range(SUBLANES):
            att = att + jnp.where(lag_masks[lag], row_sums[lag * ch:(lag + 1) * ch, :ch], 0.0)
        return att

    for sub, h0 in ((sub, h0) for sub in range(n_sub) for h0 in range(0, C_HEADS, _HGRN_HEAD_GROUP)):
        heads = range(h0, h0 + _HGRN_HEAD_GROUP)
        rs = slice(sub * ch, (sub + 1) * ch)
        gs = [gates(h, sub) for h in heads]
        sums = [_dot_ones(sel_all, g[0], False) for g in gs]
        atts = [level_att(s, g[1], g[2]) for s, g in zip(sums, gs)]
        row_sums = [jnp.dot(lag_stack(s, g[1], g[2]), ones, preferred_element_type=F32) for s, g in zip(sums, gs)]
        atts = [add_lags(a, r) for a, r in zip(atts, row_sums)]
        for h, s, (_, q, k, v), att in zip(heads, sums, gs, atts):
            cum = s[:ch, :]
            c_last = cum[ch - 1:ch, :]
            st = st_ref[h]
            o = _dot_nt(q * jnp.exp2(cum), st) + _dot(att, v)
            st_ref[h] = st * jnp.exp2(c_last) + _dot_tn(v, k * jnp.exp2(c_last - cum))
            sl = slice(h * hd, (h + 1) * hd)
            o_ref[rs, sl] = (_rms(o, nw_ref[...]) * _silu(g_ref[rs, sl])).astype(o_ref.dtype)

    @pl.when(c == pl.num_programs(1) - 1)
    def _():
        for h in range(C_HEADS):
            sfin_ref[h] = st_ref[h].T


def _hgrn(proj, lb, s0, norm_w, t_valid):
    bsz, t, four_w = proj.shape
    width = four_w // 4
    n_sub = _HGRN_CHUNKS_PER_STEP if t % (_HGRN_CHUNKS_PER_STEP * C_CHUNK) == 0 else 1
    rows = n_sub * C_CHUNK
    lb = lb.astype(F32).reshape(1, width)
    col = lambda j: pl.BlockSpec((None, rows, width), lambda b, c: (b, c, j))
    vec = pl.BlockSpec((1, width), lambda b, c: (0, 0))
    st_spec = pl.BlockSpec((None, C_HEADS, HEAD_DIM, HEAD_DIM), lambda b, c: (b, 0, 0, 0))
    return pl.pallas_call(
        functools.partial(_hgrn_kernel, t_valid=None if t_valid == t else t_valid, n_sub=n_sub),
        grid=(bsz, t // rows),
        in_specs=[col(0), col(1), col(2), col(3), vec, vec, vec, st_spec,
                  pl.BlockSpec((1, HEAD_DIM), lambda b, c: (0, 0))],
        out_specs=[pl.BlockSpec((None, rows, width), lambda b, c: (b, c, 0)), st_spec],
        out_shape=[jax.ShapeDtypeStruct((bsz, t, width), BF16),
                   jax.ShapeDtypeStruct((bsz, C_HEADS, HEAD_DIM, HEAD_DIM), F32)],
        scratch_shapes=[pltpu.VMEM((C_HEADS, HEAD_DIM, HEAD_DIM), F32)],
        compiler_params=_cparams("parallel", "arbitrary"),
        name="hgrn2",
    )(proj, proj, proj, proj, jnp.log(lb), jnp.log1p(-lb), 1.0 - lb, s0, norm_w)


def _pad_time(x, multiple):
    t = x.shape[1]
    t_pad = -(-t // multiple) * multiple
    return x if t_pad == t else jnp.pad(x, ((0, 0), (0, t_pad - t), (0, 0)))


def _ab_mixer(proj, zgb, t_valid, rope_tabs, kv_past, s0, conv_buf, j, conv_w, a_log, dt_bias, norm_w):
    bsz, t, _ = proj.shape
    a_dim = A_HEADS * HEAD_DIM
    b_dim = B_HEADS * HEAD_DIM

    q_r, k_r = _rope(proj, *rope_tabs)
    if kv_past is None:
        o_a = _dilated_attention(q_r, k_r, proj)
        keep = min(A_GROUPS[-1][0], t_valid)
        k_rows = k_r[:, t_valid - keep:t_valid]
        v_rows = proj[:, t_valid - keep:t_valid, 2 * a_dim:3 * a_dim]
    else:
        o_a = _cache_attention(q_r, k_r, proj, kv_past[0], kv_past[1], j, t_valid)
        k_rows = k_r[:, :t_valid]
        v_rows = proj[:, :t_valid, 2 * a_dim:3 * a_dim]
    k_rows = k_rows.reshape(bsz, -1, A_HEADS, HEAD_DIM)
    v_rows = v_rows.reshape(bsz, -1, A_HEADS, HEAD_DIM)

    raw_tail = proj[:, max(t_valid - (B_CONV - 1), 0):t_valid, 3 * a_dim:]
    buf_new = jnp.concatenate([conv_buf, raw_tail], axis=1)[:, -(B_CONV - 1):]
    buf8 = jnp.pad(conv_buf, ((0, 0), (SUBLANES - (B_CONV - 1), 0), (0, 0)))
    proj_c, zgb_c = _pad_time(proj, B_CHUNK), _pad_time(zgb, B_CHUNK)
    nc = proj_c.shape[1] // B_CHUNK
    gbt = zgb_c[:, :, b_dim:b_dim + 2 * SUBLANES].reshape(bsz, nc, B_CHUNK, 2 * SUBLANES).swapaxes(2, 3)
    o_b, s_new = _gdn(proj_c, buf8, conv_w, zgb_c, gbt, a_log, dt_bias, s0, norm_w.reshape(1, HEAD_DIM), t_valid)

    mixed = [o_a.reshape(bsz * t, a_dim), o_b[:, :t].reshape(bsz * t, b_dim)]
    return mixed, (k_rows, v_rows, s_new, buf_new)


def _hgrn_mixer(proj, t_valid, s0, lb, norm_w):
    bsz, t, n_proj = proj.shape
    o, s_new = _hgrn(_pad_time(proj, C_CHUNK), lb, s0, norm_w.reshape(1, HEAD_DIM), t_valid)
    return [o[:, :t].reshape(bsz * t, n_proj // 4)], s_new


def _trunk(groups, mods5, p):
    main, rider = groups
    d = main["x"].shape[2]
    depth = p["norm_pre"].shape[0]
    tm = 1024
    a_dim = A_HEADS * HEAD_DIM
    b_dim = B_HEADS * HEAD_DIM
    d_ff = p["ffn_wo"].shape[2]
    wi2d = p["ffn_wi"].reshape(-1, 2 * d_ff)
    wo2d = p["ffn_wo"].reshape(-1, d)
    ab_out2d = p["ab_w_out"].reshape(-1, d)
    c_in2d = p["c_w_in"].reshape(-1, p["c_w_in"].shape[2])
    c_out2d = p["c_w_out"].reshape(-1, d)
    norm_pre = p["norm_pre"].reshape(depth, 3, 1, d)
    norm_post = p["norm_post"].reshape(depth, 3, 1, d)
    lb_all = jnp.cumsum(jax.nn.softmax(p["c_lower_bounds"].astype(F32), axis=0), axis=0)
    lb_all = lb_all - lb_all[0:1]

    def rows2d(g, a):
        return a.reshape(g["x"].shape[0] * g["x"].shape[1], a.shape[-1])

    def rows3d(g, a):
        return a.reshape(g["x"].shape[0], g["x"].shape[1], a.shape[-1])

    def project(w2d, k_block, n, tn, w_t=None):
        outs = _mm([rows2d(main, main["h"])], w2d, k_block=k_block, n0=0, n=n, tn=tn, tm=tm,
                   rider=[rows2d(rider, rider["h"])], w_t=w_t)
        return [rows3d(g, o) for g, o in zip(groups, outs)]

    for g in groups:
        g["rope"] = _rope_tables(g["pos"])
        g["states"] = ([], [], [], [], [])
        (g["h"],) = _norm_call(g["x"], mods5, g["boff"], pre=(norm_pre[0, 0], 0, 0, 1))
    for layer in range(depth):
        j = layer // 2
        for sub in range(3):
            if sub == 1:
                if layer % 2 == 0:
                    w_zgb_t = p["w_zgb_t"][j]
                    projs = project(None, j, 3 * a_dim + 3 * b_dim, 1024, w_t=p["ab_in_t"])
                    zgbs = project(None, 0, w_zgb_t.shape[1], w_zgb_t.shape[1], w_t=w_zgb_t)
                    for g, proj, zgb in zip(groups, projs, zgbs):
                        bsz, past = g["x"].shape[0], g["past"]
                        if past is None:
                            kv_past = None
                            s0 = jnp.zeros((bsz, B_HEADS, HEAD_DIM, HEAD_DIM), F32)
                            buf = jnp.zeros((bsz, B_CONV - 1, 3 * b_dim), F32)
                        else:
                            kv_past, s0, buf = (past[0], past[1]), past[2][j], past[3][j]
                        g["mixed"], new = _ab_mixer(proj, zgb, g["t_valid"], g["rope"], kv_past, s0, buf, j,
                                                    p["b_conv_w"][j], p["b_a_log"][j], p["b_dt_bias"][j],
                                                    p["b_norm"][j])
                        for dst, val in zip(g["states"][:4], new):
                            dst.append(val)
                else:
                    projs = project(c_in2d, j, c_in2d.shape[1], 1024)
                    for g, proj in zip(groups, projs):
                        bsz, past = g["x"].shape[0], g["past"]
                        s0 = jnp.zeros((bsz, C_HEADS, HEAD_DIM, HEAD_DIM), F32) if past is None else past[4][j]
                        g["mixed"], s_new = _hgrn_mixer(proj, g["t_valid"], s0, lb_all[j], p["c_norm"][j])
                        g["states"][4].append(s_new)
                w_out2d = ab_out2d if layer % 2 == 0 else c_out2d
                post = (norm_post[layer, 1], layer, 5, 1.0)
                pre = (norm_pre[layer, 2], layer, 6, 7)
                for g in groups:
                    if g["x"].shape[1] >= _MM_NORM_MIN_ROWS:
                        g["x"], g["h"] = _mm_norm(g["mixed"], w_out2d, j, g["x"], mods5, g["boff"], post=post, pre=pre)
                    else:
                        y = _mm(g["mixed"], w_out2d, k_block=j, n0=0, n=d, tn=1024, tm=tm)
                        g["x"], g["h"] = _norm_call(g["x"], mods5, g["boff"], post=(rows3d(g, y),) + post, pre=pre)
                continue
            k_ffn = layer * 2 + (0 if sub == 0 else 1)
            acts = _swiglu_in(rows2d(main, main["h"]), wi2d, k_ffn, 2 * tm, rider=rows2d(rider, rider["h"]))
            ys = _mm([acts[0]], wo2d, k_block=k_ffn, n0=0, n=d, tn=512, tm=512, rider=[acts[1]])
            if sub == 0:
                pre = (norm_pre[layer, 1], layer, 3, 4)
            elif layer + 1 < depth:
                pre = (norm_pre[layer + 1, 0], layer + 1, 0, 1)
            else:
                pre = None
            for g, y in zip(groups, ys):
                post = (rows3d(g, y), norm_post[layer, sub], layer, 3 * sub + 2, FFN_RESIDUAL)
                outs = _norm_call(g["x"], mods5, g["boff"], post=post, pre=pre)
                g["x"] = outs[0]
                g["h"] = outs[1] if pre is not None else None
    return [(g["x"],) + tuple(jnp.stack(s) for s in g["states"]) for g in groups]


def kernel(x_prompt, x_sample, cache_a_k, cache_a_v, state_b_s, state_b_conv, state_c_s, c_prompt, c_sample,
           ada_w, ada_b, norm_pre, norm_post, ffn_wi, ffn_wo, ab_w_in, ab_w_out, b_conv_w, b_a_log, b_dt_bias,
           b_norm, c_w_in, c_w_out, c_lower_bounds, c_norm):
    p = dict(norm_pre=norm_pre, norm_post=norm_post, ffn_wi=ffn_wi, ffn_wo=ffn_wo, ab_w_in=ab_w_in,
             ab_w_out=ab_w_out, b_conv_w=b_conv_w, b_a_log=b_a_log, b_dt_bias=b_dt_bias, b_norm=b_norm,
             c_w_in=c_w_in, c_w_out=c_w_out, c_lower_bounds=c_lower_bounds, c_norm=c_norm)
    depth, d = norm_pre.shape[0], x_prompt.shape[2]
    main_cols = (3 * A_HEADS + 3 * B_HEADS) * HEAD_DIM
    gb_cols = 2 * B_HEADS
    ab_in_t = jnp.swapaxes(ab_w_in, 1, 2)
    p["ab_in_t"] = ab_in_t
    p["w_zgb_t"] = [jnp.concatenate([ab_in_t[j, main_cols + gb_cols:], ab_in_t[j, main_cols:main_cols + gb_cols],
                                     jnp.zeros((LANES - gb_cols, d), F32)], axis=0)[None]
                    for j in range(ab_w_in.shape[0])]
    n_p, t_p = x_prompt.shape[0], x_prompt.shape[1]
    n_s, t_s = x_sample.shape[0], x_sample.shape[1]
    past_len = 16384

    rows = -(-(n_p + n_s) // (2 * SUBLANES)) * (2 * SUBLANES)
    c_all = jnp.concatenate([c_prompt, c_sample, jnp.zeros((rows - n_p - n_s, d), F32)], axis=0)
    mods5 = _ada_mods(c_all, ada_w, ada_b).reshape(depth, rows, N_MOD, 1, d)

    prompt = dict(x=x_prompt, t_valid=t_p, pos=jnp.arange(t_p, dtype=jnp.int32), boff=0, past=None)
    sample = dict(x=jnp.pad(x_sample, ((0, 0), (0, SAMPLE_T_PAD - t_s), (0, 0))), t_valid=t_s,
                  pos=past_len + jnp.arange(SAMPLE_T_PAD, dtype=jnp.int32), boff=n_p,
                  past=(cache_a_k, cache_a_v, state_b_s, state_b_conv, state_c_s))
    (y_p, ak_p, av_p, bs_p, bc_p, cs_p), (y_s, ak_s, av_s, bs_s, bc_s, cs_s) = _trunk((prompt, sample), mods5, p)
    return (y_p, y_s[:, :t_s], ak_p, av_p, bs_p, bc_p, cs_p, ak_s, av_s, bs_s, bc_s, cs_s)
```
